```python
import jax
import jax.numpy as jnp
from jax import lax
import numpy as np

D_MODEL = 1024
BATCH = 32
SEQ = 2048
DEPTH = 2

GRID_W = 64
CTX_LEN = 256
N_MIXERS = 2
N_RET_LAYERS = (DEPTH + N_MIXERS - 1) // N_MIXERS
N_ATT_LAYERS = DEPTH // N_MIXERS
N_MOD = 6
NORM_EPS = 1e-6
HEAD_NORM_EPS = 1e-5
ROPE_THETA = 10000.0

RET_HEADS = 4
RET_DK = D_MODEL // RET_HEADS
RET_DV = 2 * RET_DK
RET_QK = RET_HEADS * RET_DK
RET_V = RET_HEADS * RET_DV
RET_QKV = 2 * RET_QK + RET_V
RET_IN = RET_QKV + RET_V
RET_CHUNK = 128

ATT_HEAD_DIM = 128
ATT_Q_HEADS = D_MODEL // ATT_HEAD_DIM
ATT_KV_HEADS = 2
ATT_GROUP = ATT_Q_HEADS // ATT_KV_HEADS
ATT_QW = ATT_Q_HEADS * ATT_HEAD_DIM
ATT_KW = ATT_KV_HEADS * ATT_HEAD_DIM
ATT_IN = ATT_QW + 2 * ATT_KW
Q_BLOCK = 128

N_GROUPS = 4
EXPERTS_PER_GROUP = 8
N_EXPERTS = N_GROUPS * EXPERTS_PER_GROUP
TOP_K_IN_GROUP = 2
EXPERT_HIDDEN = D_MODEL // 2
EXPERT_BLOCK = 256

kernel_name = "hybrid_retention_gqa_hmoe_dit"


def rms_norm(x, eps=NORM_EPS):
    xf = x.astype(jnp.float32)
    return (xf * lax.rsqrt(jnp.mean(xf * xf, axis=-1, keepdims=True) + eps)).astype(x.dtype)


def head_norm(y):
    yf = y.astype(jnp.float32)
    mu = jnp.mean(yf, axis=-1, keepdims=True)
    var = jnp.mean(jnp.square(yf - mu), axis=-1, keepdims=True)
    return ((yf - mu) * lax.rsqrt(var + HEAD_NORM_EPS)).astype(y.dtype)


def modulate(x, shift, scale):
    return rms_norm(x) * (1 + scale) + shift


def axial_rope_tables(n, head_dim, dtype):
    rows = n // GRID_W
    r = jnp.repeat(jnp.arange(rows), GRID_W).astype(jnp.float32)
    col = jnp.tile(jnp.arange(GRID_W), rows).astype(jnp.float32)
    nf = head_dim // 4
    inv = ROPE_THETA ** (-jnp.arange(nf, dtype=jnp.float32) / nf)
    ar = r[:, None] * inv
    ac = col[:, None] * inv
    ang = jnp.concatenate([ar, ar, ac, ac], axis=-1)
    return jnp.cos(ang).astype(dtype), jnp.sin(ang).astype(dtype)


def rotate_half(u):
    a, b = jnp.split(u, 2, axis=-1)
    return jnp.concatenate([-b, a], axis=-1)


def apply_axial_rope(x, cos, sin):
    half = x.shape[-1] // 2
    x_rot = jnp.concatenate([rotate_half(x[..., :half]), rotate_half(x[..., half:])], axis=-1)
    return x * cos[:, None, :] + x_rot * sin[:, None, :]


def chunk_retention(q, k, v, log_g, state0, with_output):
    B, H, T, dk = q.shape
    dv = v.shape[-1]
    C = RET_CHUNK
    n = T // C
    lg = log_g.astype(jnp.float32)
    pos = jnp.arange(C, dtype=jnp.float32)
    zeta = jnp.exp(lg[:, None] * (C - 1 - pos))
    chunk_decay = jnp.exp(lg * C)
    qc = q.reshape(B, H, n, C, dk)
    kc = k.reshape(B, H, n, C, dk)
    vc = v.reshape(B, H, n, C, dv)
    kz = kc * zeta[:, None, :, None].astype(k.dtype)

    def step(state, xs):
        q_i, kz_i, v_i = xs
        cross = jnp.einsum('bhcd,bhde->bhce', q_i.astype(jnp.float32), state) if with_output else None
        upd = jnp.einsum('bhcd,bhce->bhde', kz_i, v_i).astype(jnp.float32)
        return state * chunk_decay[:, None, None] + upd, cross

    final_state, crosses = lax.scan(
        step, state0, (jnp.moveaxis(qc, 2, 0), jnp.moveaxis(kz, 2, 0), jnp.moveaxis(vc, 2, 0)))
    if not with_output:
        return final_state, None
    diff = pos[:, None] - pos[None, :]
    dmask = jnp.where(diff >= 0, jnp.exp(lg[:, None, None] * jnp.maximum(diff, 0.0)), 0.0)
    scores = jnp.einsum('bhnid,bhnjd->bhnij', qc, kc) * dmask[:, None].astype(q.dtype)
    inner = jnp.einsum('bhnij,bhnje->bhnie', scores, vc)
    xi = jnp.exp(lg[:, None] * (pos + 1))
    cross = (jnp.moveaxis(crosses, 0, 2) * xi[:, None, :, None]).astype(v.dtype)
    return final_state, (inner + cross).reshape(B, H, T, dv)


def _ret_project(u, w_in, with_gate):
    B, n, _ = u.shape
    p = u @ (w_in if with_gate else w_in[:, :RET_QKV])
    q = p[..., :RET_QK].reshape(B, n, RET_HEADS, RET_DK)
    k = p[..., RET_QK:2 * RET_QK].reshape(B, n, RET_HEADS, RET_DK) * (RET_DK ** -0.5)
    v = p[..., 2 * RET_QK:RET_QKV].reshape(B, n, RET_HEADS, RET_DV)
    g = p[..., RET_QKV:] if with_gate else None
    return q, k, v, g


def retention_mixer(h, hc, w_in, w_out, log_g_fwd, log_g_bwd, need_ctx):
    B, S, _ = h.shape
    L = hc.shape[1]
    q, k, v, g = _ret_project(h, w_in, True)
    cos, sin = axial_rope_tables(S, RET_DK, h.dtype)
    q = apply_axial_rope(q, cos, sin)
    k = apply_axial_rope(k, cos, sin)
    qc, kc, vc, gc = _ret_project(hc, w_in, need_ctx)
    to_bhtd = lambda u: u.transpose(0, 2, 1, 3)
    flip = lambda u: jnp.flip(u, axis=2)
    q, k, v = to_bhtd(q), to_bhtd(k), to_bhtd(v)
    qc, kc, vc = to_bhtd(qc), to_bhtd(kc), to_bhtd(vc)
    zero = jnp.zeros((B, RET_HEADS, RET_DK, RET_DV), jnp.float32)
    st_f, yc_f = chunk_retention(qc, kc, vc, log_g_fwd, zero, need_ctx)
    st_b, yc_b = chunk_retention(flip(qc), flip(kc), flip(vc), log_g_bwd, zero, need_ctx)
    _, y_f = chunk_retention(q, k, v, log_g_fwd, st_f, True)
    _, y_b = chunk_retention(flip(q), flip(k), flip(v), log_g_bwd, st_b, True)

    def finish(y, gate, n):
        y = head_norm(y).transpose(0, 2, 1, 3).reshape(B, n, RET_V)
        return (y * jax.nn.silu(gate)) @ w_out

    out = finish(y_f + flip(y_b), g, S)
    out_c = finish(yc_f + flip(yc_b), gc, L) if need_ctx else None
    return out, out_c


def _att_project(u, w_qkv, q_gain, k_gain, with_q):
    B, n, _ = u.shape
    p = u @ (w_qkv if with_q else w_qkv[:, ATT_QW:])
    if with_q:
        q = rms_norm(p[..., :ATT_QW].reshape(B, n, ATT_Q_HEADS, ATT_HEAD_DIM)) * q_gain
        p = p[..., ATT_QW:]
    else:
        q = None
    k = rms_norm(p[..., :ATT_KW].reshape(B, n, ATT_KV_HEADS, ATT_HEAD_DIM)) * k_gain
    v = p[..., ATT_KW:].reshape(B, n, ATT_KV_HEADS, ATT_HEAD_DIM)
    return q, k, v


def gqa_softmax(q, k, v):
    B, nq = q.shape[0], q.shape[1]
    qg = q.reshape(B, nq, ATT_KV_HEADS, ATT_GROUP, ATT_HEAD_DIM)
    s = jnp.einsum('bqkgd,bskd->bkgqs', qg, k).astype(jnp.float32) * (ATT_HEAD_DIM ** -0.5)
    p = jax.nn.softmax(s, axis=-1).astype(v.dtype)
    o = jnp.einsum('bkgqs,bskd->bqkgd', p, v)
    return o.reshape(B, nq, ATT_QW)


def blocked_attention(q, k, v):
    B, S = q.shape[0], q.shape[1]
    nb = S // Q_BLOCK
    qb = jnp.moveaxis(q.reshape(B, nb, Q_BLOCK, ATT_Q_HEADS, ATT_HEAD_DIM), 1, 0)
    o = lax.map(lambda qi: gqa_softmax(qi, k, v), qb)
    return jnp.moveaxis(o, 0, 1).reshape(B, S, ATT_QW)


def attention_mixer(h, hc, w_qkv, w_o, q_gain, k_gain, need_ctx):
    S = h.shape[1]
    q, k, v = _att_project(h, w_qkv, q_gain, k_gain, True)
    cos, sin = axial_rope_tables(S, ATT_HEAD_DIM, h.dtype)
    q = apply_axial_rope(q, cos, sin)
    k = apply_axial_rope(k, cos, sin)
    qc, kc, vc = _att_project(hc, w_qkv, q_gain, k_gain, need_ctx)
    k_all = jnp.concatenate([k, kc], axis=1)
    v_all = jnp.concatenate([v, vc], axis=1)
    out = blocked_attention(q, k_all, v_all) @ w_o
    out_c = gqa_softmax(qc, kc, vc) @ w_o if need_ctx else None
    return out, out_c


def grouped_expert_ffn(h, expert_ids, gates, w_gate, w_up, w_down):
    T, D = h.shape
    K = expert_ids.shape[1]
    A = T * K
    flat_e = expert_ids.reshape(A)
    order = jnp.argsort(flat_e)
    sorted_e = flat_e[order]
    counts = jnp.bincount(flat_e, length=N_EXPERTS)
    starts = jnp.cumsum(counts) - counts
    padded = (counts + EXPERT_BLOCK - 1) // EXPERT_BLOCK * EXPERT_BLOCK
    ends_p = jnp.cumsum(padded)
    starts_p = ends_p - padded
    dest = starts_p[sorted_e] + (jnp.arange(A) - starts[sorted_e])
    n_blocks = -(-A // EXPERT_BLOCK) + N_EXPERTS
    slot_tok = jnp.full((n_blocks * EXPERT_BLOCK,), T, dtype=jnp.int32).at[dest].set(
        (order // K).astype(jnp.int32))
    block_e = jnp.minimum(
        jnp.searchsorted(ends_p, jnp.arange(n_blocks) * EXPERT_BLOCK, side='right'), N_EXPERTS - 1)
    h_pad = jnp.concatenate([h, jnp.zeros((1, D), h.dtype)], axis=0)

    def run_block(args):
        toks, e = args
        xb = h_pad[toks]
        return (jax.nn.silu(xb @ w_gate[e]) * (xb @ w_up[e])) @ w_down[e]

    y_slots = lax.map(run_block, (slot_tok.reshape(n_blocks, EXPERT_BLOCK), block_e))
    y_sorted = y_slots.reshape(-1, D)[dest]
    y_assign = jnp.zeros_like(y_sorted).at[order].set(y_sorted).reshape(T, K, D)
    return jnp.sum(y_assign * gates[..., None].astype(h.dtype), axis=1)


def hier_moe(h, w_group, b_group, w_expert, b_expert, w_gate, w_up, w_down):
    T = h.shape[0]
    glog = (h @ w_group).astype(jnp.float32) + b_group.astype(jnp.float32)
    gprob = jax.nn.softmax(glog, axis=-1)
    g_idx = jnp.argmax(glog, axis=-1)
    g_p = jnp.take_along_axis(gprob, g_idx[:, None], axis=-1)
    elog = ((h @ w_expert).astype(jnp.float32) + b_expert.astype(jnp.float32)).reshape(
        T, N_GROUPS, EXPERTS_PER_GROUP)
    elog = jnp.take_along_axis(elog, g_idx[:, None, None], axis=1)[:, 0]
    top_p, top_e = lax.top_k(jax.nn.softmax(elog, axis=-1), TOP_K_IN_GROUP)
    gates = g_p * top_p / jnp.sum(top_p, axis=-1, keepdims=True)
    expert_ids = g_idx[:, None] * EXPERTS_PER_GROUP + top_e
    return grouped_expert_ffn(h, expert_ids, gates, w_gate, w_up, w_down)


def setup_inputs(seed: int = 0) -> dict:
    key = jax.random.key(seed)
    ks = jax.random.split(key, 22)
    f32 = jnp.float32
    D = D_MODEL

    def normal(k, shape, scale):
        return jax.random.normal(k, shape, f32) * scale

    heads = jnp.arange(RET_HEADS, dtype=f32)
    base_log_decay = jnp.log1p(-(2.0 ** (-5.0 - heads)))
    return {
        'x': normal(ks[0], (BATCH, SEQ, D), 1.0),
        'c': normal(ks[1], (BATCH, D), 1.0),
        'ctx': normal(ks[2], (BATCH, CTX_LEN, D), 1.0),
        'c_ctx': normal(ks[3], (D,), 1.0),
        'w_mod': normal(ks[4], (DEPTH, D, N_MOD * D), 0.5 * D ** -0.5),
        'b_mod': normal(ks[5], (DEPTH, N_MOD * D), 0.02),
        'ret_w_in': normal(ks[6], (N_RET_LAYERS, D, RET_IN), D ** -0.5),
        'ret_w_out': normal(ks[7], (N_RET_LAYERS, RET_V, D), RET_V ** -0.5),
        'ret_log_decay_fwd': base_log_decay[None] * (1.0 + 0.05 * normal(ks[8], (N_RET_LAYERS, RET_HEADS), 1.0)),
        'ret_log_decay_bwd': base_log_decay[None] * (1.0 + 0.05 * normal(ks[9], (N_RET_LAYERS, RET_HEADS), 1.0)),
        'att_w_qkv': normal(ks[10], (N_ATT_LAYERS, D, ATT_IN), D ** -0.5),
        'att_w_o': normal(ks[11], (N_ATT_LAYERS, ATT_QW, D), ATT_QW ** -0.5),
        'att_q_gain': 1.0 + normal(ks[12], (N_ATT_LAYERS, ATT_HEAD_DIM), 0.02),
        'att_k_gain': 1.0 + normal(ks[13], (N_ATT_LAYERS, ATT_HEAD_DIM), 0.02),
        'moe_w_group': normal(ks[14], (DEPTH, D, N_GROUPS), D ** -0.5),
        'moe_b_group': normal(ks[15], (DEPTH, N_GROUPS), 0.01),
        'moe_w_expert': normal(ks[16], (DEPTH, D, N_EXPERTS), D ** -0.5),
        'moe_b_expert': normal(ks[17], (DEPTH, N_EXPERTS), 0.01),
        'moe_w_gate': normal(ks[18], (DEPTH, N_EXPERTS, D, EXPERT_HIDDEN), D ** -0.5),
        'moe_w_up': normal(ks[19], (DEPTH, N_EXPERTS, D, EXPERT_HIDDEN), D ** -0.5),
        'moe_w_down': normal(ks[20], (DEPTH, N_EXPERTS, EXPERT_HIDDEN, D), EXPERT_HIDDEN ** -0.5),
        'final_norm_gain': 1.0 + normal(ks[21], (D,), 0.02),
    }


def reference(x, c, ctx, c_ctx, w_mod, b_mod, ret_w_in, ret_w_out, ret_log_decay_fwd,
              ret_log_decay_bwd, att_w_qkv, att_w_o, att_q_gain, att_k_gain, moe_w_group,
              moe_b_group, moe_w_expert, moe_b_expert, moe_w_gate, moe_w_up, moe_w_down,
              final_norm_gain):
    B, S, D = x.shape
    L = ctx.shape[1]
    xc = ctx
    for i in range(DEPTH):
        need_ctx = i < DEPTH - 1
        j = i // N_MIXERS
        m = jax.nn.silu(c) @ w_mod[i] + b_mod[i]
        mc = jax.nn.silu(c_ctx) @ w_mod[i] + b_mod[i]
        sh1, sc1, g1, sh2, sc2, g2 = jnp.split(m[:, None, :], N_MOD, axis=-1)
        csh1, csc1, cg1, csh2, csc2, cg2 = jnp.split(mc, N_MOD, axis=-1)
        h = modulate(x, sh1, sc1)
        hc = modulate(xc, csh1, csc1)
        if i % N_MIXERS == 0:
            y, yc = retention_mixer(h, hc, ret_w_in[j], ret_w_out[j], ret_log_decay_fwd[j],
                                    ret_log_decay_bwd[j], need_ctx)
        else:
            y, yc = attention_mixer(h, hc, att_w_qkv[j], att_w_o[j], att_q_gain[j],
                                    att_k_gain[j], need_ctx)
        x = x + g1 * y
        h2 = modulate(x, sh2, sc2)
        moe_args = (moe_w_group[i], moe_b_group[i], moe_w_expert[i], moe_b_expert[i],
                    moe_w_gate[i], moe_w_up[i], moe_w_down[i])
        if need_ctx:
            xc = xc + cg1 * yc
            h2c = modulate(xc, csh2, csc2)
            f = hier_moe(jnp.concatenate([h2.reshape(B * S, D), h2c.reshape(B * L, D)], axis=0), *moe_args)
            x = x + g2 * f[:B * S].reshape(B, S, D)
            xc = xc + cg2 * f[B * S:].reshape(B, L, D)
        else:
            x = x + g2 * hier_moe(h2.reshape(B * S, D), *moe_args).reshape(B, S, D)
    return rms_norm(x) * final_norm_gain
```

```python
import functools

import jax
import jax.numpy as jnp
from jax import lax
from jax.experimental import pallas as pl
from jax.experimental.pallas import tpu as pltpu

F32 = jnp.float32
BF16 = jnp.bfloat16

D_MODEL = 1024
DEPTH = 2
GRID_W = 64
N_MOD = 6
NORM_EPS = 1e-6
HEAD_NORM_EPS = 1e-5
ROPE_THETA = 10000.0

RET_HEADS = 4
RET_DK = D_MODEL // RET_HEADS
RET_DV = 2 * RET_DK
RET_QK = RET_HEADS * RET_DK
RET_V = RET_HEADS * RET_DV
RET_IN = 2 * RET_QK + 2 * RET_V

ATT_HEAD_DIM = 128
ATT_Q_HEADS = D_MODEL // ATT_HEAD_DIM
ATT_KV_HEADS = 2
ATT_GROUP = ATT_Q_HEADS // ATT_KV_HEADS
ATT_QW = ATT_Q_HEADS * ATT_HEAD_DIM
ATT_KW = ATT_KV_HEADS * ATT_HEAD_DIM
ATT_IN = ATT_QW + 2 * ATT_KW

N_GROUPS = 4
EXPERTS_PER_GROUP = 8
N_EXPERTS = N_GROUPS * EXPERTS_PER_GROUP
EXPERT_HIDDEN = D_MODEL // 2

LANES = 128
ROW_TILE = 512
RET_CHUNK = 256
ATT_Q_TILE = 256
EXPERT_BLOCK = 256
MOD_ROWS_PAD = 8
VMEM_LIMIT = 56 * 1024 * 1024


def _cparams(n_axes):
    return pltpu.CompilerParams(
        dimension_semantics=("arbitrary",) * n_axes, vmem_limit_bytes=VMEM_LIMIT)


def _silu(v):
    return v * (1.0 / (1.0 + jnp.exp(-v)))


def _modulate(x, shift, scale):
    ms = jnp.mean(x * x, axis=-1, keepdims=True)
    return x * lax.rsqrt(ms + NORM_EPS) * (1.0 + scale) + shift


def _split_bf16(v):
    hi = v.astype(BF16)
    lo = (v - hi.astype(F32)).astype(BF16)
    return hi, lo


def _dot(a, b):
    return jnp.dot(a, b, preferred_element_type=F32)


def _dot_nt(a, b):
    return lax.dot_general(a, b, (((1,), (1,)), ((), ())), preferred_element_type=F32)


def _mod_kernel(c_ref, w_ref, b_ref, o_ref):
    a_hi, a_lo = _split_bf16(_silu(c_ref[...]))
    w_hi, w_lo = _split_bf16(w_ref[0])
    acc = _dot(a_hi, w_hi) + _dot(a_lo, w_hi) + _dot(a_hi, w_lo)
    o_ref[0] = acc + b_ref[0]


def _mod_vectors(c_rows, w_mod, b_mod):
    rows = c_rows.shape[0]
    n = w_mod.shape[-1]
    tn = 1024
    out = pl.pallas_call(
        _mod_kernel,
        grid=(DEPTH, n // tn),
        in_specs=[
            pl.BlockSpec((rows, D_MODEL), lambda l, j: (0, 0)),
            pl.BlockSpec((1, D_MODEL, tn), lambda l, j: (l, 0, j)),
            pl.BlockSpec((1, 1, tn), lambda l, j: (l, 0, j)),
        ],
        out_specs=pl.BlockSpec((1, rows, tn), lambda l, j: (l, 0, j)),
        out_shape=jax.ShapeDtypeStruct((DEPTH, rows, n), F32),
        compiler_params=_cparams(2),
        name="mod_vectors",
    )(c_rows, w_mod, b_mod.reshape(DEPTH, 1, n))
    return out.reshape(DEPTH, rows, N_MOD, D_MODEL)


def _rope_angles(n, head_dim):
    rows = n // GRID_W
    r = jnp.repeat(jnp.arange(rows), GRID_W).astype(F32)
    col = jnp.tile(jnp.arange(GRID_W), rows).astype(F32)
    nf = head_dim // 4
    inv = ROPE_THETA ** (-jnp.arange(nf, dtype=F32) / nf)
    ar = r[:, None] * inv
    ac = col[:, None] * inv
    return jnp.concatenate([ar, ar, ac, ac], axis=-1)


def _pad_identity(cos, sins, pad):
    cos = jnp.concatenate([cos, jnp.ones((pad, cos.shape[1]), F32)], axis=0)
    sins = [jnp.concatenate([s, jnp.zeros((pad, s.shape[1]), F32)], axis=0) for s in sins]
    return cos, sins


def _ret_rope_tables(n, pad):
    ang = _rope_angles(n, RET_DK)
    lane = jnp.arange(RET_DK) % LANES
    sin = jnp.where(lane < LANES // 2, -jnp.sin(ang), jnp.sin(ang))
    cos, (sin,) = _pad_identity(jnp.cos(ang), [sin], pad)
    return cos, sin


def _att_rope_tables(n, pad):
    ang = _rope_angles(n, ATT_HEAD_DIM)
    lane = jnp.arange(ATT_HEAD_DIM) % (ATT_HEAD_DIM // 2)
    first = lane < ATT_HEAD_DIM // 4
    sin = jnp.sin(ang)
    sin_a = jnp.where(first, -sin, 0.0)
    sin_b = jnp.where(first, 0.0, sin)
    cos, (sin_a, sin_b) = _pad_identity(jnp.cos(ang), [sin_a, sin_b], pad)
    return cos, sin_a, sin_b


def _proj_ret_kernel(x_ref, m_ref, w_ref, cos_ref, sin_ref, o_ref, h_ref):
    j = pl.program_id(1)

    @pl.when(j == 0)
    def _():
        h_ref[...] = _modulate(x_ref[...], m_ref[0, 0:1, :], m_ref[0, 1:2, :]).astype(BF16)

    acc = _dot(h_ref[...], w_ref[...])

    @pl.when(j <= 1)
    def _():
        scale = jnp.where(j == 1, RET_DK ** -0.5, 1.0).astype(F32)
        for g in range(acc.shape[1] // LANES):
            xs = acc[:, g * LANES:(g + 1) * LANES]
            t = (g % 2) * LANES
            r = xs * cos_ref[:, t:t + LANES] + pltpu.roll(xs, LANES // 2, 1) * sin_ref[:, t:t + LANES]
            o_ref[:, g * LANES:(g + 1) * LANES] = (r * scale).astype(BF16)

    @pl.when((j == 2) | (j == 3))
    def _():
        o_ref[...] = acc.astype(BF16)

    @pl.when(j >= 4)
    def _():
        o_ref[...] = _silu(acc).astype(BF16)


def _proj_ret(xs, mod, w_in, cos, sin, geo):
    t_rows, nl, tpb, n_batch = geo
    tm, tn = ROW_TILE, 1024
    nt = t_rows // tm
    return pl.pallas_call(
        _proj_ret_kernel,
        grid=(nt, RET_IN // tn),
        in_specs=[
            pl.BlockSpec((tm, D_MODEL), lambda i, j: (i, 0)),
            pl.BlockSpec((1, N_MOD, D_MODEL), lambda i, j: (jnp.where(i < nl, i // tpb, n_batch), 0, 0)),
            pl.BlockSpec((D_MODEL, tn), lambda i, j: (0, j)),
            pl.BlockSpec((tm, RET_DK), lambda i, j: (jnp.where(i < nl, i % tpb, tpb), 0)),
            pl.BlockSpec((tm, RET_DK), lambda i, j: (jnp.where(i < nl, i % tpb, tpb), 0)),
        ],
        out_specs=pl.BlockSpec((tm, tn), lambda i, j: (i, j)),
        out_shape=jax.ShapeDtypeStruct((t_rows, RET_IN), BF16),
        scratch_shapes=[pltpu.VMEM((tm, D_MODEL), BF16)],
        compiler_params=_cparams(2),
        name="proj_ret",
    )(xs, mod, w_in, cos, sin)


def _ret_kernel(lg_ref, ql, kl, vl, gl, qc, kc, vc, gc, ol, oc,
                sf_ref, sb_ref, y_ref, mask_ref):
    hd = pl.program_id(1)
    lgf = lg_ref[0, hd]
    lgb = lg_ref[1, hd]
    c = RET_CHUNK
    ii = lax.broadcasted_iota(jnp.int32, (c, c), 0).astype(F32)
    jj = lax.broadcasted_iota(jnp.int32, (c, c), 1).astype(F32)
    d = ii - jj
    mask_ref[...] = (jnp.where(d >= 0, jnp.exp(lgf * jnp.maximum(d, 0.0)), 0.0)
                     + jnp.where(d <= 0, jnp.exp(lgb * jnp.maximum(-d, 0.0)), 0.0))
    pos = lax.broadcasted_iota(jnp.int32, (c, 1), 0).astype(F32)
    xi_f = jnp.exp(lgf * (pos + 1.0))
    zeta_f = jnp.exp(lgf * (c - 1.0 - pos))
    xi_b = jnp.exp(lgb * (c - pos))
    zeta_b = jnp.exp(lgb * pos)
    dec_f = jnp.exp(jnp.full((1, RET_DV), lgf * c, F32))
    dec_b = jnp.exp(jnp.full((1, RET_DV), lgb * c, F32))

    def state_update(s_ref, k, v, zeta, dec):
        kz = (k.astype(F32) * zeta).T.astype(BF16)
        s_ref[...] = s_ref[...] * dec + _dot(kz, v)

    def run(q_ref, k_ref, v_ref, g_ref, o_ref, n):
        def fwd(t, carry):
            r0 = pl.multiple_of(t * c, c)
            q = q_ref[pl.ds(r0, c), :]
            k = k_ref[pl.ds(r0, c), :]
            v = v_ref[pl.ds(r0, c), :]
            a = (_dot_nt(q, k) * mask_ref[...]).astype(BF16)
            cross = _dot(q, sf_ref[...].astype(BF16)) * xi_f
            y_ref[pl.ds(r0, c), :] = _dot(a, v) + cross
            state_update(sf_ref, k, v, zeta_f, dec_f)
            return carry

        lax.fori_loop(0, n, fwd, 0)

        def bwd(t, carry):
            r0 = pl.multiple_of((n - 1 - t) * c, c)
            q = q_ref[pl.ds(r0, c), :]
            k = k_ref[pl.ds(r0, c), :]
            v = v_ref[pl.ds(r0, c), :]
            y = y_ref[pl.ds(r0, c), :] + _dot(q, sb_ref[...].astype(BF16)) * xi_b
            mu = jnp.mean(y, axis=-1, keepdims=True)
            yc = y - mu
            var = jnp.mean(yc * yc, axis=-1, keepdims=True)
            yn = yc * lax.rsqrt(var + HEAD_NORM_EPS)
            o_ref[pl.ds(r0, c), :] = (yn * g_ref[pl.ds(r0, c), :].astype(F32)).astype(BF16)
            state_update(sb_ref, k, v, zeta_b, dec_b)
            return carry

        lax.fori_loop(0, n, bwd, 0)

    sf_ref[...] = jnp.zeros_like(sf_ref)
    sb_ref[...] = jnp.zeros_like(sb_ref)
    run(qc, kc, vc, gc, oc, qc.shape[0] // c)
    run(ql, kl, vl, gl, ol, ql.shape[0] // c)


def _retention(p, lg, n_batch, seq, ctx_len):
    cb = n_batch * seq // ctx_len
    qo, ko = 0, RET_QK // RET_DK
    vo, go = 2 * RET_QK // RET_DV, (2 * RET_QK + RET_V) // RET_DV
    lat = lambda w, off: pl.BlockSpec((seq, w), lambda b, h, lg_: (b, off + h))
    ctx = lambda w, off: pl.BlockSpec((ctx_len, w), lambda b, h, lg_: (cb + b, off + h))
    return pl.pallas_call(
        _ret_kernel,
        grid_spec=pltpu.PrefetchScalarGridSpec(
            num_scalar_prefetch=1,
            grid=(n_batch, RET_HEADS),
            in_specs=[lat(RET_DK, qo), lat(RET_DK, ko), lat(RET_DV, vo), lat(RET_DV, go),
                      ctx(RET_DK, qo), ctx(RET_DK, ko), ctx(RET_DV, vo), ctx(RET_DV, go)],
            out_specs=[pl.BlockSpec((seq, RET_DV), lambda b, h, lg_: (b, h)),
                       pl.BlockSpec((ctx_len, RET_DV), lambda b, h, lg_: (b, h))],
            scratch_shapes=[pltpu.VMEM((RET_DK, RET_DV), F32), pltpu.VMEM((RET_DK, RET_DV), F32),
                            pltpu.VMEM((seq, RET_DV), F32), pltpu.VMEM((RET_CHUNK, RET_CHUNK), F32)],
        ),
        out_shape=[jax.ShapeDtypeStruct((n_batch * seq, RET_V), BF16),
                   jax.ShapeDtypeStruct((n_batch * ctx_len, RET_V), BF16)],
        compiler_params=_cparams(2),
        name="retention",
    )(lg, p, p, p, p, p, p, p, p)


def _route(h2, wr_ref, br_ref):
    hi, lo = _split_bf16(h2)
    l1 = _dot(hi, wr_ref[...])
    logits = l1[:, :LANES] + l1[:, LANES:] + _dot(lo, wr_ref[:, :LANES]) + br_ref[...]
    lane = lax.broadcasted_iota(jnp.int32, logits.shape, 1).astype(F32)
    neg = -jnp.inf
    big = float(LANES)
    is_g = lane < N_GROUPS
    gl = jnp.where(is_g, logits, neg)
    gm = jnp.max(gl, axis=-1, keepdims=True)
    gidx = jnp.min(jnp.where(gl == gm, lane, big), axis=-1, keepdims=True)
    gsum = jnp.sum(jnp.where(is_g, jnp.exp(jnp.where(is_g, logits, gm) - gm), 0.0), axis=-1, keepdims=True)
    g_p = 1.0 / gsum
    lo_l = N_GROUPS + EXPERTS_PER_GROUP * gidx
    in_grp = (lane >= lo_l) & (lane < lo_l + EXPERTS_PER_GROUP)
    el = jnp.where(in_grp, logits, neg)
    e1 = jnp.max(el, axis=-1, keepdims=True)
    i1 = jnp.min(jnp.where(el == e1, lane, big), axis=-1, keepdims=True)
    el2 = jnp.where(lane == i1, neg, el)
    e2 = jnp.max(el2, axis=-1, keepdims=True)
    i2 = jnp.min(jnp.where(el2 == e2, lane, big), axis=-1, keepdims=True)
    t = jnp.exp(e2 - e1)
    ga = g_p / (1.0 + t)
    gb = g_p * t / (1.0 + t)
    return jnp.where(lane == 0, i1 - N_GROUPS,
                     jnp.where(lane == 1, i2 - N_GROUPS,
                               jnp.where(lane == 2, ga, jnp.where(lane == 3, gb, 0.0))))


def _out_kernel(*refs, nl, has_ctx):
    if has_ctx:
        a_lat, a_ctx, x_ref, m_ref, w_ref, wr_ref, br_ref, x1_ref, h2_ref, r_ref = refs
        a = jnp.where(pl.program_id(0) < nl, a_lat[...], a_ctx[...])
    else:
        a_lat, x_ref, m_ref, w_ref, wr_ref, br_ref, x1_ref, h2_ref, r_ref = refs
        a = a_lat[...]
    x1 = x_ref[...] + m_ref[0, 2:3, :] * _dot(a, w_ref[...])
    x1_ref[...] = x1
    h2 = _modulate(x1, m_ref[0, 3:4, :], m_ref[0, 4:5, :])
    h2_ref[...] = h2.astype(BF16)
    r_ref[...] = _route(h2, wr_ref, br_ref)


def _mixer_out(a_lat, a_ctx, xs, mod, w_out, wr, br, geo, nt):
    t_rows, nl, tpb, n_batch = geo
    tm = ROW_TILE
    kdim = w_out.shape[0]
    has_ctx = a_ctx is not None
    mrow = lambda i: (jnp.where(i < nl, i // tpb, n_batch), 0, 0)
    in_specs = [pl.BlockSpec((tm, kdim), lambda i: (jnp.minimum(i, nl - 1), 0))]
    args = [a_lat]
    if has_ctx:
        in_specs.append(pl.BlockSpec((tm, kdim), lambda i: (jnp.maximum(i - nl, 0), 0)))
        args.append(a_ctx)
    in_specs += [
        pl.BlockSpec((tm, D_MODEL), lambda i: (i, 0)),
        pl.BlockSpec((1, N_MOD, D_MODEL), mrow),
        pl.BlockSpec((kdim, D_MODEL), lambda i: (0, 0)),
        pl.BlockSpec((D_MODEL, 2 * LANES), lambda i: (0, 0)),
        pl.BlockSpec((1, LANES), lambda i: (0, 0)),
    ]
    args += [xs, mod, w_out, wr, br]
    rows = nt * tm
    return pl.pallas_call(
        functools.partial(_out_kernel, nl=nl, has_ctx=has_ctx),
        grid=(nt,),
        in_specs=in_specs,
        out_specs=[pl.BlockSpec((tm, D_MODEL), lambda i: (i, 0)),
                   pl.BlockSpec((tm, D_MODEL), lambda i: (i, 0)),
                   pl.BlockSpec((tm, LANES), lambda i: (i, 0))],
        out_shape=[jax.ShapeDtypeStruct((rows, D_MODEL), F32),
                   jax.ShapeDtypeStruct((rows, D_MODEL), BF16),
                   jax.ShapeDtypeStruct((rows, LANES), F32)],
        compiler_params=_cparams(1),
        name="mixer_out_ctx" if has_ctx else "mixer_out",
    )(*args)


def _router_weights(w_group, b_group, w_expert, b_expert):
    w = jnp.concatenate([w_group, w_expert], axis=1)
    w = jnp.pad(w, ((0, 0), (0, LANES - w.shape[1])))
    hi = w.astype(BF16)
    lo = (w - hi.astype(F32)).astype(BF16)
    b = jnp.pad(jnp.concatenate([b_group, b_expert]), (0, LANES - N_GROUPS - N_EXPERTS))
    return jnp.concatenate([hi, lo], axis=1), b.reshape(1, LANES).astype(F32)


def _expert_kernel(be_ref, nv_ref, x_ref, wg_ref, wu_ref, wd_ref, o_ref):
    i = pl.program_id(0)

    @pl.when(i < nv_ref[0])
    def _():
        x = x_ref[...]
        mid = (_silu(_dot(x, wg_ref[0])) * _dot(x, wu_ref[0])).astype(BF16)
        o_ref[...] = _dot(mid, wd_ref[0]).astype(BF16)

    @pl.when(i >= nv_ref[0])
    def _():
        o_ref[...] = jnp.zeros_like(o_ref)


def _expert_ffn(xs_sorted, block_e, n_valid, w_gate, w_up, w_down):
    n_slots = xs_sorted.shape[0]
    blk = EXPERT_BLOCK
    return pl.pallas_call(
        _expert_kernel,
        grid_spec=pltpu.PrefetchScalarGridSpec(
            num_scalar_prefetch=2,
            grid=(n_slots // blk,),
            in_specs=[
                pl.BlockSpec((blk, D_MODEL), lambda i, be, nv: (i, 0)),
                pl.BlockSpec((1, D_MODEL, EXPERT_HIDDEN), lambda i, be, nv: (be[i], 0, 0)),
                pl.BlockSpec((1, D_MODEL, EXPERT_HIDDEN), lambda i, be, nv: (be[i], 0, 0)),
                pl.BlockSpec((1, EXPERT_HIDDEN, D_MODEL), lambda i, be, nv: (be[i], 0, 0)),
            ],
            out_specs=pl.BlockSpec((blk, D_MODEL), lambda i, be, nv: (i, 0)),
        ),
        out_shape=jax.ShapeDtypeStruct((n_slots, D_MODEL), BF16),
        compiler_params=_cparams(1),
        name="expert_ffn",
    )(block_e, n_valid, xs_sorted, w_gate, w_up, w_down)


def _dispatch_tables(route, n_tok):
    blk = EXPERT_BLOCK
    n_assign = 2 * n_tok
    flat_e = route[:n_tok, :2].astype(jnp.int32).reshape(n_assign)
    onehot = (flat_e[:, None] == jnp.arange(N_EXPERTS, dtype=jnp.int32)[None, :]).astype(jnp.int32)
    csum = jnp.cumsum(onehot, axis=0)
    rank = jnp.take_along_axis(csum, flat_e[:, None], axis=1)[:, 0] - 1
    counts = csum[-1]
    padded = (counts + blk - 1) // blk * blk
    ends_p = jnp.cumsum(padded)
    starts_p = ends_p - padded
    dest = starts_p[flat_e] + rank
    n_blocks = -(-n_assign // blk) + N_EXPERTS
    slot_tok = jnp.zeros((n_blocks * blk,), jnp.int32).at[dest].set(
        jnp.arange(n_assign, dtype=jnp.int32) // 2)
    block_e = jnp.minimum(
        jnp.searchsorted(ends_p, jnp.arange(n_blocks, dtype=jnp.int32) * blk, side='right'),
        N_EXPERTS - 1).astype(jnp.int32)
    n_valid = (ends_p[-1] // blk).astype(jnp.int32).reshape(1)
    return dest, slot_tok, block_e, n_valid


def _moe(h2, route, n_tok, w_gate, w_up, w_down):
    dest, slot_tok, block_e, n_valid = _dispatch_tables(route, n_tok)
    ys = _expert_ffn(h2[slot_tok], block_e, n_valid, w_gate, w_up, w_down)
    return ys[dest].reshape(n_tok, 2 * D_MODEL)


def _combine_kernel(*refs, final):
    if final:
        x1_ref, ya_ref, r_ref, m_ref, gain_ref, o_ref = refs
    else:
        x1_ref, ya_ref, r_ref, m_ref, o_ref = refs
    r = r_ref[...]
    f = (r[:, 2:3] * ya_ref[:, :D_MODEL].astype(F32)
         + r[:, 3:4] * ya_ref[:, D_MODEL:].astype(F32))
    x2 = x1_ref[...] + m_ref[0, 5:6, :] * f
    if final:
        ms = jnp.mean(x2 * x2, axis=-1, keepdims=True)
        x2 = x2 * lax.rsqrt(ms + NORM_EPS) * gain_ref[...]
    o_ref[...] = x2


def _combine(x1, ya, route, mod, geo, nt, gain=None):
    t_rows, nl, tpb, n_batch = geo
    tm = ROW_TILE
    final = gain is not None
    in_specs = [
        pl.BlockSpec((tm, D_MODEL), lambda i: (i, 0)),
        pl.BlockSpec((tm, 2 * D_MODEL), lambda i: (i, 0)),
        pl.BlockSpec((tm, LANES), lambda i: (i, 0)),
        pl.BlockSpec((1, N_MOD, D_MODEL), lambda i: (jnp.where(i < nl, i // tpb, n_batch), 0, 0)),
    ]
    args = [x1, ya, route, mod]
    if final:
        in_specs.append(pl.BlockSpec((1, D_MODEL), lambda i: (0, 0)))
        args.append(gain.reshape(1, D_MODEL))
    return pl.pallas_call(
        functools.partial(_combine_kernel, final=final),
        grid=(nt,),
        in_specs=in_specs,
        out_specs=pl.BlockSpec((tm, D_MODEL), lambda i: (i, 0)),
        out_shape=jax.ShapeDtypeStruct((nt * tm, D_MODEL), F32),
        compiler_params=_cparams(1),
        name="combine_final" if final else "combine",
    )(*args)


def _proj_att_kernel(x_ref, m_ref, w_ref, qg_ref, kg_ref, cos_ref, sa_ref, sb_ref, o_ref, h_ref):
    j = pl.program_id(1)

    @pl.when(j == 0)
    def _():
        h_ref[...] = _modulate(x_ref[...], m_ref[0, 0:1, :], m_ref[0, 1:2, :]).astype(BF16)

    acc = _dot(h_ref[...], w_ref[...])
    nq_tiles = ATT_QW // acc.shape[1]

    def normed_rope(xs, gain, scale):
        n = xs * lax.rsqrt(jnp.mean(xs * xs, axis=-1, keepdims=True) + NORM_EPS) * gain
        r = (n * cos_ref[...] + pltpu.roll(n, LANES - ATT_HEAD_DIM // 4, 1) * sa_ref[...]
             + pltpu.roll(n, ATT_HEAD_DIM // 4, 1) * sb_ref[...])
        return (r * scale).astype(BF16)

    @pl.when(j < nq_tiles)
    def _():
        for g in range(acc.shape[1] // LANES):
            o_ref[:, g * LANES:(g + 1) * LANES] = normed_rope(
                acc[:, g * LANES:(g + 1) * LANES], qg_ref[...], ATT_HEAD_DIM ** -0.5)

    @pl.when(j >= nq_tiles)
    def _():
        for g in range(ATT_KV_HEADS):
            o_ref[:, g * LANES:(g + 1) * LANES] = normed_rope(
                acc[:, g * LANES:(g + 1) * LANES], kg_ref[...], 1.0)
        o_ref[:, ATT_KW:] = acc[:, ATT_KW:].astype(BF16)


def _proj_att(xs, mod, w_qkv, q_gain, k_gain, cos, sin_a, sin_b, geo):
    t_rows, nl, tpb, n_batch = geo
    tm, tn = ROW_TILE, 2 * ATT_KW
    nt = t_rows // tm
    rope = pl.BlockSpec((tm, ATT_HEAD_DIM), lambda i, j: (jnp.where(i < nl, i % tpb, tpb), 0))
    vec = pl.BlockSpec((1, ATT_HEAD_DIM), lambda i, j: (0, 0))
    return pl.pallas_call(
        _proj_att_kernel,
        grid=(nt, ATT_IN // tn),
        in_specs=[
            pl.BlockSpec((tm, D_MODEL), lambda i, j: (i, 0)),
            pl.BlockSpec((1, N_MOD, D_MODEL), lambda i, j: (jnp.where(i < nl, i // tpb, n_batch), 0, 0)),
            pl.BlockSpec((D_MODEL, tn), lambda i, j: (0, j)),
            vec, vec, rope, rope, rope,
        ],
        out_specs=pl.BlockSpec((tm, tn), lambda i, j: (i, j)),
        out_shape=jax.ShapeDtypeStruct((t_rows, ATT_IN), BF16),
        scratch_shapes=[pltpu.VMEM((tm, D_MODEL), BF16)],
        compiler_params=_cparams(2),
        name="proj_att",
    )(xs, mod, w_qkv, q_gain.reshape(1, -1), k_gain.reshape(1, -1), cos, sin_a, sin_b)


def _att_kernel(q_ref, kl_ref, vl_ref, kc_ref, vc_ref, o_ref):
    tq = q_ref.shape[0]
    q = jnp.concatenate([q_ref[:, g * LANES:(g + 1) * LANES] for g in range(ATT_GROUP)], axis=0)
    s1 = _dot_nt(q, kl_ref[...])
    s2 = _dot_nt(q, kc_ref[...])
    m = jnp.maximum(jnp.max(s1, axis=-1, keepdims=True), jnp.max(s2, axis=-1, keepdims=True))
    p1 = jnp.exp(s1 - m)
    p2 = jnp.exp(s2 - m)
    l = jnp.sum(p1, axis=-1, keepdims=True) + jnp.sum(p2, axis=-1, keepdims=True)
    o = (_dot(p1.astype(BF16), vl_ref[...]) + _dot(p2.astype(BF16), vc_ref[...])) / l
    for g in range(ATT_GROUP):
        o_ref[:, g * LANES:(g + 1) * LANES] = o[g * tq:(g + 1) * tq, :].astype(BF16)


def _attention(p, n_batch, seq, ctx_len):
    tq = ATT_Q_TILE
    nq = seq // tq
    cb = n_batch * seq // ctx_len
    ko, vo = ATT_QW // LANES, (ATT_QW + ATT_KW) // LANES
    gw = ATT_GROUP * ATT_HEAD_DIM
    return pl.pallas_call(
        _att_kernel,
        grid=(n_batch, ATT_KV_HEADS, nq),
        in_specs=[
            pl.BlockSpec((tq, gw), lambda b, h, t: (b * nq + t, h)),
            pl.BlockSpec((seq, LANES), lambda b, h, t: (b, ko + h)),
            pl.BlockSpec((seq, LANES), lambda b, h, t: (b, vo + h)),
            pl.BlockSpec((ctx_len, LANES), lambda b, h, t: (cb + b, ko + h)),
            pl.BlockSpec((ctx_len, LANES), lambda b, h, t: (cb + b, vo + h)),
        ],
        out_specs=pl.BlockSpec((tq, gw), lambda b, h, t: (b * nq + t, h)),
        out_shape=jax.ShapeDtypeStruct((n_batch * seq, ATT_QW), BF16),
        compiler_params=_cparams(3),
        name="attention",
    )(p, p, p, p, p)


def kernel(x, c, ctx, c_ctx, w_mod, b_mod, ret_w_in, ret_w_out, ret_log_decay_fwd, ret_log_decay_bwd, att_w_qkv, att_w_o, att_q_gain, att_k_gain, moe_w_group, moe_b_group, moe_w_expert, moe_b_expert, moe_w_gate, moe_w_up, moe_w_down, final_norm_gain):
    n_batch, seq, d = x.shape
    ctx_len = ctx.shape[1]
    tm = ROW_TILE
    assert d == D_MODEL and w_mod.shape[0] == DEPTH == 2
    assert seq % tm == 0 and (n_batch * ctx_len) % tm == 0 and seq % GRID_W == 0
    assert seq % RET_CHUNK == 0 and ctx_len % RET_CHUNK == 0 and (n_batch * seq) % ctx_len == 0
    n_lat = n_batch * seq
    t_rows = n_lat + n_batch * ctx_len
    nl = n_lat // tm
    geo = (t_rows, nl, seq // tm, n_batch)
    nt = t_rows // tm

    xs = jnp.concatenate([x.reshape(n_lat, d), ctx.reshape(n_batch * ctx_len, d)], axis=0)
    pad_rows = -(n_batch + 1) % MOD_ROWS_PAD
    c_rows = jnp.concatenate([c, c_ctx[None, :], jnp.zeros((pad_rows, d), F32)], axis=0)
    mod = _mod_vectors(c_rows, w_mod, b_mod)

    cos, sin = _ret_rope_tables(seq, tm)
    p = _proj_ret(xs, mod[0], ret_w_in[0].astype(BF16), cos, sin, geo)
    lg = jnp.stack([ret_log_decay_fwd[0], ret_log_decay_bwd[0]]).astype(F32)
    y_lat, y_ctx = _retention(p, lg, n_batch, seq, ctx_len)
    wr, br = _router_weights(moe_w_group[0], moe_b_group[0], moe_w_expert[0], moe_b_expert[0])
    x1, h2, route = _mixer_out(y_lat, y_ctx, xs, mod[0], ret_w_out[0].astype(BF16), wr, br, geo, nt)
    ya = _moe(h2, route, t_rows, moe_w_gate[0].astype(BF16), moe_w_up[0].astype(BF16),
              moe_w_down[0].astype(BF16))
    xs = _combine(x1, ya, route, mod[0], geo, nt)

    cos, sin_a, sin_b = _att_rope_tables(seq, tm)
    p = _proj_att(xs, mod[1], att_w_qkv[0].astype(BF16), att_q_gain[0], att_k_gain[0],
                  cos, sin_a, sin_b, geo)
    o = _attention(p, n_batch, seq, ctx_len)
    wr, br = _router_weights(moe_w_group[1], moe_b_group[1], moe_w_expert[1], moe_b_expert[1])
    x1, h2, route = _mixer_out(o, None, xs, mod[1], att_w_o[0].astype(BF16), wr, br, geo, nl)
    ya = _moe(h2, route, n_lat, moe_w_gate[1].astype(BF16), moe_w_up[1].astype(BF16),
              moe_w_down[1].astype(BF16))
    out = _combine(x1, ya, route, mod[1], geo, nl, gain=final_norm_gain)
    return out.reshape(n_batch, seq, d)
```

```python
import functools

import jax
import jax.numpy as jnp
from jax import lax
from jax.experimental import pallas as pl
from jax.experimental.pallas import tpu as pltpu

F32 = jnp.float32
BF16 = jnp.bfloat16

D_MODEL = 1024
DEPTH = 2
GRID_W = 64
N_MOD = 6
NORM_EPS = 1e-6
HEAD_NORM_EPS = 1e-5
ROPE_THETA = 10000.0

RET_HEADS = 4
RET_DK = D_MODEL // RET_HEADS
RET_DV = 2 * RET_DK
RET_QK = RET_HEADS * RET_DK
RET_V = RET_HEADS * RET_DV
RET_IN = 2 * RET_QK + 2 * RET_V

ATT_HEAD_DIM = 128
ATT_Q_HEADS = D_MODEL // ATT_HEAD_DIM
ATT_KV_HEADS = 2
ATT_GROUP = ATT_Q_HEADS // ATT_KV_HEADS
ATT_QW = ATT_Q_HEADS * ATT_HEAD_DIM
ATT_KW = ATT_KV_HEADS * ATT_HEAD_DIM
ATT_IN = ATT_QW + 2 * ATT_KW

N_GROUPS = 4
EXPERTS_PER_GROUP = 8
N_EXPERTS = N_GROUPS * EXPERTS_PER_GROUP
EXPERT_HIDDEN = D_MODEL // 2

LANES = 128
SUBLANES = 8
ROW_TILE = 512
RET_CHUNK = 256
ATT_Q_TILE = 256
ATT_SOFTMAX_ROWS = 64
LOG2_E = 1.4426950408889634
EXPERT_BLOCK = 512
MOD_ROWS_PAD = 8
VMEM_LIMIT = 56 * 1024 * 1024


def _cparams(n_axes):
    return pltpu.CompilerParams(
        dimension_semantics=("arbitrary",) * n_axes, vmem_limit_bytes=VMEM_LIMIT)


def _silu(v):
    return v * (1.0 / (1.0 + jnp.exp(-v)))


def _modulate(x, shift, scale):
    ms = jnp.mean(x * x, axis=-1, keepdims=True)
    return x * lax.rsqrt(ms + NORM_EPS) * (1.0 + scale) + shift


def _split_bf16(v):
    hi = v.astype(BF16)
    lo = (v - hi.astype(F32)).astype(BF16)
    return hi, lo


def _dot(a, b):
    return jnp.dot(a, b, preferred_element_type=F32)


def _dot_nt(a, b):
    return lax.dot_general(a, b, (((1,), (1,)), ((), ())), preferred_element_type=F32)


def _row_halves(n):
    return ((0, n // 2), (n // 2, n))


def _mod_kernel(c_ref, w_ref, b_ref, o_ref):
    a_hi, a_lo = _split_bf16(_silu(c_ref[...]))
    w_hi, w_lo = _split_bf16(w_ref[0])
    acc = _dot(a_hi, w_hi) + _dot(a_lo, w_hi) + _dot(a_hi, w_lo)
    o_ref[0] = acc + b_ref[0]


def _mod_vectors(c_rows, w_mod, b_mod):
    rows = c_rows.shape[0]
    n = w_mod.shape[-1]
    tn = 1024
    out = pl.pallas_call(
        _mod_kernel,
        grid=(DEPTH, n // tn),
        in_specs=[
            pl.BlockSpec((rows, D_MODEL), lambda l, j: (0, 0)),
            pl.BlockSpec((1, D_MODEL, tn), lambda l, j: (l, 0, j)),
            pl.BlockSpec((1, 1, tn), lambda l, j: (l, 0, j)),
        ],
        out_specs=pl.BlockSpec((1, rows, tn), lambda l, j: (l, 0, j)),
        out_shape=jax.ShapeDtypeStruct((DEPTH, rows, n), F32),
        compiler_params=_cparams(2),
        name="mod_vectors",
    )(c_rows, w_mod, b_mod.reshape(DEPTH, 1, n))
    return out.reshape(DEPTH, rows, N_MOD, D_MODEL)


def _rope_angles(n, head_dim):
    rows = n // GRID_W
    r = jnp.repeat(jnp.arange(rows), GRID_W).astype(F32)
    col = jnp.tile(jnp.arange(GRID_W), rows).astype(F32)
    nf = head_dim // 4
    inv = ROPE_THETA ** (-jnp.arange(nf, dtype=F32) / nf)
    ar = r[:, None] * inv
    ac = col[:, None] * inv
    return jnp.concatenate([ar, ar, ac, ac], axis=-1)


def _ret_rope_tables(n, pad):
    ang = _rope_angles(n, RET_DK)
    lane = jnp.arange(RET_DK) % LANES
    sin = jnp.where(lane < LANES // 2, -jnp.sin(ang), jnp.sin(ang))
    cos = jnp.concatenate([jnp.cos(ang), jnp.ones((pad, RET_DK), F32)], axis=0)
    sin = jnp.concatenate([sin, jnp.zeros((pad, RET_DK), F32)], axis=0)
    return cos, sin


def _att_rope_tables(n, pad, gain, scale):
    ang = _rope_angles(n, ATT_HEAD_DIM)
    quarter = ATT_HEAD_DIM // 4
    first = jnp.arange(ATT_HEAD_DIM) % (2 * quarter) < quarter
    sin = jnp.sin(ang)
    c = jnp.cos(ang) * gain * scale
    a = jnp.where(first, -sin, 0.0) * jnp.roll(gain, ATT_HEAD_DIM - quarter) * scale
    b = jnp.where(first, 0.0, sin) * jnp.roll(gain, quarter) * scale
    ident = jnp.broadcast_to(gain * scale, (pad, ATT_HEAD_DIM))
    zeros = jnp.zeros((pad, ATT_HEAD_DIM), F32)
    return (jnp.concatenate([c, ident], axis=0), jnp.concatenate([a, zeros], axis=0),
            jnp.concatenate([b, zeros], axis=0))


def _proj_ret_kernel(x_ref, m_ref, w_ref, cos_ref, sin_ref, o_ref, h_ref):
    j = pl.program_id(1)

    @pl.when(j == 0)
    def _():
        h_ref[...] = _modulate(x_ref[...], m_ref[0, 0:1, :], m_ref[0, 1:2, :]).astype(BF16)

    halves = _row_halves(h_ref.shape[0])
    accs = [_dot(h_ref[a:b, :], w_ref[...]) for a, b in halves]

    @pl.when(j <= 1)
    def _():
        scale = jnp.where(j == 1, RET_DK ** -0.5, 1.0).astype(F32)
        for (a, b), acc in zip(halves, accs):
            for g in range(acc.shape[1] // LANES):
                xs = acc[:, g * LANES:(g + 1) * LANES]
                t = (g % 2) * LANES
                r = (xs * cos_ref[a:b, t:t + LANES]
                     + pltpu.roll(xs, LANES // 2, 1) * sin_ref[a:b, t:t + LANES])
                o_ref[a:b, g * LANES:(g + 1) * LANES] = (r * scale).astype(BF16)

    @pl.when((j == 2) | (j == 3))
    def _():
        for (a, b), acc in zip(halves, accs):
            o_ref[a:b, :] = acc.astype(BF16)

    @pl.when(j >= 4)
    def _():
        for (a, b), acc in zip(halves, accs):
            o_ref[a:b, :] = _silu(acc).astype(BF16)


def _proj_ret(xs, mod, w_in, cos, sin, geo):
    t_rows, nl, tpb, n_batch = geo
    tm, tn = ROW_TILE, 1024
    nt = t_rows // tm
    return pl.pallas_call(
        _proj_ret_kernel,
        grid=(nt, RET_IN // tn),
        in_specs=[
            pl.BlockSpec((tm, D_MODEL), lambda i, j: (i, 0)),
            pl.BlockSpec((1, N_MOD, D_MODEL), lambda i, j: (jnp.where(i < nl, i // tpb, n_batch), 0, 0)),
            pl.BlockSpec((D_MODEL, tn), lambda i, j: (0, j)),
            pl.BlockSpec((tm, RET_DK), lambda i, j: (jnp.where(i < nl, i % tpb, tpb), 0)),
            pl.BlockSpec((tm, RET_DK), lambda i, j: (jnp.where(i < nl, i % tpb, tpb), 0)),
        ],
        out_specs=pl.BlockSpec((tm, tn), lambda i, j: (i, j)),
        out_shape=jax.ShapeDtypeStruct((t_rows, RET_IN), BF16),
        scratch_shapes=[pltpu.VMEM((tm, D_MODEL), BF16)],
        compiler_params=_cparams(2),
        name="proj_ret",
    )(xs, mod, w_in, cos, sin)


def _ret_kernel(lg_ref, ql, kl, vl, gl, qc, kc, vc, gc, ol, oc,
                sf_ref, sb_ref, y_ref, mask_ref):
    hd = pl.program_id(0)
    lgf = lg_ref[0, hd]
    lgb = lg_ref[1, hd]
    c = RET_CHUNK

    @pl.when(pl.program_id(1) == 0)
    def _():
        ii = lax.broadcasted_iota(jnp.int32, (c, c), 0).astype(F32)
        jj = lax.broadcasted_iota(jnp.int32, (c, c), 1).astype(F32)
        d = ii - jj
        mask_ref[...] = (jnp.where(d >= 0, jnp.exp(lgf * jnp.maximum(d, 0.0)), 0.0)
                         + jnp.where(d <= 0, jnp.exp(lgb * jnp.maximum(-d, 0.0)), 0.0))

    pos = lax.broadcasted_iota(jnp.int32, (c, 1), 0).astype(F32)
    xi_f = jnp.exp(lgf * (pos + 1.0))
    zeta_f = jnp.exp(lgf * (c - 1.0 - pos))
    xi_b = jnp.exp(lgb * (c - pos))
    zeta_b = jnp.exp(lgb * pos)
    dec_f = jnp.exp(jnp.full((1, RET_DV), lgf * c, F32))
    dec_b = jnp.exp(jnp.full((1, RET_DV), lgb * c, F32))

    def state_update(s_ref, k, v, zeta, dec):
        kz = (k.astype(F32) * zeta).T.astype(BF16)
        s_ref[...] = s_ref[...] * dec + _dot(kz, v)

    def run(q_ref, k_ref, v_ref, g_ref, o_ref, n):
        def fwd(t, carry):
            r0 = pl.multiple_of(t * c, c)
            q = q_ref[pl.ds(r0, c), :]
            k = k_ref[pl.ds(r0, c), :]
            v = v_ref[pl.ds(r0, c), :]
            a = (_dot_nt(q, k) * mask_ref[...]).astype(BF16)
            cross = _dot(q, sf_ref[...].astype(BF16)) * xi_f
            y_ref[pl.ds(r0, c), :] = _dot(a, v) + cross
            state_update(sf_ref, k, v, zeta_f, dec_f)
            return carry

        lax.fori_loop(0, n, fwd, 0)

        def bwd(t, carry):
            r0 = pl.multiple_of((n - 1 - t) * c, c)
            q = q_ref[pl.ds(r0, c), :]
            k = k_ref[pl.ds(r0, c), :]
            v = v_ref[pl.ds(r0, c), :]
            y = y_ref[pl.ds(r0, c), :] + _dot(q, sb_ref[...].astype(BF16)) * xi_b
            mu = jnp.mean(y, axis=-1, keepdims=True)
            yc = y - mu
            var = jnp.mean(yc * yc, axis=-1, keepdims=True)
            yn = yc * lax.rsqrt(var + HEAD_NORM_EPS)
            o_ref[pl.ds(r0, c), :] = (yn * g_ref[pl.ds(r0, c), :].astype(F32)).astype(BF16)
            state_update(sb_ref, k, v, zeta_b, dec_b)
            return carry

        lax.fori_loop(0, n, bwd, 0)

    sf_ref[...] = jnp.zeros_like(sf_ref)
    sb_ref[...] = jnp.zeros_like(sb_ref)
    run(qc, kc, vc, gc, oc, qc.shape[0] // c)
    run(ql, kl, vl, gl, ol, ql.shape[0] // c)


def _retention(p, lg, n_batch, seq, ctx_len):
    cb = n_batch * seq // ctx_len
    qo, ko = 0, RET_QK // RET_DK
    vo, go = 2 * RET_QK // RET_DV, (2 * RET_QK + RET_V) // RET_DV
    lat = lambda w, off: pl.BlockSpec((seq, w), lambda h, b, lg_: (b, off + h))
    ctx = lambda w, off: pl.BlockSpec((ctx_len, w), lambda h, b, lg_: (cb + b, off + h))
    return pl.pallas_call(
        _ret_kernel,
        grid_spec=pltpu.PrefetchScalarGridSpec(
            num_scalar_prefetch=1,
            grid=(RET_HEADS, n_batch),
            in_specs=[lat(RET_DK, qo), lat(RET_DK, ko), lat(RET_DV, vo), lat(RET_DV, go),
                      ctx(RET_DK, qo), ctx(RET_DK, ko), ctx(RET_DV, vo), ctx(RET_DV, go)],
            out_specs=[pl.BlockSpec((seq, RET_DV), lambda h, b, lg_: (b, h)),
                       pl.BlockSpec((ctx_len, RET_DV), lambda h, b, lg_: (b, h))],
            scratch_shapes=[pltpu.VMEM((RET_DK, RET_DV), F32), pltpu.VMEM((RET_DK, RET_DV), F32),
                            pltpu.VMEM((seq, RET_DV), F32), pltpu.VMEM((RET_CHUNK, RET_CHUNK), F32)],
        ),
        out_shape=[jax.ShapeDtypeStruct((n_batch * seq, RET_V), BF16),
                   jax.ShapeDtypeStruct((n_batch * ctx_len, RET_V), BF16)],
        compiler_params=_cparams(2),
        name="retention",
    )(lg, p, p, p, p, p, p, p, p)


def _route(h2, wr_ref, br_ref, tri_ref):
    hi, lo = _split_bf16(h2)
    l1 = _dot(hi, wr_ref[...])
    logits = l1[:, :LANES] + l1[:, LANES:] + _dot(lo, wr_ref[:, :LANES]) + br_ref[...]
    lane = lax.broadcasted_iota(jnp.int32, logits.shape, 1).astype(F32)
    neg = -jnp.inf
    big = float(LANES)
    is_g = lane < N_GROUPS
    gl = jnp.where(is_g, logits, neg)
    gm = jnp.max(gl, axis=-1, keepdims=True)
    gidx = jnp.min(jnp.where(gl == gm, lane, big), axis=-1, keepdims=True)
    gsum = jnp.sum(jnp.where(is_g, jnp.exp(jnp.where(is_g, logits, gm) - gm), 0.0), axis=-1, keepdims=True)
    g_p = 1.0 / gsum
    lo_l = N_GROUPS + EXPERTS_PER_GROUP * gidx
    in_grp = (lane >= lo_l) & (lane < lo_l + EXPERTS_PER_GROUP)
    el = jnp.where(in_grp, logits, neg)
    e1 = jnp.max(el, axis=-1, keepdims=True)
    i1 = jnp.min(jnp.where(el == e1, lane, big), axis=-1, keepdims=True)
    el2 = jnp.where(lane == i1, neg, el)
    e2 = jnp.max(el2, axis=-1, keepdims=True)
    i2 = jnp.min(jnp.where(el2 == e2, lane, big), axis=-1, keepdims=True)
    t = jnp.exp(e2 - e1)
    ga = g_p / (1.0 + t)
    gb = g_p * t / (1.0 + t)
    onehot = jnp.where((lane == i1) | (lane == i2), 1.0, 0.0)
    before = _dot(tri_ref[...], onehot.astype(BF16))
    r1 = jnp.sum(jnp.where(lane == i1, before, 0.0), axis=-1, keepdims=True)
    r2 = jnp.sum(jnp.where(lane == i2, before, 0.0), axis=-1, keepdims=True)
    cols = (i1 - N_GROUPS, i2 - N_GROUPS, ga, gb, r1, r2)
    route = jnp.zeros_like(logits)
    for n, v in enumerate(cols):
        route = jnp.where(lane == n, v, route)
    return route, jnp.sum(onehot, axis=0, keepdims=True)


def _pack_bf16_pairs(v):
    bits = lax.bitcast_convert_type(v.astype(BF16).astype(F32), jnp.int32)
    half = v.shape[1] // 2
    return lax.shift_right_logical(bits[:, :half], 16) | (bits[:, half:] & jnp.int32(-65536))


def _unpack_bf16_pairs(u):
    lo = lax.bitcast_convert_type(lax.shift_left(u, 16), F32)
    hi = lax.bitcast_convert_type(u & jnp.int32(-65536), F32)
    return jnp.concatenate([lo, hi], axis=1).astype(BF16)


def _out_kernel(*refs, nl, has_ctx):
    if has_ctx:
        a_lat, a_ctx, x_ref, m_ref, w_ref, wr_ref, br_ref, tri_ref, x1_ref, h2_ref, r_ref, c_ref = refs
        is_lat = pl.program_id(0) < nl
    else:
        a_lat, x_ref, m_ref, w_ref, wr_ref, br_ref, tri_ref, x1_ref, h2_ref, r_ref, c_ref = refs
    h2s = []
    for a, b in _row_halves(x_ref.shape[0]):
        lhs = jnp.where(is_lat, a_lat[a:b, :], a_ctx[a:b, :]) if has_ctx else a_lat[a:b, :]
        x1 = x_ref[a:b, :] + m_ref[0, 2:3, :] * _dot(lhs, w_ref[...])
        x1_ref[a:b, :] = x1
        h2 = _modulate(x1, m_ref[0, 3:4, :], m_ref[0, 4:5, :])
        h2_ref[a:b, :] = _pack_bf16_pairs(h2)
        h2s.append(h2)
    route, counts = _route(jnp.concatenate(h2s, axis=0), wr_ref, br_ref, tri_ref)
    r_ref[...] = route
    c_ref[0] = jnp.broadcast_to(counts, c_ref.shape[1:])


def _mixer_out(a_lat, a_ctx, xs, mod, w_out, wr, br, geo, nt):
    t_rows, nl, tpb, n_batch = geo
    tm = ROW_TILE
    kdim = w_out.shape[0]
    has_ctx = a_ctx is not None
    mrow = lambda i: (jnp.where(i < nl, i // tpb, n_batch), 0, 0)
    in_specs = [pl.BlockSpec((tm, kdim), lambda i: (jnp.minimum(i, nl - 1), 0))]
    args = [a_lat]
    if has_ctx:
        in_specs.append(pl.BlockSpec((tm, kdim), lambda i: (jnp.maximum(i - nl, 0), 0)))
        args.append(a_ctx)
    in_specs += [
        pl.BlockSpec((tm, D_MODEL), lambda i: (i, 0)),
        pl.BlockSpec((1, N_MOD, D_MODEL), mrow),
        pl.BlockSpec((kdim, D_MODEL), lambda i: (0, 0)),
        pl.BlockSpec((D_MODEL, 2 * LANES), lambda i: (0, 0)),
        pl.BlockSpec((1, LANES), lambda i: (0, 0)),
        pl.BlockSpec((tm, tm), lambda i: (0, 0)),
    ]
    tri = (jnp.arange(tm)[:, None] > jnp.arange(tm)[None, :]).astype(BF16)
    args += [xs, mod, w_out, wr, br, tri]
    rows = nt * tm
    return pl.pallas_call(
        functools.partial(_out_kernel, nl=nl, has_ctx=has_ctx),
        grid=(nt,),
        in_specs=in_specs,
        out_specs=[pl.BlockSpec((tm, D_MODEL), lambda i: (i, 0)),
                   pl.BlockSpec((tm, D_MODEL // 2), lambda i: (i, 0)),
                   pl.BlockSpec((tm, LANES), lambda i: (i, 0)),
                   pl.BlockSpec((1, SUBLANES, LANES), lambda i: (i, 0, 0))],
        out_shape=[jax.ShapeDtypeStruct((rows, D_MODEL), F32),
                   jax.ShapeDtypeStruct((rows, D_MODEL // 2), jnp.int32),
                   jax.ShapeDtypeStruct((rows, LANES), F32),
                   jax.ShapeDtypeStruct((nt, SUBLANES, LANES), F32)],
        compiler_params=_cparams(1),
        name="mixer_out_ctx" if has_ctx else "mixer_out",
    )(*args)


def _router_weights(w_group, b_group, w_expert, b_expert):
    w = jnp.concatenate([w_group, w_expert], axis=1)
    w = jnp.pad(w, ((0, 0), (0, LANES - w.shape[1])))
    hi = w.astype(BF16)
    lo = (w - hi.astype(F32)).astype(BF16)
    b = jnp.pad(jnp.concatenate([b_group, b_expert]), (0, LANES - N_GROUPS - N_EXPERTS))
    return jnp.concatenate([hi, lo], axis=1), b.reshape(1, LANES).astype(F32)


def _slot_tables(route, counts, n_tok):
    blk, tm = EXPERT_BLOCK, ROW_TILE
    cnt = counts[:, 0, N_GROUPS:N_GROUPS + N_EXPERTS].astype(jnp.int32)
    total = jnp.sum(cnt, axis=0)
    padded = (total + blk - 1) // blk * blk
    ends_p = jnp.cumsum(padded)
    base = (ends_p - padded)[None, :] + jnp.cumsum(cnt, axis=0) - cnt
    e = route[:, :2].astype(jnp.int32)
    rank = route[:, 4:6].astype(jnp.int32)
    base_rows = jnp.repeat(base, tm, axis=0)
    sel = e[:, :, None] == jnp.arange(N_EXPERTS, dtype=jnp.int32)[None, None, :]
    dest = jnp.sum(jnp.where(sel, base_rows[:, None, :], 0), axis=-1) + rank
    n_blocks = -(-2 * n_tok // blk) + N_EXPERTS
    block_e = jnp.minimum(
        jnp.searchsorted(ends_p, jnp.arange(n_blocks, dtype=jnp.int32) * blk, side='right'),
        N_EXPERTS - 1).astype(jnp.int32)
    n_valid = (ends_p[-1] // blk).astype(jnp.int32).reshape(1)
    return dest, block_e, n_valid, n_blocks


def _dispatch_kernel(dest_ref, x_ref, init_ref, o_ref, sem):
    del init_ref
    tm = x_ref.shape[0]

    def row_copy(t, k):
        d = dest_ref[0, 0, 2 * t + k]
        return pltpu.make_async_copy(x_ref.at[pl.ds(t, 1), :], o_ref.at[pl.ds(d, 1), :], sem)

    def issue(t, carry):
        row_copy(t, 0).start()
        row_copy(t, 1).start()
        return carry

    lax.fori_loop(0, tm, issue, 0, unroll=4)

    def drain(t, carry):
        row_copy(t, 0).wait()
        row_copy(t, 1).wait()
        return carry

    lax.fori_loop(0, tm, drain, 0, unroll=4)


def _dispatch(h2p, dest, n_slots):
    n_tok, width = h2p.shape
    tm = ROW_TILE
    nt = n_tok // tm
    return pl.pallas_call(
        _dispatch_kernel,
        grid=(nt,),
        in_specs=[
            pl.BlockSpec((1, 1, 2 * tm), lambda i: (i, 0, 0), memory_space=pltpu.SMEM),
            pl.BlockSpec((tm, width), lambda i: (i, 0)),
            pl.BlockSpec(memory_space=pl.ANY),
        ],
        out_specs=pl.BlockSpec(memory_space=pl.ANY),
        out_shape=jax.ShapeDtypeStruct((n_slots, width), h2p.dtype),
        scratch_shapes=[pltpu.SemaphoreType.DMA(())],
        input_output_aliases={2: 0},
        compiler_params=_cparams(1),
        name="dispatch",
    )(dest.reshape(nt, 1, 2 * tm), h2p, jnp.zeros((n_slots, width), h2p.dtype))


def _expert_kernel(be_ref, nv_ref, x_ref, wg_ref, wu_ref, wd_ref, o_ref, wg_b, wu_b, wd_b):
    i = pl.program_id(0)
    valid = i < nv_ref[0]
    new_expert = (i == 0) | (be_ref[i] != be_ref[jnp.maximum(i - 1, 0)])

    @pl.when(valid & new_expert)
    def _():
        wg_b[...] = wg_ref[0].astype(BF16)
        wu_b[...] = wu_ref[0].astype(BF16)
        wd_b[...] = wd_ref[0].astype(BF16)

    @pl.when(valid)
    def _():
        for a, b in _row_halves(x_ref.shape[0]):
            x = _unpack_bf16_pairs(x_ref[a:b, :])
            mid = (_silu(_dot(x, wg_b[...])) * _dot(x, wu_b[...])).astype(BF16)
            o_ref[a:b, :] = _dot(mid, wd_b[...]).astype(BF16)

    @pl.when(jnp.logical_not(valid))
    def _():
        o_ref[...] = jnp.zeros_like(o_ref)


def _expert_ffn(xs_sorted, block_e, n_valid, w_gate, w_up, w_down):
    n_slots = xs_sorted.shape[0]
    blk = EXPERT_BLOCK
    wspec = lambda r, c: pl.BlockSpec((1, r, c), lambda i, be, nv: (be[i], 0, 0))
    return pl.pallas_call(
        _expert_kernel,
        grid_spec=pltpu.PrefetchScalarGridSpec(
            num_scalar_prefetch=2,
            grid=(n_slots // blk,),
            in_specs=[
                pl.BlockSpec((blk, D_MODEL // 2), lambda i, be, nv: (i, 0)),
                wspec(D_MODEL, EXPERT_HIDDEN), wspec(D_MODEL, EXPERT_HIDDEN), wspec(EXPERT_HIDDEN, D_MODEL),
            ],
            out_specs=pl.BlockSpec((blk, D_MODEL), lambda i, be, nv: (i, 0)),
            scratch_shapes=[pltpu.VMEM((D_MODEL, EXPERT_HIDDEN), BF16),
                            pltpu.VMEM((D_MODEL, EXPERT_HIDDEN), BF16),
                            pltpu.VMEM((EXPERT_HIDDEN, D_MODEL), BF16)],
        ),
        out_shape=jax.ShapeDtypeStruct((n_slots, D_MODEL), BF16),
        compiler_params=_cparams(1),
        name="expert_ffn",
    )(block_e, n_valid, xs_sorted, w_gate, w_up, w_down)


def _moe(h2p, route, counts, n_tok, w_gate, w_up, w_down):
    dest, block_e, n_valid, n_blocks = _slot_tables(route, counts, n_tok)
    xs_sorted = _dispatch(h2p, dest.reshape(-1), n_blocks * EXPERT_BLOCK)
    ys = _expert_ffn(xs_sorted, block_e, n_valid, w_gate, w_up, w_down)
    return ys[dest.reshape(-1)].reshape(n_tok, 2 * D_MODEL)


def _combine_kernel(*refs, final):
    if final:
        x1_ref, ya_ref, r_ref, m_ref, gain_ref, o_ref = refs
    else:
        x1_ref, ya_ref, r_ref, m_ref, o_ref = refs
    r = r_ref[...]
    f = (r[:, 2:3] * ya_ref[:, :D_MODEL].astype(F32)
         + r[:, 3:4] * ya_ref[:, D_MODEL:].astype(F32))
    x2 = x1_ref[...] + m_ref[0, 5:6, :] * f
    if final:
        ms = jnp.mean(x2 * x2, axis=-1, keepdims=True)
        x2 = x2 * lax.rsqrt(ms + NORM_EPS) * gain_ref[...]
    o_ref[...] = x2


def _combine(x1, ya, route, mod, geo, nt, gain=None):
    t_rows, nl, tpb, n_batch = geo
    tm = ROW_TILE
    final = gain is not None
    in_specs = [
        pl.BlockSpec((tm, D_MODEL), lambda i: (i, 0)),
        pl.BlockSpec((tm, 2 * D_MODEL), lambda i: (i, 0)),
        pl.BlockSpec((tm, LANES), lambda i: (i, 0)),
        pl.BlockSpec((1, N_MOD, D_MODEL), lambda i: (jnp.where(i < nl, i // tpb, n_batch), 0, 0)),
    ]
    args = [x1, ya, route, mod]
    if final:
        in_specs.append(pl.BlockSpec((1, D_MODEL), lambda i: (0, 0)))
        args.append(gain.reshape(1, D_MODEL))
    return pl.pallas_call(
        functools.partial(_combine_kernel, final=final),
        grid=(nt,),
        in_specs=in_specs,
        out_specs=pl.BlockSpec((tm, D_MODEL), lambda i: (i, 0)),
        out_shape=jax.ShapeDtypeStruct((nt * tm, D_MODEL), F32),
        compiler_params=_cparams(1),
        name="combine_final" if final else "combine",
    )(*args)


def _proj_att_kernel(x_ref, m_ref, w_ref, qc_ref, qa_ref, qb_ref, kc_ref, ka_ref, kb_ref, o_ref, h_ref):
    j = pl.program_id(1)

    @pl.when(j == 0)
    def _():
        h_ref[...] = _modulate(x_ref[...], m_ref[0, 0:1, :], m_ref[0, 1:2, :]).astype(BF16)

    halves = _row_halves(h_ref.shape[0])
    accs = [_dot(h_ref[a:b, :], w_ref[...]) for a, b in halves]
    nq_tiles = ATT_QW // w_ref.shape[1]

    def normed_rope(xs, a, b, c_ref, a_ref, b_ref):
        rs = lax.rsqrt(jnp.mean(xs * xs, axis=-1, keepdims=True) + NORM_EPS)
        r = (xs * c_ref[a:b, :] + pltpu.roll(xs, LANES - ATT_HEAD_DIM // 4, 1) * a_ref[a:b, :]
             + pltpu.roll(xs, ATT_HEAD_DIM // 4, 1) * b_ref[a:b, :])
        return (r * rs).astype(BF16)

    @pl.when(j < nq_tiles)
    def _():
        for (a, b), acc in zip(halves, accs):
            for g in range(acc.shape[1] // LANES):
                o_ref[a:b, g * LANES:(g + 1) * LANES] = normed_rope(
                    acc[:, g * LANES:(g + 1) * LANES], a, b, qc_ref, qa_ref, qb_ref)

    @pl.when(j >= nq_tiles)
    def _():
        for (a, b), acc in zip(halves, accs):
            for g in range(ATT_KV_HEADS):
                o_ref[a:b, g * LANES:(g + 1) * LANES] = normed_rope(
                    acc[:, g * LANES:(g + 1) * LANES], a, b, kc_ref, ka_ref, kb_ref)
            o_ref[a:b, ATT_KW:] = acc[:, ATT_KW:].astype(BF16)


def _proj_att(xs, mod, w_qkv, q_tabs, k_tabs, geo):
    t_rows, nl, tpb, n_batch = geo
    tm, tn = ROW_TILE, 2 * ATT_KW
    nt = t_rows // tm
    rope = pl.BlockSpec((tm, ATT_HEAD_DIM), lambda i, j: (jnp.where(i < nl, i % tpb, tpb), 0))
    return pl.pallas_call(
        _proj_att_kernel,
        grid=(nt, ATT_IN // tn),
        in_specs=[
            pl.BlockSpec((tm, D_MODEL), lambda i, j: (i, 0)),
            pl.BlockSpec((1, N_MOD, D_MODEL), lambda i, j: (jnp.where(i < nl, i // tpb, n_batch), 0, 0)),
            pl.BlockSpec((D_MODEL, tn), lambda i, j: (0, j)),
            rope, rope, rope, rope, rope, rope,
        ],
        out_specs=pl.BlockSpec((tm, tn), lambda i, j: (i, j)),
        out_shape=jax.ShapeDtypeStruct((t_rows, ATT_IN), BF16),
        scratch_shapes=[pltpu.VMEM((tm, D_MODEL), BF16)],
        compiler_params=_cparams(2),
        name="proj_att",
    )(xs, mod, w_qkv, *q_tabs, *k_tabs)


def _att_kernel(q_ref, kl_ref, vl_ref, kc_ref, vc_ref, o_ref, k_all, v_ext, s_ref, p_ref):
    tq = q_ref.shape[0]
    seq, ctx_len = kl_ref.shape[0], kc_ref.shape[0]

    @pl.when(pl.program_id(2) == 0)
    def _():
        k_all[0:seq, :] = kl_ref[...]
        k_all[seq:seq + ctx_len, :] = kc_ref[...]
        v_ext[0:seq, 0:LANES] = vl_ref[...]
        v_ext[seq:seq + ctx_len, 0:LANES] = vc_ref[...]
        v_ext[:, LANES:] = jnp.ones((seq + ctx_len, LANES), BF16)

    rows = ATT_SOFTMAX_ROWS

    def scores(g):
        s_ref[g * tq:(g + 1) * tq, :] = _dot_nt(q_ref[:, g * LANES:(g + 1) * LANES], k_all[...])

    def softmax(g):
        for r0 in range(g * tq, (g + 1) * tq, rows):
            m = jnp.max(s_ref[r0:r0 + rows, :], axis=-1, keepdims=True)
            p_ref[r0:r0 + rows, :] = jnp.exp2(s_ref[r0:r0 + rows, :] - m).astype(BF16)

    def weighted_values(g):
        oe = _dot(p_ref[g * tq:(g + 1) * tq, :], v_ext[...])
        o_ref[:, g * LANES:(g + 1) * LANES] = (oe[:, :LANES] / oe[:, LANES:]).astype(BF16)

    scores(0)
    scores(1)
    for g in range(ATT_GROUP):
        if g + 2 < ATT_GROUP:
            scores(g + 2)
        softmax(g)
        weighted_values(g)


def _attention(p, n_batch, seq, ctx_len):
    tq = ATT_Q_TILE
    nq = seq // tq
    cb = n_batch * seq // ctx_len
    ko, vo = ATT_QW // LANES, (ATT_QW + ATT_KW) // LANES
    gw = ATT_GROUP * ATT_HEAD_DIM
    n_keys = seq + ctx_len
    return pl.pallas_call(
        _att_kernel,
        grid=(n_batch, ATT_KV_HEADS, nq),
        in_specs=[
            pl.BlockSpec((tq, gw), lambda b, h, t: (b * nq + t, h)),
            pl.BlockSpec((seq, LANES), lambda b, h, t: (b, ko + h)),
            pl.BlockSpec((seq, LANES), lambda b, h, t: (b, vo + h)),
            pl.BlockSpec((ctx_len, LANES), lambda b, h, t: (cb + b, ko + h)),
            pl.BlockSpec((ctx_len, LANES), lambda b, h, t: (cb + b, vo + h)),
        ],
        out_specs=pl.BlockSpec((tq, gw), lambda b, h, t: (b * nq + t, h)),
        out_shape=jax.ShapeDtypeStruct((n_batch * seq, ATT_QW), BF16),
        scratch_shapes=[pltpu.VMEM((n_keys, LANES), BF16), pltpu.VMEM((n_keys, 2 * LANES), BF16),
                        pltpu.VMEM((ATT_GROUP * tq, n_keys), F32), pltpu.VMEM((ATT_GROUP * tq, n_keys), BF16)],
        compiler_params=_cparams(3),
        name="attention",
    )(p, p, p, p, p)


def kernel(x, c, ctx, c_ctx, w_mod, b_mod, ret_w_in, ret_w_out, ret_log_decay_fwd, ret_log_decay_bwd, att_w_qkv, att_w_o, att_q_gain, att_k_gain, moe_w_group, moe_b_group, moe_w_expert, moe_b_expert, moe_w_gate, moe_w_up, moe_w_down, final_norm_gain):
    n_batch, seq, d = x.shape
    ctx_len = ctx.shape[1]
    tm = ROW_TILE
    assert d == D_MODEL and w_mod.shape[0] == DEPTH == 2
    assert seq % tm == 0 and (n_batch * ctx_len) % tm == 0 and seq % GRID_W == 0
    assert seq % RET_CHUNK == 0 and ctx_len % RET_CHUNK == 0 and (n_batch * seq) % ctx_len == 0
    n_lat = n_batch * seq
    t_rows = n_lat + n_batch * ctx_len
    nl = n_lat // tm
    geo = (t_rows, nl, seq // tm, n_batch)
    nt = t_rows // tm

    xs = jnp.concatenate([x.reshape(n_lat, d), ctx.reshape(n_batch * ctx_len, d)], axis=0)
    pad_rows = -(n_batch + 1) % MOD_ROWS_PAD
    c_rows = jnp.concatenate([c, c_ctx[None, :], jnp.zeros((pad_rows, d), F32)], axis=0)
    mod = _mod_vectors(c_rows, w_mod, b_mod)

    cos, sin = _ret_rope_tables(seq, tm)
    p = _proj_ret(xs, mod[0], ret_w_in[0].astype(BF16), cos, sin, geo)
    lg = jnp.stack([ret_log_decay_fwd[0], ret_log_decay_bwd[0]]).astype(F32)
    y_lat, y_ctx = _retention(p, lg, n_batch, seq, ctx_len)
    wr, br = _router_weights(moe_w_group[0], moe_b_group[0], moe_w_expert[0], moe_b_expert[0])
    x1, h2p, route, counts = _mixer_out(y_lat, y_ctx, xs, mod[0], ret_w_out[0].astype(BF16), wr, br, geo, nt)
    ya = _moe(h2p, route, counts, t_rows, moe_w_gate[0], moe_w_up[0], moe_w_down[0])
    xs = _combine(x1, ya, route, mod[0], geo, nt)

    q_tabs = _att_rope_tables(seq, tm, att_q_gain[0], ATT_HEAD_DIM ** -0.5 * LOG2_E)
    k_tabs = _att_rope_tables(seq, tm, att_k_gain[0], 1.0)
    p = _proj_att(xs, mod[1], att_w_qkv[0].astype(BF16), q_tabs, k_tabs, geo)
    o = _attention(p, n_batch, seq, ctx_len)
    wr, br = _router_weights(moe_w_group[1], moe_b_group[1], moe_w_expert[1], moe_b_expert[1])
    x1, h2p, route, counts = _mixer_out(o, None, xs, mod[1], att_w_o[0].astype(BF16), wr, br, geo, nl)
    ya = _moe(h2p, route, counts, n_lat, moe_w_gate[1], moe_w_up[1], moe_w_down[1])
    out = _combine(x1, ya, route, mod[1], geo, nl, gain=final_norm_gain)
    return out.reshape(n_batch, seq, d)
```

```python
import functools

import jax
import jax.numpy as jnp
from jax import lax
from jax.experimental import pallas as pl
from jax.experimental.pallas import tpu as pltpu

F32 = jnp.float32
BF16 = jnp.bfloat16

D_MODEL = 1024
DEPTH = 2
GRID_W = 64
N_MOD = 6
NORM_EPS = 1e-6
HEAD_NORM_EPS = 1e-5
ROPE_THETA = 10000.0

RET_HEADS = 4
RET_DK = D_MODEL // RET_HEADS
RET_DV = 2 * RET_DK
RET_QK = RET_HEADS * RET_DK
RET_V = RET_HEADS * RET_DV
RET_IN = 2 * RET_QK + 2 * RET_V

ATT_HEAD_DIM = 128
ATT_Q_HEADS = D_MODEL // ATT_HEAD_DIM
ATT_KV_HEADS = 2
ATT_GROUP = ATT_Q_HEADS // ATT_KV_HEADS
ATT_QW = ATT_Q_HEADS * ATT_HEAD_DIM
ATT_KW = ATT_KV_HEADS * ATT_HEAD_DIM
ATT_IN = ATT_QW + 2 * ATT_KW

N_GROUPS = 4
EXPERTS_PER_GROUP = 8
N_EXPERTS = N_GROUPS * EXPERTS_PER_GROUP
EXPERT_HIDDEN = D_MODEL // 2

LANES = 128
SUBLANES = 8
ROW_TILE = 512
PROJ_GROUP = 512
RET_CHUNK = 256
ATT_Q_TILE = 256
ATT_SOFTMAX_ROWS = 64
LOG2_E = 1.4426950408889634
EXPERT_BLOCK = 512
MOD_ROWS_PAD = 8
VMEM_LIMIT = 56 * 1024 * 1024


def _cparams(n_axes):
    return pltpu.CompilerParams(
        dimension_semantics=("arbitrary",) * n_axes, vmem_limit_bytes=VMEM_LIMIT)


def _silu(v):
    return v * (1.0 / (1.0 + jnp.exp(-v)))


def _modulate(x, shift, scale):
    ms = jnp.mean(x * x, axis=-1, keepdims=True)
    return x * lax.rsqrt(ms + NORM_EPS) * (1.0 + scale) + shift


def _split_bf16(v):
    hi = v.astype(BF16)
    lo = (v - hi.astype(F32)).astype(BF16)
    return hi, lo


def _dot(a, b):
    return jnp.dot(a, b, preferred_element_type=F32)


def _dot_nt(a, b):
    return lax.dot_general(a, b, (((1,), (1,)), ((), ())), preferred_element_type=F32)


def _row_halves(n):
    return ((0, n // 2), (n // 2, n))


def _mod_kernel(c_ref, w_ref, b_ref, o_ref):
    a_hi, a_lo = _split_bf16(_silu(c_ref[...]))
    w_hi, w_lo = _split_bf16(w_ref[0])
    acc = _dot(a_hi, w_hi) + _dot(a_lo, w_hi) + _dot(a_hi, w_lo)
    o_ref[0] = acc + b_ref[0]


def _mod_vectors(c_rows, w_mod, b_mod):
    rows = c_rows.shape[0]
    n = w_mod.shape[-1]
    tn = 1024
    out = pl.pallas_call(
        _mod_kernel,
        grid=(DEPTH, n // tn),
        in_specs=[
            pl.BlockSpec((rows, D_MODEL), lambda l, j: (0, 0)),
            pl.BlockSpec((1, D_MODEL, tn), lambda l, j: (l, 0, j)),
            pl.BlockSpec((1, 1, tn), lambda l, j: (l, 0, j)),
        ],
        out_specs=pl.BlockSpec((1, rows, tn), lambda l, j: (l, 0, j)),
        out_shape=jax.ShapeDtypeStruct((DEPTH, rows, n), F32),
        compiler_params=_cparams(2),
        name="mod_vectors",
    )(c_rows, w_mod, b_mod.reshape(DEPTH, 1, n))
    return out.reshape(DEPTH, rows, N_MOD, D_MODEL)


def _rope_angles(n, head_dim):
    rows = n // GRID_W
    r = jnp.repeat(jnp.arange(rows), GRID_W).astype(F32)
    col = jnp.tile(jnp.arange(GRID_W), rows).astype(F32)
    nf = head_dim // 4
    inv = ROPE_THETA ** (-jnp.arange(nf, dtype=F32) / nf)
    ar = r[:, None] * inv
    ac = col[:, None] * inv
    return jnp.concatenate([ar, ar, ac, ac], axis=-1)


def _ret_rope_tables(n, pad):
    ang = _rope_angles(n, RET_DK)
    lane = jnp.arange(RET_DK) % LANES
    sin = jnp.where(lane < LANES // 2, -jnp.sin(ang), jnp.sin(ang))
    cos = jnp.concatenate([jnp.cos(ang), jnp.ones((pad, RET_DK), F32)], axis=0)
    sin = jnp.concatenate([sin, jnp.zeros((pad, RET_DK), F32)], axis=0)
    return cos, sin


def _att_rope_tables(n, pad, gain, scale):
    ang = _rope_angles(n, ATT_HEAD_DIM)
    quarter = ATT_HEAD_DIM // 4
    first = jnp.arange(ATT_HEAD_DIM) % (2 * quarter) < quarter
    sin = jnp.sin(ang)
    c = jnp.cos(ang) * gain * scale
    a = jnp.where(first, -sin, 0.0) * jnp.roll(gain, ATT_HEAD_DIM - quarter) * scale
    b = jnp.where(first, 0.0, sin) * jnp.roll(gain, quarter) * scale
    ident = jnp.broadcast_to(gain * scale, (pad, ATT_HEAD_DIM))
    zeros = jnp.zeros((pad, ATT_HEAD_DIM), F32)
    return (jnp.concatenate([c, ident], axis=0), jnp.concatenate([a, zeros], axis=0),
            jnp.concatenate([b, zeros], axis=0))


def _load_rows(is_lat, lat_ref, ctx_ref, a, b):
    return jnp.where(is_lat, lat_ref[a:b, :], ctx_ref[a:b, :])


def _proj_ret_kernel(xl_ref, xc_ref, m_ref, w_ref, cos_ref, sin_ref, o_ref, h_ref, *, nl):
    is_lat = pl.program_id(0) < nl
    for a, b in _row_halves(h_ref.shape[0]):
        x = _load_rows(is_lat, xl_ref, xc_ref, a, b)
        h_ref[a:b, :] = _modulate(x, m_ref[0, 0:1, :], m_ref[0, 1:2, :]).astype(BF16)

    gw = PROJ_GROUP
    for c0 in range(0, RET_IN, gw):
        acc = _dot(h_ref[...], w_ref[:, c0:c0 + gw])
        if c0 < 2 * RET_QK:
            scale = 1.0 if c0 < RET_QK else RET_DK ** -0.5
            for g in range(gw // LANES):
                xs = acc[:, g * LANES:(g + 1) * LANES]
                t = ((c0 // LANES + g) % 2) * LANES
                r = xs * cos_ref[:, t:t + LANES] + pltpu.roll(xs, LANES // 2, 1) * sin_ref[:, t:t + LANES]
                o_ref[:, c0 + g * LANES:c0 + (g + 1) * LANES] = (r * scale).astype(BF16)
        elif c0 < 2 * RET_QK + RET_V:
            o_ref[:, c0:c0 + gw] = acc.astype(BF16)
        else:
            o_ref[:, c0:c0 + gw] = _silu(acc).astype(BF16)


def _proj_ret(x_lat, x_ctx, mod, w_in, cos, sin, geo):
    t_rows, nl, tpb, n_batch = geo
    tm = ROW_TILE
    nt = t_rows // tm
    return pl.pallas_call(
        functools.partial(_proj_ret_kernel, nl=nl),
        grid=(nt,),
        in_specs=[
            pl.BlockSpec((tm, D_MODEL), lambda i: (jnp.minimum(i, nl - 1), 0)),
            pl.BlockSpec((tm, D_MODEL), lambda i: (jnp.maximum(i - nl, 0), 0)),
            pl.BlockSpec((1, N_MOD, D_MODEL), lambda i: (jnp.where(i < nl, i // tpb, n_batch), 0, 0)),
            pl.BlockSpec((D_MODEL, RET_IN), lambda i: (0, 0)),
            pl.BlockSpec((tm, RET_DK), lambda i: (jnp.where(i < nl, i % tpb, tpb), 0)),
            pl.BlockSpec((tm, RET_DK), lambda i: (jnp.where(i < nl, i % tpb, tpb), 0)),
        ],
        out_specs=pl.BlockSpec((tm, RET_IN), lambda i: (i, 0)),
        out_shape=jax.ShapeDtypeStruct((t_rows, RET_IN), BF16),
        scratch_shapes=[pltpu.VMEM((tm, D_MODEL), BF16)],
        compiler_params=_cparams(1),
        name="proj_ret",
    )(x_lat, x_ctx, mod, w_in, cos, sin)


def _ret_kernel(lg_ref, ql, kl, vl, gl, qc, kc, vc, gc, ol, oc,
                sf_ref, sb_ref, y_ref, mask_ref):
    hd = pl.program_id(0)
    lgf = lg_ref[0, hd]
    lgb = lg_ref[1, hd]
    c = RET_CHUNK

    @pl.when(pl.program_id(1) == 0)
    def _():
        ii = lax.broadcasted_iota(jnp.int32, (c, c), 0).astype(F32)
        jj = lax.broadcasted_iota(jnp.int32, (c, c), 1).astype(F32)
        d = ii - jj
        mask_ref[...] = (jnp.where(d >= 0, jnp.exp(lgf * jnp.maximum(d, 0.0)), 0.0)
                         + jnp.where(d <= 0, jnp.exp(lgb * jnp.maximum(-d, 0.0)), 0.0))

    pos = lax.broadcasted_iota(jnp.int32, (c, 1), 0).astype(F32)
    xi_f = jnp.exp(lgf * (pos + 1.0))
    zeta_f = jnp.exp(lgf * (c - 1.0 - pos))
    xi_b = jnp.exp(lgb * (c - pos))
    zeta_b = jnp.exp(lgb * pos)
    dec_f = jnp.exp(jnp.full((1, RET_DV), lgf * c, F32))
    dec_b = jnp.exp(jnp.full((1, RET_DV), lgb * c, F32))

    def state_update(s_ref, k, v, zeta, dec):
        kz = (k.astype(F32) * zeta).T.astype(BF16)
        s_ref[...] = s_ref[...] * dec + _dot(kz, v)

    def run(q_ref, k_ref, v_ref, g_ref, o_ref, n):
        def fwd(t, carry):
            r0 = pl.multiple_of(t * c, c)
            q = q_ref[pl.ds(r0, c), :]
            k = k_ref[pl.ds(r0, c), :]
            v = v_ref[pl.ds(r0, c), :]
            a = (_dot_nt(q, k) * mask_ref[...]).astype(BF16)
            cross = _dot(q, sf_ref[...].astype(BF16)) * xi_f
            y_ref[pl.ds(r0, c), :] = _dot(a, v) + cross
            state_update(sf_ref, k, v, zeta_f, dec_f)
            return carry

        lax.fori_loop(0, n, fwd, 0)

        def bwd(t, carry):
            r0 = pl.multiple_of((n - 1 - t) * c, c)
            q = q_ref[pl.ds(r0, c), :]
            k = k_ref[pl.ds(r0, c), :]
            v = v_ref[pl.ds(r0, c), :]
            y = y_ref[pl.ds(r0, c), :] + _dot(q, sb_ref[...].astype(BF16)) * xi_b
            mu = jnp.mean(y, axis=-1, keepdims=True)
            yc = y - mu
            var = jnp.mean(yc * yc, axis=-1, keepdims=True)
            yn = yc * lax.rsqrt(var + HEAD_NORM_EPS)
            o_ref[pl.ds(r0, c), :] = (yn * g_ref[pl.ds(r0, c), :].astype(F32)).astype(BF16)
            state_update(sb_ref, k, v, zeta_b, dec_b)
            return carry

        lax.fori_loop(0, n, bwd, 0)

    sf_ref[...] = jnp.zeros_like(sf_ref)
    sb_ref[...] = jnp.zeros_like(sb_ref)
    run(qc, kc, vc, gc, oc, qc.shape[0] // c)
    run(ql, kl, vl, gl, ol, ql.shape[0] // c)


def _retention(p, lg, n_batch, seq, ctx_len):
    cb = n_batch * seq // ctx_len
    qo, ko = 0, RET_QK // RET_DK
    vo, go = 2 * RET_QK // RET_DV, (2 * RET_QK + RET_V) // RET_DV
    lat = lambda w, off: pl.BlockSpec((seq, w), lambda h, b, lg_: (b, off + h))
    ctx = lambda w, off: pl.BlockSpec((ctx_len, w), lambda h, b, lg_: (cb + b, off + h))
    return pl.pallas_call(
        _ret_kernel,
        grid_spec=pltpu.PrefetchScalarGridSpec(
            num_scalar_prefetch=1,
            grid=(RET_HEADS, n_batch),
            in_specs=[lat(RET_DK, qo), lat(RET_DK, ko), lat(RET_DV, vo), lat(RET_DV, go),
                      ctx(RET_DK, qo), ctx(RET_DK, ko), ctx(RET_DV, vo), ctx(RET_DV, go)],
            out_specs=[pl.BlockSpec((seq, RET_DV), lambda h, b, lg_: (b, h)),
                       pl.BlockSpec((ctx_len, RET_DV), lambda h, b, lg_: (b, h))],
            scratch_shapes=[pltpu.VMEM((RET_DK, RET_DV), F32), pltpu.VMEM((RET_DK, RET_DV), F32),
                            pltpu.VMEM((seq, RET_DV), F32), pltpu.VMEM((RET_CHUNK, RET_CHUNK), F32)],
        ),
        out_shape=[jax.ShapeDtypeStruct((n_batch * seq, RET_V), BF16),
                   jax.ShapeDtypeStruct((n_batch * ctx_len, RET_V), BF16)],
        compiler_params=_cparams(2),
        name="retention",
    )(lg, p, p, p, p, p, p, p, p)


def _route(h2, wr_ref, br_ref, tri_ref):
    hi, lo = _split_bf16(h2)
    l1 = _dot(hi, wr_ref[...])
    logits = l1[:, :LANES] + l1[:, LANES:] + _dot(lo, wr_ref[:, :LANES]) + br_ref[...]
    lane = lax.broadcasted_iota(jnp.int32, logits.shape, 1).astype(F32)
    neg = -jnp.inf
    big = float(LANES)
    is_g = lane < N_GROUPS
    gl = jnp.where(is_g, logits, neg)
    gm = jnp.max(gl, axis=-1, keepdims=True)
    gidx = jnp.min(jnp.where(gl == gm, lane, big), axis=-1, keepdims=True)
    gsum = jnp.sum(jnp.where(is_g, jnp.exp(jnp.where(is_g, logits, gm) - gm), 0.0), axis=-1, keepdims=True)
    g_p = 1.0 / gsum
    lo_l = N_GROUPS + EXPERTS_PER_GROUP * gidx
    in_grp = (lane >= lo_l) & (lane < lo_l + EXPERTS_PER_GROUP)
    el = jnp.where(in_grp, logits, neg)
    e1 = jnp.max(el, axis=-1, keepdims=True)
    i1 = jnp.min(jnp.where(el == e1, lane, big), axis=-1, keepdims=True)
    el2 = jnp.where(lane == i1, neg, el)
    e2 = jnp.max(el2, axis=-1, keepdims=True)
    i2 = jnp.min(jnp.where(el2 == e2, lane, big), axis=-1, keepdims=True)
    t = jnp.exp(e2 - e1)
    ga = g_p / (1.0 + t)
    gb = g_p * t / (1.0 + t)
    onehot = jnp.where((lane == i1) | (lane == i2), 1.0, 0.0)
    before = _dot(tri_ref[...], onehot.astype(BF16))
    r1 = jnp.sum(jnp.where(lane == i1, before, 0.0), axis=-1, keepdims=True)
    r2 = jnp.sum(jnp.where(lane == i2, before, 0.0), axis=-1, keepdims=True)
    cols = (i1 - N_GROUPS, i2 - N_GROUPS, ga, gb, r1, r2)
    route = jnp.zeros_like(logits)
    for n, v in enumerate(cols):
        route = jnp.where(lane == n, v, route)
    return route, jnp.sum(onehot, axis=0, keepdims=True)


def _pack_bf16_pairs(v):
    bits = lax.bitcast_convert_type(v.astype(BF16).astype(F32), jnp.int32)
    half = v.shape[1] // 2
    return lax.shift_right_logical(bits[:, :half], 16) | (bits[:, half:] & jnp.int32(-65536))


def _unpack_bf16_pairs(u):
    lo = lax.bitcast_convert_type(lax.shift_left(u, 16), F32)
    hi = lax.bitcast_convert_type(u & jnp.int32(-65536), F32)
    return jnp.concatenate([lo, hi], axis=1).astype(BF16)


def _out_kernel(*refs, nl, has_ctx):
    if has_ctx:
        (a_lat, a_ctx, x_lat, x_ctx, m_ref, w_ref, wr_ref, br_ref, tri_ref,
         x1_ref, h2_ref, e_ref, k_ref, g_ref, c_ref) = refs
        is_lat = pl.program_id(0) < nl
    else:
        a_lat, x_lat, m_ref, w_ref, wr_ref, br_ref, tri_ref, x1_ref, h2_ref, e_ref, k_ref, g_ref, c_ref = refs
    h2s = []
    for a, b in _row_halves(x1_ref.shape[0]):
        lhs = _load_rows(is_lat, a_lat, a_ctx, a, b) if has_ctx else a_lat[a:b, :]
        x = _load_rows(is_lat, x_lat, x_ctx, a, b) if has_ctx else x_lat[a:b, :]
        x1 = x + m_ref[0, 2:3, :] * _dot(lhs, w_ref[...])
        x1_ref[a:b, :] = x1
        h2 = _modulate(x1, m_ref[0, 3:4, :], m_ref[0, 4:5, :])
        h2_ref[a:b, :] = _pack_bf16_pairs(h2)
        h2s.append(h2)
    route, counts = _route(jnp.concatenate(h2s, axis=0), wr_ref, br_ref, tri_ref)
    e_ref[...] = route[:, 0:2].astype(jnp.int32)
    k_ref[...] = route[:, 4:6].astype(jnp.int32)
    g_ref[...] = route
    c_ref[0] = jnp.broadcast_to(counts, c_ref.shape[1:])


def _mixer_out(a_lat, a_ctx, x_lat, x_ctx, mod, w_out, wr, br, geo, nt):
    t_rows, nl, tpb, n_batch = geo
    tm = ROW_TILE
    kdim = w_out.shape[0]
    has_ctx = a_ctx is not None
    mrow = lambda i: (jnp.where(i < nl, i // tpb, n_batch), 0, 0)
    lat = lambda w: pl.BlockSpec((tm, w), lambda i: (jnp.minimum(i, nl - 1), 0))
    ctx = lambda w: pl.BlockSpec((tm, w), lambda i: (jnp.maximum(i - nl, 0), 0))
    if has_ctx:
        in_specs = [lat(kdim), ctx(kdim), lat(D_MODEL), ctx(D_MODEL)]
        args = [a_lat, a_ctx, x_lat, x_ctx]
    else:
        in_specs = [lat(kdim), lat(D_MODEL)]
        args = [a_lat, x_lat]
    in_specs += [
        pl.BlockSpec((1, N_MOD, D_MODEL), mrow),
        pl.BlockSpec((kdim, D_MODEL), lambda i: (0, 0)),
        pl.BlockSpec((D_MODEL, 2 * LANES), lambda i: (0, 0)),
        pl.BlockSpec((1, LANES), lambda i: (0, 0)),
        pl.BlockSpec((tm, tm), lambda i: (0, 0)),
    ]
    tri = (jnp.arange(tm)[:, None] > jnp.arange(tm)[None, :]).astype(BF16)
    args += [mod, w_out, wr, br, tri]
    rows = nt * tm
    row_out = lambda w: pl.BlockSpec((tm, w), lambda i: (i, 0))
    return pl.pallas_call(
        functools.partial(_out_kernel, nl=nl, has_ctx=has_ctx),
        grid=(nt,),
        in_specs=in_specs,
        out_specs=[row_out(D_MODEL), row_out(D_MODEL // 2), row_out(2), row_out(2), row_out(LANES),
                   pl.BlockSpec((1, SUBLANES, LANES), lambda i: (i, 0, 0))],
        out_shape=[jax.ShapeDtypeStruct((rows, D_MODEL), F32),
                   jax.ShapeDtypeStruct((rows, D_MODEL // 2), jnp.int32),
                   jax.ShapeDtypeStruct((rows, 2), jnp.int32),
                   jax.ShapeDtypeStruct((rows, 2), jnp.int32),
                   jax.ShapeDtypeStruct((rows, LANES), F32),
                   jax.ShapeDtypeStruct((nt, SUBLANES, LANES), F32)],
        compiler_params=_cparams(1),
        name="mixer_out_ctx" if has_ctx else "mixer_out",
    )(*args)


def _router_weights(w_group, b_group, w_expert, b_expert):
    w = jnp.concatenate([w_group, w_expert], axis=1)
    w = jnp.pad(w, ((0, 0), (0, LANES - w.shape[1])))
    hi = w.astype(BF16)
    lo = (w - hi.astype(F32)).astype(BF16)
    b = jnp.pad(jnp.concatenate([b_group, b_expert]), (0, LANES - N_GROUPS - N_EXPERTS))
    return jnp.concatenate([hi, lo], axis=1), b.reshape(1, LANES).astype(F32)


def _slot_tables(counts, n_tok):
    blk = EXPERT_BLOCK
    cnt = counts[:, 0, N_GROUPS:N_GROUPS + N_EXPERTS].astype(jnp.int32)
    total = jnp.sum(cnt, axis=0)
    padded = (total + blk - 1) // blk * blk
    ends_p = jnp.cumsum(padded)
    base = (ends_p - padded)[None, :] + jnp.cumsum(cnt, axis=0) - cnt
    n_blocks = -(-2 * n_tok // blk) + N_EXPERTS
    block_e = jnp.minimum(
        jnp.searchsorted(ends_p, jnp.arange(n_blocks, dtype=jnp.int32) * blk, side='right'),
        N_EXPERTS - 1).astype(jnp.int32)
    n_valid = (ends_p[-1] // blk).astype(jnp.int32).reshape(1)
    return base.reshape(-1), block_e, n_valid, n_blocks


def _dispatch_kernel(base_ref, e_ref, k_ref, x_ref, init_ref, o_ref, d_ref, sem):
    del init_ref
    tm = x_ref.shape[0]
    tile_base = pl.program_id(0) * N_EXPERTS

    def issue(j, carry):
        t0 = pl.multiple_of(j * SUBLANES, SUBLANES)
        rows = x_ref.at[pl.ds(t0, SUBLANES), :]
        for u in range(SUBLANES):
            for k in range(2):
                n = 2 * (t0 + u) + k
                d = base_ref[tile_base + e_ref[0, 0, n]] + k_ref[0, 0, n]
                d_ref[0, 0, n] = d
                pltpu.make_async_copy(rows.at[pl.ds(u, 1), :], o_ref.at[pl.ds(d, 1), :], sem).start()
        return carry

    lax.fori_loop(0, tm // SUBLANES, issue, 0)

    for _ in range(2):
        pltpu.make_async_copy(x_ref, o_ref.at[pl.ds(0, tm), :], sem).wait()


def _dispatch(h2p, e_ids, ranks, base, n_slots):
    n_tok, width = h2p.shape
    tm = ROW_TILE
    nt = n_tok // tm
    smem_row = lambda: pl.BlockSpec((1, 1, 2 * tm), lambda i, b: (i, 0, 0), memory_space=pltpu.SMEM)
    return pl.pallas_call(
        _dispatch_kernel,
        grid_spec=pltpu.PrefetchScalarGridSpec(
            num_scalar_prefetch=1,
            grid=(nt,),
            in_specs=[smem_row(), smem_row(),
                      pl.BlockSpec((tm, width), lambda i, b: (i, 0)),
                      pl.BlockSpec(memory_space=pl.ANY)],
            out_specs=[pl.BlockSpec(memory_space=pl.ANY), smem_row()],
            scratch_shapes=[pltpu.SemaphoreType.DMA(())],
        ),
        out_shape=[jax.ShapeDtypeStruct((n_slots, width), h2p.dtype),
                   jax.ShapeDtypeStruct((nt, 1, 2 * tm), jnp.int32)],
        input_output_aliases={4: 0},
        compiler_params=_cparams(1),
        name="dispatch",
    )(base, e_ids.reshape(nt, 1, 2 * tm), ranks.reshape(nt, 1, 2 * tm), h2p,
      jnp.zeros((n_slots, width), h2p.dtype))


def _expert_kernel(be_ref, nv_ref, x_ref, wg_ref, wu_ref, wd_ref, o_ref, wg_b, wu_b, wd_b):
    i = pl.program_id(0)
    valid = i < nv_ref[0]
    new_expert = (i == 0) | (be_ref[i] != be_ref[jnp.maximum(i - 1, 0)])

    @pl.when(valid & new_expert)
    def _():
        wg_b[...] = wg_ref[0, 0].astype(BF16)
        wu_b[...] = wu_ref[0, 0].astype(BF16)
        wd_b[...] = wd_ref[0, 0].astype(BF16)

    @pl.when(valid)
    def _():
        for a, b in _row_halves(x_ref.shape[0]):
            x = _unpack_bf16_pairs(x_ref[a:b, :])
            mid = (_silu(_dot(x, wg_b[...])) * _dot(x, wu_b[...])).astype(BF16)
            o_ref[a:b, :] = _dot(mid, wd_b[...]).astype(BF16)

    @pl.when(jnp.logical_not(valid))
    def _():
        o_ref[...] = jnp.zeros_like(o_ref)


def _expert_ffn(xs_sorted, block_e, n_valid, w_gate, w_up, w_down, layer):
    n_slots = xs_sorted.shape[0]
    blk = EXPERT_BLOCK
    wspec = lambda r, c: pl.BlockSpec((1, 1, r, c), lambda i, be, nv: (layer, be[i], 0, 0))
    return pl.pallas_call(
        _expert_kernel,
        grid_spec=pltpu.PrefetchScalarGridSpec(
            num_scalar_prefetch=2,
            grid=(n_slots // blk,),
            in_specs=[
                pl.BlockSpec((blk, D_MODEL // 2), lambda i, be, nv: (i, 0)),
                wspec(D_MODEL, EXPERT_HIDDEN), wspec(D_MODEL, EXPERT_HIDDEN), wspec(EXPERT_HIDDEN, D_MODEL),
            ],
            out_specs=pl.BlockSpec((blk, D_MODEL), lambda i, be, nv: (i, 0)),
            scratch_shapes=[pltpu.VMEM((D_MODEL, EXPERT_HIDDEN), BF16),
                            pltpu.VMEM((D_MODEL, EXPERT_HIDDEN), BF16),
                            pltpu.VMEM((EXPERT_HIDDEN, D_MODEL), BF16)],
        ),
        out_shape=jax.ShapeDtypeStruct((n_slots, D_MODEL), BF16),
        compiler_params=_cparams(1),
        name="expert_ffn",
    )(block_e, n_valid, xs_sorted, w_gate, w_up, w_down)


def _moe(h2p, e_ids, ranks, counts, n_tok, w_gate, w_up, w_down, layer):
    base, block_e, n_valid, n_blocks = _slot_tables(counts, n_tok)
    xs_sorted, dest = _dispatch(h2p, e_ids, ranks, base, n_blocks * EXPERT_BLOCK)
    ys = _expert_ffn(xs_sorted, block_e, n_valid, w_gate, w_up, w_down, layer)
    dest = dest.reshape(n_tok, 2)
    return ys[dest[:, 0]], ys[dest[:, 1]]


def _combine_kernel(*refs, final):
    if final:
        x1_ref, ya_ref, yb_ref, r_ref, m_ref, gain_ref, o_ref = refs
    else:
        x1_ref, ya_ref, yb_ref, r_ref, m_ref, o_ref = refs
    r = r_ref[...]
    f = r[:, 2:3] * ya_ref[...].astype(F32) + r[:, 3:4] * yb_ref[...].astype(F32)
    x2 = x1_ref[...] + m_ref[0, 5:6, :] * f
    if final:
        ms = jnp.mean(x2 * x2, axis=-1, keepdims=True)
        x2 = x2 * lax.rsqrt(ms + NORM_EPS) * gain_ref[...]
    o_ref[...] = x2


def _combine(x1, ya, yb, route, mod, geo, nt, gain=None):
    t_rows, nl, tpb, n_batch = geo
    tm = ROW_TILE
    final = gain is not None
    in_specs = [
        pl.BlockSpec((tm, D_MODEL), lambda i: (i, 0)),
        pl.BlockSpec((tm, D_MODEL), lambda i: (i, 0)),
        pl.BlockSpec((tm, D_MODEL), lambda i: (i, 0)),
        pl.BlockSpec((tm, LANES), lambda i: (i, 0)),
        pl.BlockSpec((1, N_MOD, D_MODEL), lambda i: (jnp.where(i < nl, i // tpb, n_batch), 0, 0)),
    ]
    args = [x1, ya, yb, route, mod]
    if final:
        in_specs.append(pl.BlockSpec((1, D_MODEL), lambda i: (0, 0)))
        args.append(gain.reshape(1, D_MODEL))
    return pl.pallas_call(
        functools.partial(_combine_kernel, final=final),
        grid=(nt,),
        in_specs=in_specs,
        out_specs=pl.BlockSpec((tm, D_MODEL), lambda i: (i, 0)),
        out_shape=jax.ShapeDtypeStruct((nt * tm, D_MODEL), F32),
        compiler_params=_cparams(1),
        name="combine_final" if final else "combine",
    )(*args)


def _proj_att_kernel(x_ref, m_ref, w_ref, qc_ref, qa_ref, qb_ref, kc_ref, ka_ref, kb_ref, o_ref, h_ref):
    for a, b in _row_halves(h_ref.shape[0]):
        h_ref[a:b, :] = _modulate(x_ref[a:b, :], m_ref[0, 0:1, :], m_ref[0, 1:2, :]).astype(BF16)

    ones = jnp.ones((LANES, LANES), BF16)

    def normed_rope(xs, c_ref, a_ref, b_ref):
        rs = lax.rsqrt(_dot((xs * xs).astype(BF16), ones) * (1.0 / ATT_HEAD_DIM) + NORM_EPS)
        r = (xs * c_ref[...] + pltpu.roll(xs, LANES - ATT_HEAD_DIM // 4, 1) * a_ref[...]
             + pltpu.roll(xs, ATT_HEAD_DIM // 4, 1) * b_ref[...])
        return (r * rs).astype(BF16)

    gw = PROJ_GROUP
    for c0 in range(0, ATT_IN, gw):
        acc = _dot(h_ref[...], w_ref[:, c0:c0 + gw])
        for g in range(gw // LANES):
            col = c0 + g * LANES
            xs = acc[:, g * LANES:(g + 1) * LANES]
            if col < ATT_QW:
                o_ref[:, col:col + LANES] = normed_rope(xs, qc_ref, qa_ref, qb_ref)
            elif col < ATT_QW + ATT_KW:
                o_ref[:, col:col + LANES] = normed_rope(xs, kc_ref, ka_ref, kb_ref)
            else:
                o_ref[:, col:col + LANES] = xs.astype(BF16)


def _proj_att(xs, mod, w_qkv, q_tabs, k_tabs, geo):
    t_rows, nl, tpb, n_batch = geo
    tm = ROW_TILE
    nt = t_rows // tm
    rope = pl.BlockSpec((tm, ATT_HEAD_DIM), lambda i: (jnp.where(i < nl, i % tpb, tpb), 0))
    return pl.pallas_call(
        _proj_att_kernel,
        grid=(nt,),
        in_specs=[
            pl.BlockSpec((tm, D_MODEL), lambda i: (i, 0)),
            pl.BlockSpec((1, N_MOD, D_MODEL), lambda i: (jnp.where(i < nl, i // tpb, n_batch), 0, 0)),
            pl.BlockSpec((D_MODEL, ATT_IN), lambda i: (0, 0)),
            rope, rope, rope, rope, rope, rope,
        ],
        out_specs=pl.BlockSpec((tm, ATT_IN), lambda i: (i, 0)),
        out_shape=jax.ShapeDtypeStruct((t_rows, ATT_IN), BF16),
        scratch_shapes=[pltpu.VMEM((tm, D_MODEL), BF16)],
        compiler_params=_cparams(1),
        name="proj_att",
    )(xs, mod, w_qkv, *q_tabs, *k_tabs)


def _att_kernel(q_ref, kl_ref, vl_ref, kc_ref, vc_ref, o_ref, k_all, v_ext, s_ref, p_ref):
    tq = q_ref.shape[0]
    seq, ctx_len = kl_ref.shape[0], kc_ref.shape[0]

    @pl.when(pl.program_id(2) == 0)
    def _():
        k_all[0:seq, :] = kl_ref[...]
        k_all[seq:seq + ctx_len, :] = kc_ref[...]
        v_ext[0:seq, 0:LANES] = vl_ref[...]
        v_ext[seq:seq + ctx_len, 0:LANES] = vc_ref[...]
        v_ext[:, LANES:] = jnp.ones((seq + ctx_len, LANES), BF16)

    rows = ATT_SOFTMAX_ROWS

    def scores(g):
        s_ref[g * tq:(g + 1) * tq, :] = _dot_nt(q_ref[:, g * LANES:(g + 1) * LANES], k_all[...])

    def softmax(g):
        for r0 in range(g * tq, (g + 1) * tq, rows):
            m = jnp.max(s_ref[r0:r0 + rows, :], axis=-1, keepdims=True)
            p_ref[r0:r0 + rows, :] = jnp.exp2(s_ref[r0:r0 + rows, :] - m).astype(BF16)

    def weighted_values(g):
        oe = _dot(p_ref[g * tq:(g + 1) * tq, :], v_ext[...])
        o_ref[:, g * LANES:(g + 1) * LANES] = (oe[:, :LANES] / oe[:, LANES:]).astype(BF16)

    scores(0)
    scores(1)
    for g in range(ATT_GROUP):
        if g + 2 < ATT_GROUP:
            scores(g + 2)
        softmax(g)
        weighted_values(g)


def _attention(p, n_batch, seq, ctx_len):
    tq = ATT_Q_TILE
    nq = seq // tq
    cb = n_batch * seq // ctx_len
    ko, vo = ATT_QW // LANES, (ATT_QW + ATT_KW) // LANES
    gw = ATT_GROUP * ATT_HEAD_DIM
    n_keys = seq + ctx_len
    return pl.pallas_call(
        _att_kernel,
        grid=(n_batch, ATT_KV_HEADS, nq),
        in_specs=[
            pl.BlockSpec((tq, gw), lambda b, h, t: (b * nq + t, h)),
            pl.BlockSpec((seq, LANES), lambda b, h, t: (b, ko + h)),
            pl.BlockSpec((seq, LANES), lambda b, h, t: (b, vo + h)),
            pl.BlockSpec((ctx_len, LANES), lambda b, h, t: (cb + b, ko + h)),
            pl.BlockSpec((ctx_len, LANES), lambda b, h, t: (cb + b, vo + h)),
        ],
        out_specs=pl.BlockSpec((tq, gw), lambda b, h, t: (b * nq + t, h)),
        out_shape=jax.ShapeDtypeStruct((n_batch * seq, ATT_QW), BF16),
        scratch_shapes=[pltpu.VMEM((n_keys, LANES), BF16), pltpu.VMEM((n_keys, 2 * LANES), BF16),
                        pltpu.VMEM((ATT_GROUP * tq, n_keys), F32), pltpu.VMEM((ATT_GROUP * tq, n_keys), BF16)],
        compiler_params=_cparams(3),
        name="attention",
    )(p, p, p, p, p)


def kernel(x, c, ctx, c_ctx, w_mod, b_mod, ret_w_in, ret_w_out, ret_log_decay_fwd, ret_log_decay_bwd, att_w_qkv, att_w_o, att_q_gain, att_k_gain, moe_w_group, moe_b_group, moe_w_expert, moe_b_expert, moe_w_gate, moe_w_up, moe_w_down, final_norm_gain):
    n_batch, seq, d = x.shape
    ctx_len = ctx.shape[1]
    tm = ROW_TILE
    assert d == D_MODEL and w_mod.shape[0] == DEPTH == 2
    assert seq % tm == 0 and (n_batch * ctx_len) % tm == 0 and seq % GRID_W == 0
    assert seq % RET_CHUNK == 0 and ctx_len % RET_CHUNK == 0 and (n_batch * seq) % ctx_len == 0
    n_lat = n_batch * seq
    t_rows = n_lat + n_batch * ctx_len
    nl = n_lat // tm
    geo = (t_rows, nl, seq // tm, n_batch)
    nt = t_rows // tm

    x_lat, x_ctx = x.reshape(n_lat, d), ctx.reshape(n_batch * ctx_len, d)
    pad_rows = -(n_batch + 1) % MOD_ROWS_PAD
    c_rows = jnp.concatenate([c, c_ctx[None, :], jnp.zeros((pad_rows, d), F32)], axis=0)
    mod = _mod_vectors(c_rows, w_mod, b_mod)

    cos, sin = _ret_rope_tables(seq, tm)
    p = _proj_ret(x_lat, x_ctx, mod[0], ret_w_in[0].astype(BF16), cos, sin, geo)
    lg = jnp.stack([ret_log_decay_fwd[0], ret_log_decay_bwd[0]]).astype(F32)
    y_lat, y_ctx = _retention(p, lg, n_batch, seq, ctx_len)
    wr, br = _router_weights(moe_w_group[0], moe_b_group[0], moe_w_expert[0], moe_b_expert[0])
    x1, h2p, e_ids, ranks, route, counts = _mixer_out(
        y_lat, y_ctx, x_lat, x_ctx, mod[0], ret_w_out[0].astype(BF16), wr, br, geo, nt)
    ya, yb = _moe(h2p, e_ids, ranks, counts, t_rows, moe_w_gate, moe_w_up, moe_w_down, 0)
    xs = _combine(x1, ya, yb, route, mod[0], geo, nt)

    q_tabs = _att_rope_tables(seq, tm, att_q_gain[0], ATT_HEAD_DIM ** -0.5 * LOG2_E)
    k_tabs = _att_rope_tables(seq, tm, att_k_gain[0], 1.0)
    p = _proj_att(xs, mod[1], att_w_qkv[0].astype(BF16), q_tabs, k_tabs, geo)
    o = _attention(p, n_batch, seq, ctx_len)
    wr, br = _router_weights(moe_w_group[1], moe_b_group[1], moe_w_expert[1], moe_b_expert[1])
    x1, h2p, e_ids, ranks, route, counts = _mixer_out(
        o, None, xs, None, mod[1], att_w_o[0].astype(BF16), wr, br, geo, nl)
    ya, yb = _moe(h2p, e_ids, ranks, counts, n_lat, moe_w_gate, moe_w_up, moe_w_down, 1)
    out = _combine(x1, ya, yb, route, mod[1], geo, nl, gain=final_norm_gain)
    return out.reshape(n_batch, seq, d)
```

```python
import functools

import jax
import jax.numpy as jnp
from jax import lax
from jax.experimental import pallas as pl
from jax.experimental.pallas import tpu as pltpu

F32 = jnp.float32
BF16 = jnp.bfloat16

D_MODEL = 1024
DEPTH = 2
GRID_W = 64
N_MOD = 6
NORM_EPS = 1e-6
HEAD_NORM_EPS = 1e-5
ROPE_THETA = 10000.0

RET_HEADS = 4
RET_DK = D_MODEL // RET_HEADS
RET_DV = 2 * RET_DK
RET_QK = RET_HEADS * RET_DK
RET_V = RET_HEADS * RET_DV
RET_IN = 2 * RET_QK + 2 * RET_V

ATT_HEAD_DIM = 128
ATT_Q_HEADS = D_MODEL // ATT_HEAD_DIM
ATT_KV_HEADS = 2
ATT_GROUP = ATT_Q_HEADS // ATT_KV_HEADS
ATT_QW = ATT_Q_HEADS * ATT_HEAD_DIM
ATT_KW = ATT_KV_HEADS * ATT_HEAD_DIM
ATT_IN = ATT_QW + 2 * ATT_KW

N_GROUPS = 4
EXPERTS_PER_GROUP = 8
N_EXPERTS = N_GROUPS * EXPERTS_PER_GROUP
EXPERT_HIDDEN = D_MODEL // 2

LANES = 128
SUBLANES = 8
ROW_TILE = 512
PROJ_GROUP = 512
RET_CHUNK = 256
ATT_Q_TILE = 256
ATT_SOFTMAX_ROWS = 64
LOG2_E = 1.4426950408889634
EXPERT_BLOCK = 512
SEG_ALIGN = 8
SORT_ROWS = 2 * ROW_TILE + 256
SORT_BLOCK = 256
MOD_ROWS_PAD = 8
VMEM_LIMIT = 56 * 1024 * 1024


def _cparams(n_axes):
    return pltpu.CompilerParams(
        dimension_semantics=("arbitrary",) * n_axes, vmem_limit_bytes=VMEM_LIMIT)


def _silu(v):
    return v * (1.0 / (1.0 + jnp.exp(-v)))


def _modulate(x, shift, scale):
    ms = jnp.mean(x * x, axis=-1, keepdims=True)
    return x * lax.rsqrt(ms + NORM_EPS) * (1.0 + scale) + shift


def _split_bf16(v):
    hi = v.astype(BF16)
    lo = (v - hi.astype(F32)).astype(BF16)
    return hi, lo


def _dot(a, b):
    return jnp.dot(a, b, preferred_element_type=F32)


def _dot_nt(a, b):
    return lax.dot_general(a, b, (((1,), (1,)), ((), ())), preferred_element_type=F32)


def _row_halves(n):
    return ((0, n // 2), (n // 2, n))


def _mod_kernel(c_ref, w_ref, b_ref, o_ref):
    a_hi, a_lo = _split_bf16(_silu(c_ref[...]))
    w_hi, w_lo = _split_bf16(w_ref[0])
    acc = _dot(a_hi, w_hi) + _dot(a_lo, w_hi) + _dot(a_hi, w_lo)
    o_ref[0] = acc + b_ref[0]


def _mod_vectors(c_rows, w_mod, b_mod):
    rows = c_rows.shape[0]
    n = w_mod.shape[-1]
    tn = 1024
    out = pl.pallas_call(
        _mod_kernel,
        grid=(DEPTH, n // tn),
        in_specs=[
            pl.BlockSpec((rows, D_MODEL), lambda l, j: (0, 0)),
            pl.BlockSpec((1, D_MODEL, tn), lambda l, j: (l, 0, j)),
            pl.BlockSpec((1, 1, tn), lambda l, j: (l, 0, j)),
        ],
        out_specs=pl.BlockSpec((1, rows, tn), lambda l, j: (l, 0, j)),
        out_shape=jax.ShapeDtypeStruct((DEPTH, rows, n), F32),
        compiler_params=_cparams(2),
        name="mod_vectors",
    )(c_rows, w_mod, b_mod.reshape(DEPTH, 1, n))
    return out.reshape(DEPTH, rows, N_MOD, D_MODEL)


def _rope_angles(n, head_dim):
    rows = n // GRID_W
    r = jnp.repeat(jnp.arange(rows), GRID_W).astype(F32)
    col = jnp.tile(jnp.arange(GRID_W), rows).astype(F32)
    nf = head_dim // 4
    inv = ROPE_THETA ** (-jnp.arange(nf, dtype=F32) / nf)
    ar = r[:, None] * inv
    ac = col[:, None] * inv
    return jnp.concatenate([ar, ar, ac, ac], axis=-1)


def _ret_rope_tables(n, pad):
    ang = _rope_angles(n, RET_DK)
    lane = jnp.arange(RET_DK) % LANES
    sin = jnp.where(lane < LANES // 2, -jnp.sin(ang), jnp.sin(ang))
    cos = jnp.concatenate([jnp.cos(ang), jnp.ones((pad, RET_DK), F32)], axis=0)
    sin = jnp.concatenate([sin, jnp.zeros((pad, RET_DK), F32)], axis=0)
    return cos, sin


def _att_rope_tables(n, pad, gain, scale):
    ang = _rope_angles(n, ATT_HEAD_DIM)
    quarter = ATT_HEAD_DIM // 4
    first = jnp.arange(ATT_HEAD_DIM) % (2 * quarter) < quarter
    sin = jnp.sin(ang)
    c = jnp.cos(ang) * gain * scale
    a = jnp.where(first, -sin, 0.0) * jnp.roll(gain, ATT_HEAD_DIM - quarter) * scale
    b = jnp.where(first, 0.0, sin) * jnp.roll(gain, quarter) * scale
    ident = jnp.broadcast_to(gain * scale, (pad, ATT_HEAD_DIM))
    zeros = jnp.zeros((pad, ATT_HEAD_DIM), F32)
    return (jnp.concatenate([c, ident], axis=0), jnp.concatenate([a, zeros], axis=0),
            jnp.concatenate([b, zeros], axis=0))


def _load_rows(is_lat, lat_ref, ctx_ref, a, b):
    return jnp.where(is_lat, lat_ref[a:b, :], ctx_ref[a:b, :])


def _proj_ret_kernel(xl_ref, xc_ref, m_ref, w_ref, cos_ref, sin_ref, o_ref, h_ref, *, nl):
    is_lat = pl.program_id(0) < nl
    for a, b in _row_halves(h_ref.shape[0]):
        x = _load_rows(is_lat, xl_ref, xc_ref, a, b)
        h_ref[a:b, :] = _modulate(x, m_ref[0, 0:1, :], m_ref[0, 1:2, :]).astype(BF16)

    gw = PROJ_GROUP
    for c0 in range(0, RET_IN, gw):
        acc = _dot(h_ref[...], w_ref[:, c0:c0 + gw])
        if c0 < 2 * RET_QK:
            scale = 1.0 if c0 < RET_QK else RET_DK ** -0.5
            for g in range(gw // LANES):
                xs = acc[:, g * LANES:(g + 1) * LANES]
                t = ((c0 // LANES + g) % 2) * LANES
                r = xs * cos_ref[:, t:t + LANES] + pltpu.roll(xs, LANES // 2, 1) * sin_ref[:, t:t + LANES]
                o_ref[:, c0 + g * LANES:c0 + (g + 1) * LANES] = (r * scale).astype(BF16)
        elif c0 < 2 * RET_QK + RET_V:
            o_ref[:, c0:c0 + gw] = acc.astype(BF16)
        else:
            o_ref[:, c0:c0 + gw] = _silu(acc).astype(BF16)


def _proj_ret(x_lat, x_ctx, mod, w_in, cos, sin, geo):
    t_rows, nl, tpb, n_batch = geo
    tm = ROW_TILE
    nt = t_rows // tm
    return pl.pallas_call(
        functools.partial(_proj_ret_kernel, nl=nl),
        grid=(nt,),
        in_specs=[
            pl.BlockSpec((tm, D_MODEL), lambda i: (jnp.minimum(i, nl - 1), 0)),
            pl.BlockSpec((tm, D_MODEL), lambda i: (jnp.maximum(i - nl, 0), 0)),
            pl.BlockSpec((1, N_MOD, D_MODEL), lambda i: (jnp.where(i < nl, i // tpb, n_batch), 0, 0)),
            pl.BlockSpec((D_MODEL, RET_IN), lambda i: (0, 0)),
            pl.BlockSpec((tm, RET_DK), lambda i: (jnp.where(i < nl, i % tpb, tpb), 0)),
            pl.BlockSpec((tm, RET_DK), lambda i: (jnp.where(i < nl, i % tpb, tpb), 0)),
        ],
        out_specs=pl.BlockSpec((tm, RET_IN), lambda i: (i, 0)),
        out_shape=jax.ShapeDtypeStruct((t_rows, RET_IN), BF16),
        scratch_shapes=[pltpu.VMEM((tm, D_MODEL), BF16)],
        compiler_params=_cparams(1),
        name="proj_ret",
    )(x_lat, x_ctx, mod, w_in, cos, sin)


def _ret_kernel(lg_ref, ql, kl, vl, gl, qc, kc, vc, gc, ol, oc,
                sf_ref, sb_ref, y_ref, mask_ref):
    hd = pl.program_id(0)
    lgf = lg_ref[0, hd]
    lgb = lg_ref[1, hd]
    c = RET_CHUNK

    @pl.when(pl.program_id(1) == 0)
    def _():
        ii = lax.broadcasted_iota(jnp.int32, (c, c), 0).astype(F32)
        jj = lax.broadcasted_iota(jnp.int32, (c, c), 1).astype(F32)
        d = ii - jj
        mask_ref[...] = (jnp.where(d >= 0, jnp.exp(lgf * jnp.maximum(d, 0.0)), 0.0)
                         + jnp.where(d <= 0, jnp.exp(lgb * jnp.maximum(-d, 0.0)), 0.0))

    pos = lax.broadcasted_iota(jnp.int32, (c, 1), 0).astype(F32)
    xi_f = jnp.exp(lgf * (pos + 1.0))
    zeta_f = jnp.exp(lgf * (c - 1.0 - pos))
    xi_b = jnp.exp(lgb * (c - pos))
    zeta_b = jnp.exp(lgb * pos)
    dec_f = jnp.exp(jnp.full((1, RET_DV), lgf * c, F32))
    dec_b = jnp.exp(jnp.full((1, RET_DV), lgb * c, F32))

    def state_update(s_ref, k, v, zeta, dec):
        kz = (k.astype(F32) * zeta).T.astype(BF16)
        s_ref[...] = s_ref[...] * dec + _dot(kz, v)

    def run(q_ref, k_ref, v_ref, g_ref, o_ref, n):
        def fwd(t, carry):
            r0 = pl.multiple_of(t * c, c)
            q = q_ref[pl.ds(r0, c), :]
            k = k_ref[pl.ds(r0, c), :]
            v = v_ref[pl.ds(r0, c), :]
            a = (_dot_nt(q, k) * mask_ref[...]).astype(BF16)
            cross = _dot(q, sf_ref[...].astype(BF16)) * xi_f
            y_ref[pl.ds(r0, c), :] = _dot(a, v) + cross
            state_update(sf_ref, k, v, zeta_f, dec_f)
            return carry

        lax.fori_loop(0, n, fwd, 0)

        def bwd(t, carry):
            r0 = pl.multiple_of((n - 1 - t) * c, c)
            q = q_ref[pl.ds(r0, c), :]
            k = k_ref[pl.ds(r0, c), :]
            v = v_ref[pl.ds(r0, c), :]
            y = y_ref[pl.ds(r0, c), :] + _dot(q, sb_ref[...].astype(BF16)) * xi_b
            mu = jnp.mean(y, axis=-1, keepdims=True)
            yc = y - mu
            var = jnp.mean(yc * yc, axis=-1, keepdims=True)
            yn = yc * lax.rsqrt(var + HEAD_NORM_EPS)
            o_ref[pl.ds(r0, c), :] = (yn * g_ref[pl.ds(r0, c), :].astype(F32)).astype(BF16)
            state_update(sb_ref, k, v, zeta_b, dec_b)
            return carry

        lax.fori_loop(0, n, bwd, 0)

    sf_ref[...] = jnp.zeros_like(sf_ref)
    sb_ref[...] = jnp.zeros_like(sb_ref)
    run(qc, kc, vc, gc, oc, qc.shape[0] // c)
    run(ql, kl, vl, gl, ol, ql.shape[0] // c)


def _retention(p, lg, n_batch, seq, ctx_len):
    cb = n_batch * seq // ctx_len
    qo, ko = 0, RET_QK // RET_DK
    vo, go = 2 * RET_QK // RET_DV, (2 * RET_QK + RET_V) // RET_DV
    lat = lambda w, off: pl.BlockSpec((seq, w), lambda h, b, lg_: (b, off + h))
    ctx = lambda w, off: pl.BlockSpec((ctx_len, w), lambda h, b, lg_: (cb + b, off + h))
    return pl.pallas_call(
        _ret_kernel,
        grid_spec=pltpu.PrefetchScalarGridSpec(
            num_scalar_prefetch=1,
            grid=(RET_HEADS, n_batch),
            in_specs=[lat(RET_DK, qo), lat(RET_DK, ko), lat(RET_DV, vo), lat(RET_DV, go),
                      ctx(RET_DK, qo), ctx(RET_DK, ko), ctx(RET_DV, vo), ctx(RET_DV, go)],
            out_specs=[pl.BlockSpec((seq, RET_DV), lambda h, b, lg_: (b, h)),
                       pl.BlockSpec((ctx_len, RET_DV), lambda h, b, lg_: (b, h))],
            scratch_shapes=[pltpu.VMEM((RET_DK, RET_DV), F32), pltpu.VMEM((RET_DK, RET_DV), F32),
                            pltpu.VMEM((seq, RET_DV), F32), pltpu.VMEM((RET_CHUNK, RET_CHUNK), F32)],
        ),
        out_shape=[jax.ShapeDtypeStruct((n_batch * seq, RET_V), BF16),
                   jax.ShapeDtypeStruct((n_batch * ctx_len, RET_V), BF16)],
        compiler_params=_cparams(2),
        name="retention",
    )(lg, p, p, p, p, p, p, p, p)


def _route(h2, wr_ref, br_ref):
    hi, lo = _split_bf16(h2)
    l1 = _dot(hi, wr_ref[...])
    logits = l1[:, :LANES] + l1[:, LANES:] + _dot(lo, wr_ref[:, :LANES]) + br_ref[...]
    lane = lax.broadcasted_iota(jnp.int32, logits.shape, 1).astype(F32)
    neg = -jnp.inf
    big = float(LANES)
    is_g = lane < N_GROUPS
    gl = jnp.where(is_g, logits, neg)
    gm = jnp.max(gl, axis=-1, keepdims=True)
    gidx = jnp.min(jnp.where(gl == gm, lane, big), axis=-1, keepdims=True)
    gsum = jnp.sum(jnp.where(is_g, jnp.exp(jnp.where(is_g, logits, gm) - gm), 0.0), axis=-1, keepdims=True)
    g_p = 1.0 / gsum
    lo_l = N_GROUPS + EXPERTS_PER_GROUP * gidx
    in_grp = (lane >= lo_l) & (lane < lo_l + EXPERTS_PER_GROUP)
    el = jnp.where(in_grp, logits, neg)
    e1 = jnp.max(el, axis=-1, keepdims=True)
    i1 = jnp.min(jnp.where(el == e1, lane, big), axis=-1, keepdims=True)
    el2 = jnp.where(lane == i1, neg, el)
    e2 = jnp.max(el2, axis=-1, keepdims=True)
    i2 = jnp.min(jnp.where(el2 == e2, lane, big), axis=-1, keepdims=True)
    t = jnp.exp(e2 - e1)
    return lane, i1, i2, g_p / (1.0 + t), g_p * t / (1.0 + t)


def _place_in_tile(lane, i1, i2, tri_ref):
    onehot = jnp.where((lane == i1) | (lane == i2), 1.0, 0.0)
    before = _dot(tri_ref[...], onehot.astype(BF16))
    counts = jnp.sum(onehot, axis=0, keepdims=True)
    runs = jnp.floor((counts + (SEG_ALIGN - 1)) * (1.0 / SEG_ALIGN)) * SEG_ALIGN
    li = lax.broadcasted_iota(jnp.int32, (LANES, LANES), 0)
    lj = lax.broadcasted_iota(jnp.int32, (LANES, LANES), 1)
    upper = jnp.where(li < lj, 1.0, 0.0).astype(BF16)
    starts = _dot(jnp.broadcast_to(runs, (SUBLANES, LANES)).astype(BF16), upper)[0:1, :]
    p1 = jnp.sum(jnp.where(lane == i1, before + starts, 0.0), axis=-1, keepdims=True)
    p2 = jnp.sum(jnp.where(lane == i2, before + starts, 0.0), axis=-1, keepdims=True)
    return p1, p2, runs


def _unpack_bf16_pairs(u):
    lo = lax.bitcast_convert_type(lax.shift_left(u, 16), F32)
    hi = lax.bitcast_convert_type(u & jnp.int32(-65536), F32)
    return jnp.concatenate([lo, hi], axis=1).astype(BF16)


def _out_kernel(*refs, nl, has_ctx):
    if has_ctx:
        (a_lat, a_ctx, x_lat, x_ctx, m_ref, w_ref, wr_ref, br_ref, tri_ref,
         x1_ref, h2_ref, e_ref, k_ref, g_ref, c_ref) = refs
        is_lat = pl.program_id(0) < nl
    else:
        a_lat, x_lat, m_ref, w_ref, wr_ref, br_ref, tri_ref, x1_ref, h2_ref, e_ref, k_ref, g_ref, c_ref = refs
    picks = []
    for a, b in _row_halves(x1_ref.shape[0]):
        lhs = _load_rows(is_lat, a_lat, a_ctx, a, b) if has_ctx else a_lat[a:b, :]
        x = _load_rows(is_lat, x_lat, x_ctx, a, b) if has_ctx else x_lat[a:b, :]
        x1 = x + m_ref[0, 2:3, :] * _dot(lhs, w_ref[...])
        x1_ref[a:b, :] = x1
        h2 = _modulate(x1, m_ref[0, 3:4, :], m_ref[0, 4:5, :])
        h2_ref[a:b, :] = h2.astype(BF16)
        picks.append(_route(h2, wr_ref, br_ref))
    lane, i1, i2, ga, gb = (jnp.concatenate(v, axis=0) for v in zip(*picks))
    p1, p2, runs = _place_in_tile(lane, i1, i2, tri_ref)
    e_ref[...] = jnp.concatenate([i1, i2], axis=1).astype(jnp.int32) - N_GROUPS
    k_ref[...] = jnp.concatenate([p1, p2], axis=1).astype(jnp.int32)
    g_ref[...] = jnp.where(lane == 2, ga, jnp.where(lane == 3, gb, 0.0))
    c_ref[0] = jnp.broadcast_to(runs, c_ref.shape[1:])


def _mixer_out(a_lat, a_ctx, x_lat, x_ctx, mod, w_out, wr, br, geo, nt):
    t_rows, nl, tpb, n_batch = geo
    tm = ROW_TILE
    kdim = w_out.shape[0]
    has_ctx = a_ctx is not None
    mrow = lambda i: (jnp.where(i < nl, i // tpb, n_batch), 0, 0)
    lat = lambda w: pl.BlockSpec((tm, w), lambda i: (jnp.minimum(i, nl - 1), 0))
    ctx = lambda w: pl.BlockSpec((tm, w), lambda i: (jnp.maximum(i - nl, 0), 0))
    if has_ctx:
        in_specs = [lat(kdim), ctx(kdim), lat(D_MODEL), ctx(D_MODEL)]
        args = [a_lat, a_ctx, x_lat, x_ctx]
    else:
        in_specs = [lat(kdim), lat(D_MODEL)]
        args = [a_lat, x_lat]
    in_specs += [
        pl.BlockSpec((1, N_MOD, D_MODEL), mrow),
        pl.BlockSpec((kdim, D_MODEL), lambda i: (0, 0)),
        pl.BlockSpec((D_MODEL, 2 * LANES), lambda i: (0, 0)),
        pl.BlockSpec((1, LANES), lambda i: (0, 0)),
        pl.BlockSpec((tm, tm), lambda i: (0, 0)),
    ]
    tri = (jnp.arange(tm)[:, None] > jnp.arange(tm)[None, :]).astype(BF16)
    args += [mod, w_out, wr, br, tri]
    rows = nt * tm
    row_out = lambda w: pl.BlockSpec((tm, w), lambda i: (i, 0))
    return pl.pallas_call(
        functools.partial(_out_kernel, nl=nl, has_ctx=has_ctx),
        grid=(nt,),
        in_specs=in_specs,
        out_specs=[row_out(D_MODEL), row_out(D_MODEL), row_out(2), row_out(2), row_out(LANES),
                   pl.BlockSpec((1, SUBLANES, LANES), lambda i: (i, 0, 0))],
        out_shape=[jax.ShapeDtypeStruct((rows, D_MODEL), F32),
                   jax.ShapeDtypeStruct((rows, D_MODEL), BF16),
                   jax.ShapeDtypeStruct((rows, 2), jnp.int32),
                   jax.ShapeDtypeStruct((rows, 2), jnp.int32),
                   jax.ShapeDtypeStruct((rows, LANES), F32),
                   jax.ShapeDtypeStruct((nt, SUBLANES, LANES), F32)],
        compiler_params=_cparams(1),
        name="mixer_out_ctx" if has_ctx else "mixer_out",
    )(*args)


def _router_weights(w_group, b_group, w_expert, b_expert):
    w = jnp.concatenate([w_group, w_expert], axis=1)
    w = jnp.pad(w, ((0, 0), (0, LANES - w.shape[1])))
    hi = w.astype(BF16)
    lo = (w - hi.astype(F32)).astype(BF16)
    b = jnp.pad(jnp.concatenate([b_group, b_expert]), (0, LANES - N_GROUPS - N_EXPERTS))
    return jnp.concatenate([hi, lo], axis=1), b.reshape(1, LANES).astype(F32)


def _slot_tables(runs, n_tok):
    blk = EXPERT_BLOCK
    nt = runs.shape[0]
    run = runs[:, 0, N_GROUPS:N_GROUPS + N_EXPERTS].astype(jnp.int32)
    total = jnp.sum(run, axis=0)
    padded = (total + blk - 1) // blk * blk
    ends_p = jnp.cumsum(padded)
    base = (ends_p - padded)[None, :] + jnp.cumsum(run, axis=0) - run
    ends_t = jnp.cumsum(run, axis=1)
    shift = base - (ends_t - run)
    shift_rows = jnp.pad(shift.astype(F32), ((0, 0), (N_GROUPS, LANES - N_GROUPS - N_EXPERTS)))[:, None, :]
    chunk_row = jnp.arange(SORT_ROWS // SEG_ALIGN, dtype=jnp.int32) * SEG_ALIGN
    owner = jnp.minimum(jnp.sum(ends_t[:, None, :] <= chunk_row[None, :, None], axis=-1), N_EXPERTS - 1)
    chunk_shift = jnp.take_along_axis(shift, owner, axis=1)[:, None, :]
    n_chunks = ends_t[:, -1] // SEG_ALIGN
    n_blocks = -(-(2 * n_tok + nt * N_EXPERTS * (SEG_ALIGN - 1)) // blk) + N_EXPERTS
    first_slot = jnp.arange(n_blocks, dtype=jnp.int32) * blk
    block_e = jnp.minimum(jnp.sum(ends_p[None, :] <= first_slot[:, None], axis=1), N_EXPERTS - 1).astype(jnp.int32)
    n_valid = (ends_p[-1] // blk).astype(jnp.int32).reshape(1)
    return n_chunks, chunk_shift, shift_rows, block_e, n_valid, n_blocks


def _dispatch_kernel(nchunk_ref, cshift_ref, h_ref, e_ref, pos_ref, srow_ref, init_ref,
                     o_ref, dest_ref, perm_ref, sorted_ref, sem):
    del init_ref
    tm = h_ref.shape[0]
    i = pl.program_id(0)
    pos = pos_ref[...].astype(F32)
    lane = lax.broadcasted_iota(jnp.int32, (tm, LANES), 1).astype(F32)

    e_lane = e_ref[...].astype(F32) + N_GROUPS
    dest = [jnp.sum(jnp.where(lane == e_lane[:, k:k + 1], srow_ref[0], 0.0), axis=-1, keepdims=True)
            + pos[:, k:k + 1] for k in range(2)]
    dest_ref[...] = jnp.concatenate(dest, axis=1).astype(jnp.int32)

    pos_row = [jnp.transpose(jnp.where(lane == 0.0, pos[:, k:k + 1], 0.0))[0:1, :] for k in range(2)]
    for s0 in range(0, SORT_ROWS, SORT_BLOCK):
        s = (lax.broadcasted_iota(jnp.int32, (SORT_BLOCK, tm), 0) + s0).astype(F32)
        perm_ref[s0:s0 + SORT_BLOCK, :] = jnp.where((s == pos_row[0]) | (s == pos_row[1]), 1.0, 0.0).astype(BF16)
    half = D_MODEL // 2
    lo = lax.bitcast_convert_type(_dot(perm_ref[...], h_ref[:, :half]), jnp.int32)
    hi = lax.bitcast_convert_type(_dot(perm_ref[...], h_ref[:, half:]), jnp.int32)
    sorted_ref[...] = lax.shift_right_logical(lo, 16) | (hi & jnp.int32(-65536))

    def chunk_copy(c):
        r = pl.multiple_of(c * SEG_ALIGN, SEG_ALIGN)
        d = pl.multiple_of(r + cshift_ref[0, 0, c], SEG_ALIGN)
        return pltpu.make_async_copy(sorted_ref.at[pl.ds(r, SEG_ALIGN), :], o_ref.at[pl.ds(d, SEG_ALIGN), :], sem)

    def start(c, carry):
        chunk_copy(c).start()
        return carry

    def wait(c, carry):
        chunk_copy(c).wait()
        return carry

    lax.fori_loop(0, nchunk_ref[i], start, 0)
    lax.fori_loop(0, nchunk_ref[i], wait, 0)


def _dispatch(h2, e_ids, pos, n_chunks, chunk_shift, shift_rows, n_slots):
    n_tok = h2.shape[0]
    tm = ROW_TILE
    nt = n_tok // tm
    width = D_MODEL // 2
    return pl.pallas_call(
        _dispatch_kernel,
        grid_spec=pltpu.PrefetchScalarGridSpec(
            num_scalar_prefetch=1,
            grid=(nt,),
            in_specs=[pl.BlockSpec((1, 1, SORT_ROWS // SEG_ALIGN), lambda i, n: (i, 0, 0), memory_space=pltpu.SMEM),
                      pl.BlockSpec((tm, D_MODEL), lambda i, n: (i, 0)),
                      pl.BlockSpec((tm, 2), lambda i, n: (i, 0)),
                      pl.BlockSpec((tm, 2), lambda i, n: (i, 0)),
                      pl.BlockSpec((1, 1, LANES), lambda i, n: (i, 0, 0)),
                      pl.BlockSpec(memory_space=pl.ANY)],
            out_specs=[pl.BlockSpec(memory_space=pl.ANY),
                       pl.BlockSpec((tm, 2), lambda i, n: (i, 0))],
            scratch_shapes=[pltpu.VMEM((SORT_ROWS, tm), BF16), pltpu.VMEM((SORT_ROWS, width), jnp.int32),
                            pltpu.SemaphoreType.DMA(())],
        ),
        out_shape=[jax.ShapeDtypeStruct((n_slots, width), jnp.int32),
                   jax.ShapeDtypeStruct((n_tok, 2), jnp.int32)],
        input_output_aliases={6: 0},
        compiler_params=_cparams(1),
        name="dispatch",
    )(n_chunks, chunk_shift, h2, e_ids, pos, shift_rows, jnp.zeros((n_slots, width), jnp.int32))


def _expert_kernel(be_ref, nv_ref, x_ref, wg_ref, wu_ref, wd_ref, o_ref, wg_b, wu_b, wd_b):
    i = pl.program_id(0)
    valid = i < nv_ref[0]
    new_expert = (i == 0) | (be_ref[i] != be_ref[jnp.maximum(i - 1, 0)])

    @pl.when(valid & new_expert)
    def _():
        wg_b[...] = wg_ref[0, 0].astype(BF16)
        wu_b[...] = wu_ref[0, 0].astype(BF16)
        wd_b[...] = wd_ref[0, 0].astype(BF16)

    @pl.when(valid)
    def _():
        x = _unpack_bf16_pairs(x_ref[...])
        mid = (_silu(_dot(x, wg_b[...])) * _dot(x, wu_b[...])).astype(BF16)
        for a, b in _row_halves(o_ref.shape[1]):
            o_ref[:, a:b] = _dot(mid, wd_b[:, a:b]).astype(BF16)

    @pl.when(jnp.logical_not(valid))
    def _():
        o_ref[...] = jnp.zeros_like(o_ref)


def _expert_ffn(xs_sorted, block_e, n_valid, w_gate, w_up, w_down, layer):
    n_slots = xs_sorted.shape[0]
    blk = EXPERT_BLOCK
    wspec = lambda r, c: pl.BlockSpec((1, 1, r, c), lambda i, be, nv: (layer, be[i], 0, 0))
    return pl.pallas_call(
        _expert_kernel,
        grid_spec=pltpu.PrefetchScalarGridSpec(
            num_scalar_prefetch=2,
            grid=(n_slots // blk,),
            in_specs=[
                pl.BlockSpec((blk, D_MODEL // 2), lambda i, be, nv: (i, 0)),
                wspec(D_MODEL, EXPERT_HIDDEN), wspec(D_MODEL, EXPERT_HIDDEN), wspec(EXPERT_HIDDEN, D_MODEL),
            ],
            out_specs=pl.BlockSpec((blk, D_MODEL), lambda i, be, nv: (i, 0)),
            scratch_shapes=[pltpu.VMEM((D_MODEL, EXPERT_HIDDEN), BF16),
                            pltpu.VMEM((D_MODEL, EXPERT_HIDDEN), BF16),
                            pltpu.VMEM((EXPERT_HIDDEN, D_MODEL), BF16)],
        ),
        out_shape=jax.ShapeDtypeStruct((n_slots, D_MODEL), BF16),
        compiler_params=_cparams(1),
        name="expert_ffn",
    )(block_e, n_valid, xs_sorted, w_gate, w_up, w_down)


def _moe(h2, e_ids, pos, runs, n_tok, w_gate, w_up, w_down, layer):
    n_chunks, chunk_shift, shift_rows, block_e, n_valid, n_blocks = _slot_tables(runs, n_tok)
    xs_sorted, dest = _dispatch(h2, e_ids, pos, n_chunks, chunk_shift, shift_rows, n_blocks * EXPERT_BLOCK)
    ys = _expert_ffn(xs_sorted, block_e, n_valid, w_gate, w_up, w_down, layer)
    return ys[dest[:, 0]], ys[dest[:, 1]]


def _combine_kernel(*refs, final):
    if final:
        x1_ref, ya_ref, yb_ref, r_ref, m_ref, gain_ref, o_ref = refs
    else:
        x1_ref, ya_ref, yb_ref, r_ref, m_ref, o_ref = refs
    r = r_ref[...]
    f = r[:, 2:3] * ya_ref[...].astype(F32) + r[:, 3:4] * yb_ref[...].astype(F32)
    x2 = x1_ref[...] + m_ref[0, 5:6, :] * f
    if final:
        ms = jnp.mean(x2 * x2, axis=-1, keepdims=True)
        x2 = x2 * lax.rsqrt(ms + NORM_EPS) * gain_ref[...]
    o_ref[...] = x2


def _combine(x1, ya, yb, route, mod, geo, nt, gain=None):
    t_rows, nl, tpb, n_batch = geo
    tm = ROW_TILE
    final = gain is not None
    in_specs = [
        pl.BlockSpec((tm, D_MODEL), lambda i: (i, 0)),
        pl.BlockSpec((tm, D_MODEL), lambda i: (i, 0)),
        pl.BlockSpec((tm, D_MODEL), lambda i: (i, 0)),
        pl.BlockSpec((tm, LANES), lambda i: (i, 0)),
        pl.BlockSpec((1, N_MOD, D_MODEL), lambda i: (jnp.where(i < nl, i // tpb, n_batch), 0, 0)),
    ]
    args = [x1, ya, yb, route, mod]
    if final:
        in_specs.append(pl.BlockSpec((1, D_MODEL), lambda i: (0, 0)))
        args.append(gain.reshape(1, D_MODEL))
    return pl.pallas_call(
        functools.partial(_combine_kernel, final=final),
        grid=(nt,),
        in_specs=in_specs,
        out_specs=pl.BlockSpec((tm, D_MODEL), lambda i: (i, 0)),
        out_shape=jax.ShapeDtypeStruct((nt * tm, D_MODEL), F32),
        compiler_params=_cparams(1),
        name="combine_final" if final else "combine",
    )(*args)


def _proj_att_kernel(x_ref, m_ref, w_ref, qc_ref, qa_ref, qb_ref, kc_ref, ka_ref, kb_ref, o_ref, h_ref):
    for a, b in _row_halves(h_ref.shape[0]):
        h_ref[a:b, :] = _modulate(x_ref[a:b, :], m_ref[0, 0:1, :], m_ref[0, 1:2, :]).astype(BF16)

    ones = jnp.ones((LANES, LANES), BF16)

    def normed_rope(xs, c_ref, a_ref, b_ref):
        rs = lax.rsqrt(_dot((xs * xs).astype(BF16), ones) * (1.0 / ATT_HEAD_DIM) + NORM_EPS)
        r = (xs * c_ref[...] + pltpu.roll(xs, LANES - ATT_HEAD_DIM // 4, 1) * a_ref[...]
             + pltpu.roll(xs, ATT_HEAD_DIM // 4, 1) * b_ref[...])
        return (r * rs).astype(BF16)

    gw = PROJ_GROUP
    for c0 in range(0, ATT_IN, gw):
        acc = _dot(h_ref[...], w_ref[:, c0:c0 + gw])
        for g in range(gw // LANES):
            col = c0 + g * LANES
            xs = acc[:, g * LANES:(g + 1) * LANES]
            if col < ATT_QW:
                o_ref[:, col:col + LANES] = normed_rope(xs, qc_ref, qa_ref, qb_ref)
            elif col < ATT_QW + ATT_KW:
                o_ref[:, col:col + LANES] = normed_rope(xs, kc_ref, ka_ref, kb_ref)
            else:
                o_ref[:, col:col + LANES] = xs.astype(BF16)


def _proj_att(xs, mod, w_qkv, q_tabs, k_tabs, geo):
    t_rows, nl, tpb, n_batch = geo
    tm = ROW_TILE
    nt = t_rows // tm
    rope = pl.BlockSpec((tm, ATT_HEAD_DIM), lambda i: (jnp.where(i < nl, i % tpb, tpb), 0))
    return pl.pallas_call(
        _proj_att_kernel,
        grid=(nt,),
        in_specs=[
            pl.BlockSpec((tm, D_MODEL), lambda i: (i, 0)),
            pl.BlockSpec((1, N_MOD, D_MODEL), lambda i: (jnp.where(i < nl, i // tpb, n_batch), 0, 0)),
            pl.BlockSpec((D_MODEL, ATT_IN), lambda i: (0, 0)),
            rope, rope, rope, rope, rope, rope,
        ],
        out_specs=pl.BlockSpec((tm, ATT_IN), lambda i: (i, 0)),
        out_shape=jax.ShapeDtypeStruct((t_rows, ATT_IN), BF16),
        scratch_shapes=[pltpu.VMEM((tm, D_MODEL), BF16)],
        compiler_params=_cparams(1),
        name="proj_att",
    )(xs, mod, w_qkv, *q_tabs, *k_tabs)


def _att_kernel(q_ref, kl_ref, vl_ref, kc_ref, vc_ref, o_ref, k_all, v_ext, s_ref, p_ref):
    tq = q_ref.shape[0]
    seq, ctx_len = kl_ref.shape[0], kc_ref.shape[0]

    @pl.when(pl.program_id(2) == 0)
    def _():
        k_all[0:seq, :] = kl_ref[...]
        k_all[seq:seq + ctx_len, :] = kc_ref[...]
        v_ext[0:seq, 0:LANES] = vl_ref[...]
        v_ext[seq:seq + ctx_len, 0:LANES] = vc_ref[...]
        v_ext[:, LANES:] = jnp.ones((seq + ctx_len, LANES), BF16)

    rows = ATT_SOFTMAX_ROWS

    def scores(g):
        s_ref[g * tq:(g + 1) * tq, :] = _dot_nt(q_ref[:, g * LANES:(g + 1) * LANES], k_all[...])

    def softmax(g):
        for r0 in range(g * tq, (g + 1) * tq, rows):
            m = jnp.max(s_ref[r0:r0 + rows, :], axis=-1, keepdims=True)
            p_ref[r0:r0 + rows, :] = jnp.exp2(s_ref[r0:r0 + rows, :] - m).astype(BF16)

    def weighted_values(g):
        oe = _dot(p_ref[g * tq:(g + 1) * tq, :], v_ext[...])
        o_ref[:, g * LANES:(g + 1) * LANES] = (oe[:, :LANES] / oe[:, LANES:]).astype(BF16)

    scores(0)
    scores(1)
    for g in range(ATT_GROUP):
        if g + 2 < ATT_GROUP:
            scores(g + 2)
        softmax(g)
        weighted_values(g)


def _attention(p, n_batch, seq, ctx_len):
    tq = ATT_Q_TILE
    nq = seq // tq
    cb = n_batch * seq // ctx_len
    ko, vo = ATT_QW // LANES, (ATT_QW + ATT_KW) // LANES
    gw = ATT_GROUP * ATT_HEAD_DIM
    n_keys = seq + ctx_len
    return pl.pallas_call(
        _att_kernel,
        grid=(n_batch, ATT_KV_HEADS, nq),
        in_specs=[
            pl.BlockSpec((tq, gw), lambda b, h, t: (b * nq + t, h)),
            pl.BlockSpec((seq, LANES), lambda b, h, t: (b, ko + h)),
            pl.BlockSpec((seq, LANES), lambda b, h, t: (b, vo + h)),
            pl.BlockSpec((ctx_len, LANES), lambda b, h, t: (cb + b, ko + h)),
            pl.BlockSpec((ctx_len, LANES), lambda b, h, t: (cb + b, vo + h)),
        ],
        out_specs=pl.BlockSpec((tq, gw), lambda b, h, t: (b * nq + t, h)),
        out_shape=jax.ShapeDtypeStruct((n_batch * seq, ATT_QW), BF16),
        scratch_shapes=[pltpu.VMEM((n_keys, LANES), BF16), pltpu.VMEM((n_keys, 2 * LANES), BF16),
                        pltpu.VMEM((ATT_GROUP * tq, n_keys), F32), pltpu.VMEM((ATT_GROUP * tq, n_keys), BF16)],
        compiler_params=_cparams(3),
        name="attention",
    )(p, p, p, p, p)


def kernel(x, c, ctx, c_ctx, w_mod, b_mod, ret_w_in, ret_w_out, ret_log_decay_fwd, ret_log_decay_bwd, att_w_qkv, att_w_o, att_q_gain, att_k_gain, moe_w_group, moe_b_group, moe_w_expert, moe_b_expert, moe_w_gate, moe_w_up, moe_w_down, final_norm_gain):
    n_batch, seq, d = x.shape
    ctx_len = ctx.shape[1]
    tm = ROW_TILE
    assert d == D_MODEL and w_mod.shape[0] == DEPTH == 2
    assert seq % tm == 0 and (n_batch * ctx_len) % tm == 0 and seq % GRID_W == 0
    assert seq % RET_CHUNK == 0 and ctx_len % RET_CHUNK == 0 and (n_batch * seq) % ctx_len == 0
    n_lat = n_batch * seq
    t_rows = n_lat + n_batch * ctx_len
    nl = n_lat // tm
    geo = (t_rows, nl, seq // tm, n_batch)
    nt = t_rows // tm

    x_lat, x_ctx = x.reshape(n_lat, d), ctx.reshape(n_batch * ctx_len, d)
    pad_rows = -(n_batch + 1) % MOD_ROWS_PAD
    c_rows = jnp.concatenate([c, c_ctx[None, :], jnp.zeros((pad_rows, d), F32)], axis=0)
    mod = _mod_vectors(c_rows, w_mod, b_mod)

    cos, sin = _ret_rope_tables(seq, tm)
    p = _proj_ret(x_lat, x_ctx, mod[0], ret_w_in[0].astype(BF16), cos, sin, geo)
    lg = jnp.stack([ret_log_decay_fwd[0], ret_log_decay_bwd[0]]).astype(F32)
    y_lat, y_ctx = _retention(p, lg, n_batch, seq, ctx_len)
    wr, br = _router_weights(moe_w_group[0], moe_b_group[0], moe_w_expert[0], moe_b_expert[0])
    x1, h2, e_ids, pos, route, runs = _mixer_out(
        y_lat, y_ctx, x_lat, x_ctx, mod[0], ret_w_out[0].astype(BF16), wr, br, geo, nt)
    ya, yb = _moe(h2, e_ids, pos, runs, t_rows, moe_w_gate, moe_w_up, moe_w_down, 0)
    xs = _combine(x1, ya, yb, route, mod[0], geo, nt)

    q_tabs = _att_rope_tables(seq, tm, att_q_gain[0], ATT_HEAD_DIM ** -0.5 * LOG2_E)
    k_tabs = _att_rope_tables(seq, tm, att_k_gain[0], 1.0)
    p = _proj_att(xs, mod[1], att_w_qkv[0].astype(BF16), q_tabs, k_tabs, geo)
    o = _attention(p, n_batch, seq, ctx_len)
    wr, br = _router_weights(moe_w_group[1], moe_b_group[1], moe_w_expert[1], moe_b_expert[1])
    x1, h2, e_ids, pos, route, runs = _mixer_out(
        o, None, xs, None, mod[1], att_w_o[0].astype(BF16), wr, br, geo, nl)
    ya, yb = _moe(h2, e_ids, pos, runs, n_lat, moe_w_gate, moe_w_up, moe_w_down, 1)
    out = _combine(x1, ya, yb, route, mod[1], geo, nl, gain=final_norm_gain)
    return out.reshape(n_batch, seq, d)
```

```python
import functools

import jax
import jax.numpy as jnp
from jax import lax
from jax.experimental import pallas as pl
from jax.experimental.pallas import tpu as pltpu

F32 = jnp.float32
BF16 = jnp.bfloat16

D_MODEL = 1024
DEPTH = 2
GRID_W = 64
N_MOD = 6
NORM_EPS = 1e-6
HEAD_NORM_EPS = 1e-5
ROPE_THETA = 10000.0

RET_HEADS = 4
RET_DK = D_MODEL // RET_HEADS
RET_DV = 2 * RET_DK
RET_QK = RET_HEADS * RET_DK
RET_V = RET_HEADS * RET_DV
RET_IN = 2 * RET_QK + 2 * RET_V

ATT_HEAD_DIM = 128
ATT_Q_HEADS = D_MODEL // ATT_HEAD_DIM
ATT_KV_HEADS = 2
ATT_GROUP = ATT_Q_HEADS // ATT_KV_HEADS
ATT_QW = ATT_Q_HEADS * ATT_HEAD_DIM
ATT_KW = ATT_KV_HEADS * ATT_HEAD_DIM
ATT_IN = ATT_QW + 2 * ATT_KW

N_GROUPS = 4
EXPERTS_PER_GROUP = 8
N_EXPERTS = N_GROUPS * EXPERTS_PER_GROUP
EXPERT_HIDDEN = D_MODEL // 2

LANES = 128
SUBLANES = 8
ROW_TILE = 512
PROJ_GROUP = 512
RET_CHUNK = 256
RET_PAIR = 2
ATT_Q_TILE = 256
ATT_SOFTMAX_ROWS = 64
LOG2_E = 1.4426950408889634
EXPERT_BLOCK = 512
SEG_ALIGN = 8
SORT_ROWS = 2 * ROW_TILE + 256
SORT_BLOCK = 256
MOD_ROWS_PAD = 8
VMEM_LIMIT = 56 * 1024 * 1024


def _cparams(n_axes):
    return pltpu.CompilerParams(
        dimension_semantics=("arbitrary",) * n_axes, vmem_limit_bytes=VMEM_LIMIT)


def _silu(v):
    return v * (1.0 / (1.0 + jnp.exp(-v)))


def _modulate(x, shift, scale):
    ms = jnp.mean(x * x, axis=-1, keepdims=True)
    return x * lax.rsqrt(ms + NORM_EPS) * (1.0 + scale) + shift


def _split_bf16(v):
    hi = v.astype(BF16)
    lo = (v - hi.astype(F32)).astype(BF16)
    return hi, lo


def _dot(a, b):
    return jnp.dot(a, b, preferred_element_type=F32)


def _dot_nt(a, b):
    return lax.dot_general(a, b, (((1,), (1,)), ((), ())), preferred_element_type=F32)


def _row_halves(n):
    return ((0, n // 2), (n // 2, n))


def _mod_kernel(c_ref, w_ref, b_ref, o_ref):
    a_hi, a_lo = _split_bf16(_silu(c_ref[...]))
    w_hi, w_lo = _split_bf16(w_ref[0])
    acc = _dot(a_hi, w_hi) + _dot(a_lo, w_hi) + _dot(a_hi, w_lo)
    o_ref[0] = acc + b_ref[0]


def _mod_vectors(c_rows, w_mod, b_mod):
    rows = c_rows.shape[0]
    n = w_mod.shape[-1]
    tn = 1024
    out = pl.pallas_call(
        _mod_kernel,
        grid=(DEPTH, n // tn),
        in_specs=[
            pl.BlockSpec((rows, D_MODEL), lambda l, j: (0, 0)),
            pl.BlockSpec((1, D_MODEL, tn), lambda l, j: (l, 0, j)),
            pl.BlockSpec((1, 1, tn), lambda l, j: (l, 0, j)),
        ],
        out_specs=pl.BlockSpec((1, rows, tn), lambda l, j: (l, 0, j)),
        out_shape=jax.ShapeDtypeStruct((DEPTH, rows, n), F32),
        compiler_params=_cparams(2),
        name="mod_vectors",
    )(c_rows, w_mod, b_mod.reshape(DEPTH, 1, n))
    return out.reshape(DEPTH, rows, N_MOD, D_MODEL)


def _rope_angles(n, head_dim):
    rows = n // GRID_W
    r = jnp.repeat(jnp.arange(rows), GRID_W).astype(F32)
    col = jnp.tile(jnp.arange(GRID_W), rows).astype(F32)
    nf = head_dim // 4
    inv = ROPE_THETA ** (-jnp.arange(nf, dtype=F32) / nf)
    ar = r[:, None] * inv
    ac = col[:, None] * inv
    return jnp.concatenate([ar, ar, ac, ac], axis=-1)


def _ret_rope_tables(n, pad):
    ang = _rope_angles(n, RET_DK)
    lane = jnp.arange(RET_DK) % LANES
    sin = jnp.where(lane < LANES // 2, -jnp.sin(ang), jnp.sin(ang))
    cos = jnp.concatenate([jnp.cos(ang), jnp.ones((pad, RET_DK), F32)], axis=0)
    sin = jnp.concatenate([sin, jnp.zeros((pad, RET_DK), F32)], axis=0)
    return cos, sin


def _att_rope_tables(n, pad, gain, scale):
    ang = _rope_angles(n, ATT_HEAD_DIM)
    quarter = ATT_HEAD_DIM // 4
    first = jnp.arange(ATT_HEAD_DIM) % (2 * quarter) < quarter
    sin = jnp.sin(ang)
    c = jnp.cos(ang) * gain * scale
    a = jnp.where(first, -sin, 0.0) * jnp.roll(gain, ATT_HEAD_DIM - quarter) * scale
    b = jnp.where(first, 0.0, sin) * jnp.roll(gain, quarter) * scale
    ident = jnp.broadcast_to(gain * scale, (pad, ATT_HEAD_DIM))
    zeros = jnp.zeros((pad, ATT_HEAD_DIM), F32)
    return (jnp.concatenate([c, ident], axis=0), jnp.concatenate([a, zeros], axis=0),
            jnp.concatenate([b, zeros], axis=0))


def _load_rows(is_lat, lat_ref, ctx_ref, a, b):
    return jnp.where(is_lat, lat_ref[a:b, :], ctx_ref[a:b, :])


def _proj_ret_kernel(xl_ref, xc_ref, m_ref, w_ref, cos_ref, sin_ref, o_ref, h_ref, *, nl):
    is_lat = pl.program_id(0) < nl
    for a, b in _row_halves(h_ref.shape[0]):
        x = _load_rows(is_lat, xl_ref, xc_ref, a, b)
        h_ref[a:b, :] = _modulate(x, m_ref[0, 0:1, :], m_ref[0, 1:2, :]).astype(BF16)

    gw = PROJ_GROUP
    for c0 in range(0, RET_IN, gw):
        acc = _dot(h_ref[...], w_ref[:, c0:c0 + gw])
        if c0 < 2 * RET_QK:
            scale = 1.0 if c0 < RET_QK else RET_DK ** -0.5
            for g in range(gw // LANES):
                xs = acc[:, g * LANES:(g + 1) * LANES]
                t = ((c0 // LANES + g) % 2) * LANES
                r = xs * cos_ref[:, t:t + LANES] + pltpu.roll(xs, LANES // 2, 1) * sin_ref[:, t:t + LANES]
                o_ref[:, c0 + g * LANES:c0 + (g + 1) * LANES] = (r * scale).astype(BF16)
        elif c0 < 2 * RET_QK + RET_V:
            o_ref[:, c0:c0 + gw] = acc.astype(BF16)
        else:
            o_ref[:, c0:c0 + gw] = _silu(acc).astype(BF16)


def _proj_ret(x_lat, x_ctx, mod, w_in, cos, sin, geo):
    t_rows, nl, tpb, n_batch = geo
    tm = ROW_TILE
    nt = t_rows // tm
    return pl.pallas_call(
        functools.partial(_proj_ret_kernel, nl=nl),
        grid=(nt,),
        in_specs=[
            pl.BlockSpec((tm, D_MODEL), lambda i: (jnp.minimum(i, nl - 1), 0)),
            pl.BlockSpec((tm, D_MODEL), lambda i: (jnp.maximum(i - nl, 0), 0)),
            pl.BlockSpec((1, N_MOD, D_MODEL), lambda i: (jnp.where(i < nl, i // tpb, n_batch), 0, 0)),
            pl.BlockSpec((D_MODEL, RET_IN), lambda i: (0, 0)),
            pl.BlockSpec((tm, RET_DK), lambda i: (jnp.where(i < nl, i % tpb, tpb), 0)),
            pl.BlockSpec((tm, RET_DK), lambda i: (jnp.where(i < nl, i % tpb, tpb), 0)),
        ],
        out_specs=pl.BlockSpec((tm, RET_IN), lambda i: (i, 0)),
        out_shape=jax.ShapeDtypeStruct((t_rows, RET_IN), BF16),
        scratch_shapes=[pltpu.VMEM((tm, D_MODEL), BF16)],
        compiler_params=_cparams(1),
        name="proj_ret",
    )(x_lat, x_ctx, mod, w_in, cos, sin)


def _ret_kernel(lg_ref, ql, kl, vl, gl, qc, kc, vc, gc, ol, oc,
                sf_ref, sb_ref, y_ref, mask_ref):
    c = RET_CHUNK
    heads = [pl.program_id(0) * RET_PAIR + j for j in range(RET_PAIR)]
    lgf = [lg_ref[0, hd] for hd in heads]
    lgb = [lg_ref[1, hd] for hd in heads]

    @pl.when(pl.program_id(1) == 0)
    def _():
        ii = lax.broadcasted_iota(jnp.int32, (c, c), 0).astype(F32)
        jj = lax.broadcasted_iota(jnp.int32, (c, c), 1).astype(F32)
        d = ii - jj
        for j in range(RET_PAIR):
            mask_ref[j] = (jnp.where(d >= 0, jnp.exp(lgf[j] * jnp.maximum(d, 0.0)), 0.0)
                           + jnp.where(d <= 0, jnp.exp(lgb[j] * jnp.maximum(-d, 0.0)), 0.0))

    pos = lax.broadcasted_iota(jnp.int32, (c, 1), 0).astype(F32)
    xi_f = [jnp.exp(l * (pos + 1.0)) for l in lgf]
    zeta_f = [jnp.exp(l * (c - 1.0 - pos)) for l in lgf]
    xi_b = [jnp.exp(l * (c - pos)) for l in lgb]
    zeta_b = [jnp.exp(l * pos) for l in lgb]
    dec_f = [jnp.exp(jnp.full((1, RET_DV), l * c, F32)) for l in lgf]
    dec_b = [jnp.exp(jnp.full((1, RET_DV), l * c, F32)) for l in lgb]

    def state_update(s_ref, j, k, v, zeta, dec):
        kz = (k.astype(F32) * zeta).T.astype(BF16)
        s_ref[j] = s_ref[j] * dec + _dot(kz, v)

    def chunk(ref, r0, j, width):
        return ref[pl.ds(r0, c), j * width:(j + 1) * width]

    def run(q_ref, k_ref, v_ref, g_ref, o_ref, n):
        def fwd(t, carry):
            r0 = pl.multiple_of(t * c, c)
            for j in range(RET_PAIR):
                q, k, v = chunk(q_ref, r0, j, RET_DK), chunk(k_ref, r0, j, RET_DK), chunk(v_ref, r0, j, RET_DV)
                a = (_dot_nt(q, k) * mask_ref[j]).astype(BF16)
                cross = _dot(q, sf_ref[j].astype(BF16)) * xi_f[j]
                y_ref[pl.ds(r0, c), j * RET_DV:(j + 1) * RET_DV] = _dot(a, v) + cross
                state_update(sf_ref, j, k, v, zeta_f[j], dec_f[j])
            return carry

        lax.fori_loop(0, n, fwd, 0)

        def bwd(t, carry):
            r0 = pl.multiple_of((n - 1 - t) * c, c)
            for j in range(RET_PAIR):
                q, k, v = chunk(q_ref, r0, j, RET_DK), chunk(k_ref, r0, j, RET_DK), chunk(v_ref, r0, j, RET_DV)
                y = chunk(y_ref, r0, j, RET_DV) + _dot(q, sb_ref[j].astype(BF16)) * xi_b[j]
                mu = jnp.mean(y, axis=-1, keepdims=True)
                yc = y - mu
                var = jnp.mean(yc * yc, axis=-1, keepdims=True)
                yn = yc * lax.rsqrt(var + HEAD_NORM_EPS)
                o_ref[pl.ds(r0, c), j * RET_DV:(j + 1) * RET_DV] = (
                    yn * chunk(g_ref, r0, j, RET_DV).astype(F32)).astype(BF16)
                state_update(sb_ref, j, k, v, zeta_b[j], dec_b[j])
            return carry

        lax.fori_loop(0, n, bwd, 0)

    sf_ref[...] = jnp.zeros_like(sf_ref)
    sb_ref[...] = jnp.zeros_like(sb_ref)
    run(qc, kc, vc, gc, oc, qc.shape[0] // c)
    run(ql, kl, vl, gl, ol, ql.shape[0] // c)


def _retention(p, lg, n_batch, seq, ctx_len):
    cb = n_batch * seq // ctx_len
    kw, vw = RET_PAIR * RET_DK, RET_PAIR * RET_DV
    qo, ko = 0, RET_QK // kw
    vo, go = 2 * RET_QK // vw, (2 * RET_QK + RET_V) // vw
    lat = lambda w, off: pl.BlockSpec((seq, w), lambda h, b, lg_: (b, off + h))
    ctx = lambda w, off: pl.BlockSpec((ctx_len, w), lambda h, b, lg_: (cb + b, off + h))
    return pl.pallas_call(
        _ret_kernel,
        grid_spec=pltpu.PrefetchScalarGridSpec(
            num_scalar_prefetch=1,
            grid=(RET_HEADS // RET_PAIR, n_batch),
            in_specs=[lat(kw, qo), lat(kw, ko), lat(vw, vo), lat(vw, go),
                      ctx(kw, qo), ctx(kw, ko), ctx(vw, vo), ctx(vw, go)],
            out_specs=[pl.BlockSpec((seq, vw), lambda h, b, lg_: (b, h)),
                       pl.BlockSpec((ctx_len, vw), lambda h, b, lg_: (b, h))],
            scratch_shapes=[pltpu.VMEM((RET_PAIR, RET_DK, RET_DV), F32), pltpu.VMEM((RET_PAIR, RET_DK, RET_DV), F32),
                            pltpu.VMEM((seq, vw), F32), pltpu.VMEM((RET_PAIR, RET_CHUNK, RET_CHUNK), F32)],
        ),
        out_shape=[jax.ShapeDtypeStruct((n_batch * seq, RET_V), BF16),
                   jax.ShapeDtypeStruct((n_batch * ctx_len, RET_V), BF16)],
        compiler_params=_cparams(2),
        name="retention",
    )(lg, p, p, p, p, p, p, p, p)


def _route(h2, wr_ref, br_ref):
    hi, lo = _split_bf16(h2)
    l1 = _dot(hi, wr_ref[...])
    logits = l1[:, :LANES] + l1[:, LANES:] + _dot(lo, wr_ref[:, :LANES]) + br_ref[...]
    lane = lax.broadcasted_iota(jnp.int32, logits.shape, 1).astype(F32)
    neg = -jnp.inf
    big = float(LANES)
    is_g = lane < N_GROUPS
    gl = jnp.where(is_g, logits, neg)
    gm = jnp.max(gl, axis=-1, keepdims=True)
    gidx = jnp.min(jnp.where(gl == gm, lane, big), axis=-1, keepdims=True)
    gsum = jnp.sum(jnp.where(is_g, jnp.exp(jnp.where(is_g, logits, gm) - gm), 0.0), axis=-1, keepdims=True)
    g_p = 1.0 / gsum
    lo_l = N_GROUPS + EXPERTS_PER_GROUP * gidx
    in_grp = (lane >= lo_l) & (lane < lo_l + EXPERTS_PER_GROUP)
    el = jnp.where(in_grp, logits, neg)
    e1 = jnp.max(el, axis=-1, keepdims=True)
    i1 = jnp.min(jnp.where(el == e1, lane, big), axis=-1, keepdims=True)
    el2 = jnp.where(lane == i1, neg, el)
    e2 = jnp.max(el2, axis=-1, keepdims=True)
    i2 = jnp.min(jnp.where(el2 == e2, lane, big), axis=-1, keepdims=True)
    t = jnp.exp(e2 - e1)
    return lane, i1, i2, g_p / (1.0 + t), g_p * t / (1.0 + t)


def _place_in_tile(lane, i1, i2, tri_ref):
    onehot = jnp.where((lane == i1) | (lane == i2), 1.0, 0.0)
    before = _dot(tri_ref[...], onehot.astype(BF16))
    counts = jnp.sum(onehot, axis=0, keepdims=True)
    runs = jnp.floor((counts + (SEG_ALIGN - 1)) * (1.0 / SEG_ALIGN)) * SEG_ALIGN
    li = lax.broadcasted_iota(jnp.int32, (LANES, LANES), 0)
    lj = lax.broadcasted_iota(jnp.int32, (LANES, LANES), 1)
    upper = jnp.where(li < lj, 1.0, 0.0).astype(BF16)
    starts = _dot(jnp.broadcast_to(runs, (SUBLANES, LANES)).astype(BF16), upper)[0:1, :]
    p1 = jnp.sum(jnp.where(lane == i1, before + starts, 0.0), axis=-1, keepdims=True)
    p2 = jnp.sum(jnp.where(lane == i2, before + starts, 0.0), axis=-1, keepdims=True)
    return p1, p2, runs


def _unpack_bf16_pairs(u):
    lo = lax.bitcast_convert_type(lax.shift_left(u, 16), F32)
    hi = lax.bitcast_convert_type(u & jnp.int32(-65536), F32)
    return jnp.concatenate([lo, hi], axis=1).astype(BF16)


def _out_kernel(*refs, nl, has_ctx):
    if has_ctx:
        (a_lat, a_ctx, x_lat, x_ctx, m_ref, w_ref, wr_ref, br_ref, tri_ref,
         x1_ref, h2_ref, e_ref, k_ref, g_ref, c_ref) = refs
        is_lat = pl.program_id(0) < nl
    else:
        a_lat, x_lat, m_ref, w_ref, wr_ref, br_ref, tri_ref, x1_ref, h2_ref, e_ref, k_ref, g_ref, c_ref = refs
    picks = []
    for a, b in _row_halves(x1_ref.shape[0]):
        lhs = _load_rows(is_lat, a_lat, a_ctx, a, b) if has_ctx else a_lat[a:b, :]
        x = _load_rows(is_lat, x_lat, x_ctx, a, b) if has_ctx else x_lat[a:b, :]
        x1 = x + m_ref[0, 2:3, :] * _dot(lhs, w_ref[...])
        x1_ref[a:b, :] = x1
        h2 = _modulate(x1, m_ref[0, 3:4, :], m_ref[0, 4:5, :])
        h2_ref[a:b, :] = h2.astype(BF16)
        picks.append(_route(h2, wr_ref, br_ref))
    lane, i1, i2, ga, gb = (jnp.concatenate(v, axis=0) for v in zip(*picks))
    p1, p2, runs = _place_in_tile(lane, i1, i2, tri_ref)
    e_ref[...] = jnp.concatenate([i1, i2], axis=1).astype(jnp.int32) - N_GROUPS
    k_ref[...] = jnp.concatenate([p1, p2], axis=1).astype(jnp.int32)
    g_ref[...] = jnp.where(lane == 2, ga, jnp.where(lane == 3, gb, 0.0))
    c_ref[0] = jnp.broadcast_to(runs, c_ref.shape[1:])


def _mixer_out(a_lat, a_ctx, x_lat, x_ctx, mod, w_out, wr, br, geo, nt):
    t_rows, nl, tpb, n_batch = geo
    tm = ROW_TILE
    kdim = w_out.shape[0]
    has_ctx = a_ctx is not None
    mrow = lambda i: (jnp.where(i < nl, i // tpb, n_batch), 0, 0)
    lat = lambda w: pl.BlockSpec((tm, w), lambda i: (jnp.minimum(i, nl - 1), 0))
    ctx = lambda w: pl.BlockSpec((tm, w), lambda i: (jnp.maximum(i - nl, 0), 0))
    if has_ctx:
        in_specs = [lat(kdim), ctx(kdim), lat(D_MODEL), ctx(D_MODEL)]
        args = [a_lat, a_ctx, x_lat, x_ctx]
    else:
        in_specs = [lat(kdim), lat(D_MODEL)]
        args = [a_lat, x_lat]
    in_specs += [
        pl.BlockSpec((1, N_MOD, D_MODEL), mrow),
        pl.BlockSpec((kdim, D_MODEL), lambda i: (0, 0)),
        pl.BlockSpec((D_MODEL, 2 * LANES), lambda i: (0, 0)),
        pl.BlockSpec((1, LANES), lambda i: (0, 0)),
        pl.BlockSpec((tm, tm), lambda i: (0, 0)),
    ]
    tri = (jnp.arange(tm)[:, None] > jnp.arange(tm)[None, :]).astype(BF16)
    args += [mod, w_out, wr, br, tri]
    rows = nt * tm
    row_out = lambda w: pl.BlockSpec((tm, w), lambda i: (i, 0))
    return pl.pallas_call(
        functools.partial(_out_kernel, nl=nl, has_ctx=has_ctx),
        grid=(nt,),
        in_specs=in_specs,
        out_specs=[row_out(D_MODEL), row_out(D_MODEL), row_out(2), row_out(2), row_out(LANES),
                   pl.BlockSpec((1, SUBLANES, LANES), lambda i: (i, 0, 0))],
        out_shape=[jax.ShapeDtypeStruct((rows, D_MODEL), F32),
                   jax.ShapeDtypeStruct((rows, D_MODEL), BF16),
                   jax.ShapeDtypeStruct((rows, 2), jnp.int32),
                   jax.ShapeDtypeStruct((rows, 2), jnp.int32),
                   jax.ShapeDtypeStruct((rows, LANES), F32),
                   jax.ShapeDtypeStruct((nt, SUBLANES, LANES), F32)],
        compiler_params=_cparams(1),
        name="mixer_out_ctx" if has_ctx else "mixer_out",
    )(*args)


def _router_weights(w_group, b_group, w_expert, b_expert):
    w = jnp.concatenate([w_group, w_expert], axis=1)
    w = jnp.pad(w, ((0, 0), (0, LANES - w.shape[1])))
    hi = w.astype(BF16)
    lo = (w - hi.astype(F32)).astype(BF16)
    b = jnp.pad(jnp.concatenate([b_group, b_expert]), (0, LANES - N_GROUPS - N_EXPERTS))
    return jnp.concatenate([hi, lo], axis=1), b.reshape(1, LANES).astype(F32)


def _slot_tables(runs, n_tok):
    blk = EXPERT_BLOCK
    nt = runs.shape[0]
    run = runs[:, 0, N_GROUPS:N_GROUPS + N_EXPERTS].astype(jnp.int32)
    total = jnp.sum(run, axis=0)
    padded = (total + blk - 1) // blk * blk
    ends_p = jnp.cumsum(padded)
    base = (ends_p - padded)[None, :] + jnp.cumsum(run, axis=0) - run
    ends_t = jnp.cumsum(run, axis=1)
    shift = base - (ends_t - run)
    shift_rows = jnp.pad(shift.astype(F32), ((0, 0), (N_GROUPS, LANES - N_GROUPS - N_EXPERTS)))[:, None, :]
    chunk_row = jnp.arange(SORT_ROWS // SEG_ALIGN, dtype=jnp.int32) * SEG_ALIGN
    starts_t = ends_t - run
    owns = (starts_t[:, None, :] <= chunk_row[None, :, None]) & (chunk_row[None, :, None] < ends_t[:, None, :])
    chunk_shift = jnp.sum(jnp.where(owns, shift[:, None, :], 0), axis=-1)[:, None, :]
    n_chunks = ends_t[:, -1] // SEG_ALIGN
    n_blocks = -(-(2 * n_tok + nt * N_EXPERTS * (SEG_ALIGN - 1)) // blk) + N_EXPERTS
    first_slot = jnp.arange(n_blocks, dtype=jnp.int32) * blk
    block_e = jnp.minimum(jnp.sum(ends_p[None, :] <= first_slot[:, None], axis=1), N_EXPERTS - 1).astype(jnp.int32)
    n_valid = (ends_p[-1] // blk).astype(jnp.int32).reshape(1)
    return n_chunks, chunk_shift, shift_rows, block_e, n_valid, n_blocks


def _dispatch_kernel(nchunk_ref, cshift_ref, h_ref, e_ref, pos_ref, srow_ref, init_ref,
                     o_ref, dest_ref, perm_ref, sorted_ref, sem):
    del init_ref
    tm = h_ref.shape[0]
    i = pl.program_id(0)
    pos = pos_ref[...].astype(F32)
    lane = lax.broadcasted_iota(jnp.int32, (tm, LANES), 1).astype(F32)

    e_lane = e_ref[...].astype(F32) + N_GROUPS
    dest = [jnp.sum(jnp.where(lane == e_lane[:, k:k + 1], srow_ref[0], 0.0), axis=-1, keepdims=True)
            + pos[:, k:k + 1] for k in range(2)]
    dest_ref[...] = jnp.concatenate(dest, axis=1).astype(jnp.int32)

    pos_row = [jnp.transpose(jnp.where(lane == 0.0, pos[:, k:k + 1], 0.0))[0:1, :] for k in range(2)]
    for s0 in range(0, SORT_ROWS, SORT_BLOCK):
        s = (lax.broadcasted_iota(jnp.int32, (SORT_BLOCK, tm), 0) + s0).astype(F32)
        perm_ref[s0:s0 + SORT_BLOCK, :] = jnp.where((s == pos_row[0]) | (s == pos_row[1]), 1.0, 0.0).astype(BF16)
    half = D_MODEL // 2
    lo = lax.bitcast_convert_type(_dot(perm_ref[...], h_ref[:, :half]), jnp.int32)
    hi = lax.bitcast_convert_type(_dot(perm_ref[...], h_ref[:, half:]), jnp.int32)
    sorted_ref[...] = lax.shift_right_logical(lo, 16) | (hi & jnp.int32(-65536))

    def chunk_copy(c):
        r = pl.multiple_of(c * SEG_ALIGN, SEG_ALIGN)
        d = pl.multiple_of(r + cshift_ref[0, 0, c], SEG_ALIGN)
        return pltpu.make_async_copy(sorted_ref.at[pl.ds(r, SEG_ALIGN), :], o_ref.at[pl.ds(d, SEG_ALIGN), :], sem)

    def start(c, carry):
        chunk_copy(c).start()
        return carry

    def wait(c, carry):
        chunk_copy(c).wait()
        return carry

    lax.fori_loop(0, nchunk_ref[i], start, 0)
    lax.fori_loop(0, nchunk_ref[i], wait, 0)


def _dispatch(h2, e_ids, pos, n_chunks, chunk_shift, shift_rows, n_slots):
    n_tok = h2.shape[0]
    tm = ROW_TILE
    nt = n_tok // tm
    width = D_MODEL // 2
    return pl.pallas_call(
        _dispatch_kernel,
        grid_spec=pltpu.PrefetchScalarGridSpec(
            num_scalar_prefetch=1,
            grid=(nt,),
            in_specs=[pl.BlockSpec((1, 1, SORT_ROWS // SEG_ALIGN), lambda i, n: (i, 0, 0), memory_space=pltpu.SMEM),
                      pl.BlockSpec((tm, D_MODEL), lambda i, n: (i, 0)),
                      pl.BlockSpec((tm, 2), lambda i, n: (i, 0)),
                      pl.BlockSpec((tm, 2), lambda i, n: (i, 0)),
                      pl.BlockSpec((1, 1, LANES), lambda i, n: (i, 0, 0)),
                      pl.BlockSpec(memory_space=pl.ANY)],
            out_specs=[pl.BlockSpec(memory_space=pl.ANY),
                       pl.BlockSpec((tm, 2), lambda i, n: (i, 0))],
            scratch_shapes=[pltpu.VMEM((SORT_ROWS, tm), BF16), pltpu.VMEM((SORT_ROWS, width), jnp.int32),
                            pltpu.SemaphoreType.DMA(())],
        ),
        out_shape=[jax.ShapeDtypeStruct((n_slots, width), jnp.int32),
                   jax.ShapeDtypeStruct((n_tok, 2), jnp.int32)],
        input_output_aliases={6: 0},
        compiler_params=_cparams(1),
        name="dispatch",
    )(n_chunks, chunk_shift, h2, e_ids, pos, shift_rows, jnp.zeros((n_slots, width), jnp.int32))


def _expert_kernel(be_ref, nv_ref, x_ref, wg_ref, wu_ref, wd_ref, o_ref, wg_b, wu_b, wd_b):
    i = pl.program_id(0)
    valid = i < nv_ref[0]
    new_expert = (i == 0) | (be_ref[i] != be_ref[jnp.maximum(i - 1, 0)])

    @pl.when(valid & new_expert)
    def _():
        wg_b[...] = wg_ref[0, 0].astype(BF16)
        wu_b[...] = wu_ref[0, 0].astype(BF16)
        wd_b[...] = wd_ref[0, 0].astype(BF16)

    @pl.when(valid)
    def _():
        x = _unpack_bf16_pairs(x_ref[...])
        mid = (_silu(_dot(x, wg_b[...])) * _dot(x, wu_b[...])).astype(BF16)
        for a, b in _row_halves(o_ref.shape[1]):
            o_ref[:, a:b] = _dot(mid, wd_b[:, a:b]).astype(BF16)

    @pl.when(jnp.logical_not(valid))
    def _():
        o_ref[...] = jnp.zeros_like(o_ref)


def _expert_ffn(xs_sorted, block_e, n_valid, w_gate, w_up, w_down, layer):
    n_slots = xs_sorted.shape[0]
    blk = EXPERT_BLOCK
    wspec = lambda r, c: pl.BlockSpec((1, 1, r, c), lambda i, be, nv: (layer, be[i], 0, 0))
    return pl.pallas_call(
        _expert_kernel,
        grid_spec=pltpu.PrefetchScalarGridSpec(
            num_scalar_prefetch=2,
            grid=(n_slots // blk,),
            in_specs=[
                pl.BlockSpec((blk, D_MODEL // 2), lambda i, be, nv: (i, 0)),
                wspec(D_MODEL, EXPERT_HIDDEN), wspec(D_MODEL, EXPERT_HIDDEN), wspec(EXPERT_HIDDEN, D_MODEL),
            ],
            out_specs=pl.BlockSpec((blk, D_MODEL), lambda i, be, nv: (i, 0)),
            scratch_shapes=[pltpu.VMEM((D_MODEL, EXPERT_HIDDEN), BF16),
                            pltpu.VMEM((D_MODEL, EXPERT_HIDDEN), BF16),
                            pltpu.VMEM((EXPERT_HIDDEN, D_MODEL), BF16)],
        ),
        out_shape=jax.ShapeDtypeStruct((n_slots, D_MODEL), BF16),
        compiler_params=_cparams(1),
        name="expert_ffn",
    )(block_e, n_valid, xs_sorted, w_gate, w_up, w_down)


def _moe(h2, e_ids, pos, runs, n_tok, w_gate, w_up, w_down, layer):
    n_chunks, chunk_shift, shift_rows, block_e, n_valid, n_blocks = _slot_tables(runs, n_tok)
    xs_sorted, dest = _dispatch(h2, e_ids, pos, n_chunks, chunk_shift, shift_rows, n_blocks * EXPERT_BLOCK)
    ys = _expert_ffn(xs_sorted, block_e, n_valid, w_gate, w_up, w_down, layer)
    return ys[dest[:, 0]], ys[dest[:, 1]]


def _final_kernel(x1_ref, ya_ref, yb_ref, r_ref, m_ref, gain_ref, o_ref):
    r = r_ref[...]
    f = r[:, 2:3] * ya_ref[...].astype(F32) + r[:, 3:4] * yb_ref[...].astype(F32)
    x2 = x1_ref[...] + m_ref[0, 5:6, :] * f
    ms = jnp.mean(x2 * x2, axis=-1, keepdims=True)
    o_ref[...] = x2 * lax.rsqrt(ms + NORM_EPS) * gain_ref[...]


def _final(x1, ya, yb, route, mod, gain, geo, nt):
    t_rows, nl, tpb, n_batch = geo
    tm = ROW_TILE
    rows = lambda w: pl.BlockSpec((tm, w), lambda i: (i, 0))
    return pl.pallas_call(
        _final_kernel,
        grid=(nt,),
        in_specs=[rows(D_MODEL), rows(D_MODEL), rows(D_MODEL), rows(LANES),
                  pl.BlockSpec((1, N_MOD, D_MODEL), lambda i: (jnp.where(i < nl, i // tpb, n_batch), 0, 0)),
                  pl.BlockSpec((1, D_MODEL), lambda i: (0, 0))],
        out_specs=rows(D_MODEL),
        out_shape=jax.ShapeDtypeStruct((nt * tm, D_MODEL), F32),
        compiler_params=_cparams(1),
        name="combine_final",
    )(x1, ya, yb, route, mod, gain.reshape(1, D_MODEL))


def _proj_att_kernel(x1_ref, ya_ref, yb_ref, r_ref, m0_ref, m_ref, w_ref,
                     qc_ref, qa_ref, qb_ref, kc_ref, ka_ref, kb_ref, o_ref, x2_ref, h_ref):
    for a, b in _row_halves(h_ref.shape[0]):
        r = r_ref[a:b, :]
        f = r[:, 2:3] * ya_ref[a:b, :].astype(F32) + r[:, 3:4] * yb_ref[a:b, :].astype(F32)
        x2 = x1_ref[a:b, :] + m0_ref[0, 5:6, :] * f
        x2_ref[a:b, :] = x2
        h_ref[a:b, :] = _modulate(x2, m_ref[0, 0:1, :], m_ref[0, 1:2, :]).astype(BF16)

    ones = jnp.ones((LANES, LANES), BF16)

    def normed_rope(xs, c_ref, a_ref, b_ref):
        rs = lax.rsqrt(_dot((xs * xs).astype(BF16), ones) * (1.0 / ATT_HEAD_DIM) + NORM_EPS)
        r = (xs * c_ref[...] + pltpu.roll(xs, LANES - ATT_HEAD_DIM // 4, 1) * a_ref[...]
             + pltpu.roll(xs, ATT_HEAD_DIM // 4, 1) * b_ref[...])
        return (r * rs).astype(BF16)

    gw = PROJ_GROUP
    for c0 in range(0, ATT_IN, gw):
        acc = _dot(h_ref[...], w_ref[:, c0:c0 + gw])
        for g in range(gw // LANES):
            col = c0 + g * LANES
            xs = acc[:, g * LANES:(g + 1) * LANES]
            if col < ATT_QW:
                o_ref[:, col:col + LANES] = normed_rope(xs, qc_ref, qa_ref, qb_ref)
            elif col < ATT_QW + ATT_KW:
                o_ref[:, col:col + LANES] = normed_rope(xs, kc_ref, ka_ref, kb_ref)
            else:
                o_ref[:, col:col + LANES] = xs.astype(BF16)


def _proj_att(x1, ya, yb, route, mod_prev, mod, w_qkv, q_tabs, k_tabs, geo):
    t_rows, nl, tpb, n_batch = geo
    tm = ROW_TILE
    nt = t_rows // tm
    rope = pl.BlockSpec((tm, ATT_HEAD_DIM), lambda i: (jnp.where(i < nl, i % tpb, tpb), 0))
    rows = lambda w: pl.BlockSpec((tm, w), lambda i: (i, 0))
    mrow = pl.BlockSpec((1, N_MOD, D_MODEL), lambda i: (jnp.where(i < nl, i // tpb, n_batch), 0, 0))
    return pl.pallas_call(
        _proj_att_kernel,
        grid=(nt,),
        in_specs=[
            rows(D_MODEL), rows(D_MODEL), rows(D_MODEL), rows(LANES), mrow, mrow,
            pl.BlockSpec((D_MODEL, ATT_IN), lambda i: (0, 0)),
            rope, rope, rope, rope, rope, rope,
        ],
        out_specs=[rows(ATT_IN), rows(D_MODEL)],
        out_shape=[jax.ShapeDtypeStruct((t_rows, ATT_IN), BF16),
                   jax.ShapeDtypeStruct((t_rows, D_MODEL), F32)],
        scratch_shapes=[pltpu.VMEM((tm, D_MODEL), BF16)],
        compiler_params=_cparams(1),
        name="proj_att",
    )(x1, ya, yb, route, mod_prev, mod, w_qkv, *q_tabs, *k_tabs)


def _att_kernel(q_ref, kl_ref, vl_ref, kc_ref, vc_ref, o_ref, k_all, v_ext, s_ref, p_ref):
    tq = q_ref.shape[0]
    seq, ctx_len = kl_ref.shape[0], kc_ref.shape[0]

    @pl.when(pl.program_id(2) == 0)
    def _():
        k_all[0:seq, :] = kl_ref[...]
        k_all[seq:seq + ctx_len, :] = kc_ref[...]
        v_ext[0:seq, 0:LANES] = vl_ref[...]
        v_ext[seq:seq + ctx_len, 0:LANES] = vc_ref[...]
        v_ext[:, LANES:] = jnp.ones((seq + ctx_len, LANES), BF16)

    rows = ATT_SOFTMAX_ROWS

    def scores(g):
        s_ref[g * tq:(g + 1) * tq, :] = _dot_nt(q_ref[:, g * LANES:(g + 1) * LANES], k_all[...])

    def softmax(g):
        for r0 in range(g * tq, (g + 1) * tq, rows):
            m = jnp.max(s_ref[r0:r0 + rows, :], axis=-1, keepdims=True)
            p_ref[r0:r0 + rows, :] = jnp.exp2(s_ref[r0:r0 + rows, :] - m).astype(BF16)

    def weighted_values(g):
        oe = _dot(p_ref[g * tq:(g + 1) * tq, :], v_ext[...])
        o_ref[:, g * LANES:(g + 1) * LANES] = (oe[:, :LANES] / oe[:, LANES:]).astype(BF16)

    scores(0)
    scores(1)
    for g in range(ATT_GROUP):
        if g + 2 < ATT_GROUP:
            scores(g + 2)
        softmax(g)
        weighted_values(g)


def _attention(p, n_batch, seq, ctx_len):
    tq = ATT_Q_TILE
    nq = seq // tq
    cb = n_batch * seq // ctx_len
    ko, vo = ATT_QW // LANES, (ATT_QW + ATT_KW) // LANES
    gw = ATT_GROUP * ATT_HEAD_DIM
    n_keys = seq + ctx_len
    return pl.pallas_call(
        _att_kernel,
        grid=(n_batch, ATT_KV_HEADS, nq),
        in_specs=[
            pl.BlockSpec((tq, gw), lambda b, h, t: (b * nq + t, h)),
            pl.BlockSpec((seq, LANES), lambda b, h, t: (b, ko + h)),
            pl.BlockSpec((seq, LANES), lambda b, h, t: (b, vo + h)),
            pl.BlockSpec((ctx_len, LANES), lambda b, h, t: (cb + b, ko + h)),
            pl.BlockSpec((ctx_len, LANES), lambda b, h, t: (cb + b, vo + h)),
        ],
        out_specs=pl.BlockSpec((tq, gw), lambda b, h, t: (b * nq + t, h)),
        out_shape=jax.ShapeDtypeStruct((n_batch * seq, ATT_QW), BF16),
        scratch_shapes=[pltpu.VMEM((n_keys, LANES), BF16), pltpu.VMEM((n_keys, 2 * LANES), BF16),
                        pltpu.VMEM((ATT_GROUP * tq, n_keys), F32), pltpu.VMEM((ATT_GROUP * tq, n_keys), BF16)],
        compiler_params=_cparams(3),
        name="attention",
    )(p, p, p, p, p)


def kernel(x, c, ctx, c_ctx, w_mod, b_mod, ret_w_in, ret_w_out, ret_log_decay_fwd, ret_log_decay_bwd, att_w_qkv, att_w_o, att_q_gain, att_k_gain, moe_w_group, moe_b_group, moe_w_expert, moe_b_expert, moe_w_gate, moe_w_up, moe_w_down, final_norm_gain):
    n_batch, seq, d = x.shape
    ctx_len = ctx.shape[1]
    tm = ROW_TILE
    assert d == D_MODEL and w_mod.shape[0] == DEPTH == 2
    assert seq % tm == 0 and (n_batch * ctx_len) % tm == 0 and seq % GRID_W == 0
    assert seq % RET_CHUNK == 0 and ctx_len % RET_CHUNK == 0 and (n_batch * seq) % ctx_len == 0
    n_lat = n_batch * seq
    t_rows = n_lat + n_batch * ctx_len
    nl = n_lat // tm
    geo = (t_rows, nl, seq // tm, n_batch)
    nt = t_rows // tm

    x_lat, x_ctx = x.reshape(n_lat, d), ctx.reshape(n_batch * ctx_len, d)
    pad_rows = -(n_batch + 1) % MOD_ROWS_PAD
    c_rows = jnp.concatenate([c, c_ctx[None, :], jnp.zeros((pad_rows, d), F32)], axis=0)
    mod = _mod_vectors(c_rows, w_mod, b_mod)

    cos, sin = _ret_rope_tables(seq, tm)
    p = _proj_ret(x_lat, x_ctx, mod[0], ret_w_in[0].astype(BF16), cos, sin, geo)
    lg = jnp.stack([ret_log_decay_fwd[0], ret_log_decay_bwd[0]]).astype(F32)
    y_lat, y_ctx = _retention(p, lg, n_batch, seq, ctx_len)
    wr, br = _router_weights(moe_w_group[0], moe_b_group[0], moe_w_expert[0], moe_b_expert[0])
    x1, h2, e_ids, pos, route, runs = _mixer_out(
        y_lat, y_ctx, x_lat, x_ctx, mod[0], ret_w_out[0].astype(BF16), wr, br, geo, nt)
    ya, yb = _moe(h2, e_ids, pos, runs, t_rows, moe_w_gate, moe_w_up, moe_w_down, 0)

    q_tabs = _att_rope_tables(seq, tm, att_q_gain[0], ATT_HEAD_DIM ** -0.5 * LOG2_E)
    k_tabs = _att_rope_tables(seq, tm, att_k_gain[0], 1.0)
    p, xs = _proj_att(x1, ya, yb, route, mod[0], mod[1], att_w_qkv[0].astype(BF16), q_tabs, k_tabs, geo)
    o = _attention(p, n_batch, seq, ctx_len)
    wr, br = _router_weights(moe_w_group[1], moe_b_group[1], moe_w_expert[1], moe_b_expert[1])
    x1, h2, e_ids, pos, route, runs = _mixer_out(
        o, None, xs, None, mod[1], att_w_o[0].astype(BF16), wr, br, geo, nl)
    ya, yb = _moe(h2, e_ids, pos, runs, n_lat, moe_w_gate, moe_w_up, moe_w_down, 1)
    out = _final(x1, ya, yb, route, mod[1], final_norm_gain, geo, nl)
    return out.reshape(n_batch, seq, d)
```

```python
import functools

import jax
import jax.numpy as jnp
from jax import lax
from jax.experimental import pallas as pl
from jax.experimental.pallas import tpu as pltpu

F32 = jnp.float32
BF16 = jnp.bfloat16

D_MODEL = 1024
DEPTH = 2
GRID_W = 64
N_MOD = 6
NORM_EPS = 1e-6
HEAD_NORM_EPS = 1e-5
ROPE_THETA = 10000.0

RET_HEADS = 4
RET_DK = D_MODEL // RET_HEADS
RET_DV = 2 * RET_DK
RET_QK = RET_HEADS * RET_DK
RET_V = RET_HEADS * RET_DV
RET_IN = 2 * RET_QK + 2 * RET_V

ATT_HEAD_DIM = 128
ATT_Q_HEADS = D_MODEL // ATT_HEAD_DIM
ATT_KV_HEADS = 2
ATT_GROUP = ATT_Q_HEADS // ATT_KV_HEADS
ATT_QW = ATT_Q_HEADS * ATT_HEAD_DIM
ATT_KW = ATT_KV_HEADS * ATT_HEAD_DIM
ATT_IN = ATT_QW + 2 * ATT_KW

N_GROUPS = 4
EXPERTS_PER_GROUP = 8
N_EXPERTS = N_GROUPS * EXPERTS_PER_GROUP
EXPERT_HIDDEN = D_MODEL // 2

LANES = 128
SUBLANES = 8
ROW_TILE = 512
PROJ_GROUP = 512
RET_CHUNK = 256
RET_PAIR = 2
ATT_Q_TILE = 512
ATT_SOFTMAX_ROWS = 64
LOG2_E = 1.4426950408889634
EXPERT_BLOCK = 512
SEG_ALIGN = 8
SORT_ROWS = 2 * ROW_TILE + 256
SORT_BLOCK = 256
MOD_ROWS_PAD = 8
VMEM_LIMIT = 56 * 1024 * 1024


def _cparams(n_axes):
    return pltpu.CompilerParams(
        dimension_semantics=("arbitrary",) * n_axes, vmem_limit_bytes=VMEM_LIMIT)


def _silu(v):
    return v * (1.0 / (1.0 + jnp.exp(-v)))


def _modulate(x, shift, scale):
    ms = jnp.mean(x * x, axis=-1, keepdims=True)
    return x * lax.rsqrt(ms + NORM_EPS) * (1.0 + scale) + shift


def _split_bf16(v):
    hi = v.astype(BF16)
    lo = (v - hi.astype(F32)).astype(BF16)
    return hi, lo


def _dot(a, b):
    return jnp.dot(a, b, preferred_element_type=F32)


def _dot_nt(a, b):
    return lax.dot_general(a, b, (((1,), (1,)), ((), ())), preferred_element_type=F32)


def _row_halves(n):
    return ((0, n // 2), (n // 2, n))


def _mod_kernel(c_ref, w_ref, b_ref, o_ref):
    a_hi, a_lo = _split_bf16(_silu(c_ref[...]))
    w_hi, w_lo = _split_bf16(w_ref[0])
    acc = _dot(a_hi, w_hi) + _dot(a_lo, w_hi) + _dot(a_hi, w_lo)
    o_ref[0] = acc + b_ref[0]


def _mod_vectors(c_rows, w_mod, b_mod):
    rows = c_rows.shape[0]
    n = w_mod.shape[-1]
    tn = 1024
    out = pl.pallas_call(
        _mod_kernel,
        grid=(DEPTH, n // tn),
        in_specs=[
            pl.BlockSpec((rows, D_MODEL), lambda l, j: (0, 0)),
            pl.BlockSpec((1, D_MODEL, tn), lambda l, j: (l, 0, j)),
            pl.BlockSpec((1, 1, tn), lambda l, j: (l, 0, j)),
        ],
        out_specs=pl.BlockSpec((1, rows, tn), lambda l, j: (l, 0, j)),
        out_shape=jax.ShapeDtypeStruct((DEPTH, rows, n), F32),
        compiler_params=_cparams(2),
        name="mod_vectors",
    )(c_rows, w_mod, b_mod.reshape(DEPTH, 1, n))
    return out.reshape(DEPTH, rows, N_MOD, D_MODEL)


def _rope_angles(n, head_dim):
    rows = n // GRID_W
    r = jnp.repeat(jnp.arange(rows), GRID_W).astype(F32)
    col = jnp.tile(jnp.arange(GRID_W), rows).astype(F32)
    nf = head_dim // 4
    inv = ROPE_THETA ** (-jnp.arange(nf, dtype=F32) / nf)
    ar = r[:, None] * inv
    ac = col[:, None] * inv
    return jnp.concatenate([ar, ar, ac, ac], axis=-1)


def _ret_rope_tables(n, pad):
    ang = _rope_angles(n, RET_DK)
    lane = jnp.arange(RET_DK) % LANES
    sin = jnp.where(lane < LANES // 2, -jnp.sin(ang), jnp.sin(ang))
    cos = jnp.concatenate([jnp.cos(ang), jnp.ones((pad, RET_DK), F32)], axis=0)
    sin = jnp.concatenate([sin, jnp.zeros((pad, RET_DK), F32)], axis=0)
    return cos, sin


def _att_rope_tables(n, pad, gain, scale):
    ang = _rope_angles(n, ATT_HEAD_DIM)
    quarter = ATT_HEAD_DIM // 4
    first = jnp.arange(ATT_HEAD_DIM) % (2 * quarter) < quarter
    sin = jnp.sin(ang)
    c = jnp.cos(ang) * gain * scale
    a = jnp.where(first, -sin, 0.0) * jnp.roll(gain, ATT_HEAD_DIM - quarter) * scale
    b = jnp.where(first, 0.0, sin) * jnp.roll(gain, quarter) * scale
    ident = jnp.broadcast_to(gain * scale, (pad, ATT_HEAD_DIM))
    zeros = jnp.zeros((pad, ATT_HEAD_DIM), F32)
    return (jnp.concatenate([c, ident], axis=0), jnp.concatenate([a, zeros], axis=0),
            jnp.concatenate([b, zeros], axis=0))


def _load_rows(is_lat, lat_ref, ctx_ref, a, b):
    return jnp.where(is_lat, lat_ref[a:b, :], ctx_ref[a:b, :])


def _proj_ret_kernel(xl_ref, xc_ref, m_ref, w_ref, cos_ref, sin_ref, o_ref, h_ref, *, nl):
    is_lat = pl.program_id(0) < nl
    for a, b in _row_halves(h_ref.shape[0]):
        x = _load_rows(is_lat, xl_ref, xc_ref, a, b)
        h_ref[a:b, :] = _modulate(x, m_ref[0, 0:1, :], m_ref[0, 1:2, :]).astype(BF16)

    gw = PROJ_GROUP
    for c0 in range(0, RET_IN, gw):
        acc = _dot(h_ref[...], w_ref[:, c0:c0 + gw])
        if c0 < 2 * RET_QK:
            scale = 1.0 if c0 < RET_QK else RET_DK ** -0.5
            for g in range(gw // LANES):
                xs = acc[:, g * LANES:(g + 1) * LANES]
                t = ((c0 // LANES + g) % 2) * LANES
                r = xs * cos_ref[:, t:t + LANES] + pltpu.roll(xs, LANES // 2, 1) * sin_ref[:, t:t + LANES]
                o_ref[:, c0 + g * LANES:c0 + (g + 1) * LANES] = (r * scale).astype(BF16)
        elif c0 < 2 * RET_QK + RET_V:
            o_ref[:, c0:c0 + gw] = acc.astype(BF16)
        else:
            o_ref[:, c0:c0 + gw] = _silu(acc).astype(BF16)


def _proj_ret(x_lat, x_ctx, mod, w_in, cos, sin, geo):
    t_rows, nl, tpb, n_batch = geo
    tm = ROW_TILE
    nt = t_rows // tm
    return pl.pallas_call(
        functools.partial(_proj_ret_kernel, nl=nl),
        grid=(nt,),
        in_specs=[
            pl.BlockSpec((tm, D_MODEL), lambda i: (jnp.minimum(i, nl - 1), 0)),
            pl.BlockSpec((tm, D_MODEL), lambda i: (jnp.maximum(i - nl, 0), 0)),
            pl.BlockSpec((1, N_MOD, D_MODEL), lambda i: (jnp.where(i < nl, i // tpb, n_batch), 0, 0)),
            pl.BlockSpec((D_MODEL, RET_IN), lambda i: (0, 0)),
            pl.BlockSpec((tm, RET_DK), lambda i: (jnp.where(i < nl, i % tpb, tpb), 0)),
            pl.BlockSpec((tm, RET_DK), lambda i: (jnp.where(i < nl, i % tpb, tpb), 0)),
        ],
        out_specs=pl.BlockSpec((tm, RET_IN), lambda i: (i, 0)),
        out_shape=jax.ShapeDtypeStruct((t_rows, RET_IN), BF16),
        scratch_shapes=[pltpu.VMEM((tm, D_MODEL), BF16)],
        compiler_params=_cparams(1),
        name="proj_ret",
    )(x_lat, x_ctx, mod, w_in, cos, sin)


def _ret_kernel(lg_ref, ql, kl, vl, gl, qc, kc, vc, gc, ol, oc,
                sf_ref, sb_ref, y_ref, mask_ref):
    c = RET_CHUNK
    heads = [pl.program_id(0) * RET_PAIR + j for j in range(RET_PAIR)]
    lgf = [lg_ref[0, hd] for hd in heads]
    lgb = [lg_ref[1, hd] for hd in heads]

    @pl.when(pl.program_id(1) == 0)
    def _():
        ii = lax.broadcasted_iota(jnp.int32, (c, c), 0).astype(F32)
        jj = lax.broadcasted_iota(jnp.int32, (c, c), 1).astype(F32)
        d = ii - jj
        for j in range(RET_PAIR):
            mask_ref[j] = (jnp.where(d >= 0, jnp.exp(lgf[j] * jnp.maximum(d, 0.0)), 0.0)
                           + jnp.where(d <= 0, jnp.exp(lgb[j] * jnp.maximum(-d, 0.0)), 0.0))

    pos = lax.broadcasted_iota(jnp.int32, (c, 1), 0).astype(F32)
    xi_f = [jnp.exp(l * (pos + 1.0)) for l in lgf]
    zeta_f = [jnp.exp(l * (c - 1.0 - pos)) for l in lgf]
    xi_b = [jnp.exp(l * (c - pos)) for l in lgb]
    zeta_b = [jnp.exp(l * pos) for l in lgb]
    dec_f = [jnp.exp(jnp.full((1, RET_DV), l * c, F32)) for l in lgf]
    dec_b = [jnp.exp(jnp.full((1, RET_DV), l * c, F32)) for l in lgb]

    def state_update(s_ref, j, k, v, zeta, dec):
        kz = (k.astype(F32) * zeta).T.astype(BF16)
        s_ref[j] = s_ref[j] * dec + _dot(kz, v)

    def chunk(ref, r0, j, width):
        return ref[pl.ds(r0, c), j * width:(j + 1) * width]

    def run(q_ref, k_ref, v_ref, g_ref, o_ref, n):
        def fwd(t, carry):
            r0 = pl.multiple_of(t * c, c)
            for j in range(RET_PAIR):
                q, k, v = chunk(q_ref, r0, j, RET_DK), chunk(k_ref, r0, j, RET_DK), chunk(v_ref, r0, j, RET_DV)
                a = (_dot_nt(q, k) * mask_ref[j]).astype(BF16)
                cross = _dot(q, sf_ref[j].astype(BF16)) * xi_f[j]
                y_ref[pl.ds(r0, c), j * RET_DV:(j + 1) * RET_DV] = _dot(a, v) + cross
                state_update(sf_ref, j, k, v, zeta_f[j], dec_f[j])
            return carry

        lax.fori_loop(0, n, fwd, 0)

        def bwd(t, carry):
            r0 = pl.multiple_of((n - 1 - t) * c, c)
            for j in range(RET_PAIR):
                q, k, v = chunk(q_ref, r0, j, RET_DK), chunk(k_ref, r0, j, RET_DK), chunk(v_ref, r0, j, RET_DV)
                y = chunk(y_ref, r0, j, RET_DV) + _dot(q, sb_ref[j].astype(BF16)) * xi_b[j]
                mu = jnp.mean(y, axis=-1, keepdims=True)
                yc = y - mu
                var = jnp.mean(yc * yc, axis=-1, keepdims=True)
                yn = yc * lax.rsqrt(var + HEAD_NORM_EPS)
                o_ref[pl.ds(r0, c), j * RET_DV:(j + 1) * RET_DV] = (
                    yn * chunk(g_ref, r0, j, RET_DV).astype(F32)).astype(BF16)
                state_update(sb_ref, j, k, v, zeta_b[j], dec_b[j])
            return carry

        lax.fori_loop(0, n, bwd, 0)

    sf_ref[...] = jnp.zeros_like(sf_ref)
    sb_ref[...] = jnp.zeros_like(sb_ref)
    run(qc, kc, vc, gc, oc, qc.shape[0] // c)
    run(ql, kl, vl, gl, ol, ql.shape[0] // c)


def _retention(p, lg, n_batch, seq, ctx_len):
    cb = n_batch * seq // ctx_len
    kw, vw = RET_PAIR * RET_DK, RET_PAIR * RET_DV
    qo, ko = 0, RET_QK // kw
    vo, go = 2 * RET_QK // vw, (2 * RET_QK + RET_V) // vw
    lat = lambda w, off: pl.BlockSpec((seq, w), lambda h, b, lg_: (b, off + h))
    ctx = lambda w, off: pl.BlockSpec((ctx_len, w), lambda h, b, lg_: (cb + b, off + h))
    return pl.pallas_call(
        _ret_kernel,
        grid_spec=pltpu.PrefetchScalarGridSpec(
            num_scalar_prefetch=1,
            grid=(RET_HEADS // RET_PAIR, n_batch),
            in_specs=[lat(kw, qo), lat(kw, ko), lat(vw, vo), lat(vw, go),
                      ctx(kw, qo), ctx(kw, ko), ctx(vw, vo), ctx(vw, go)],
            out_specs=[pl.BlockSpec((seq, vw), lambda h, b, lg_: (b, h)),
                       pl.BlockSpec((ctx_len, vw), lambda h, b, lg_: (b, h))],
            scratch_shapes=[pltpu.VMEM((RET_PAIR, RET_DK, RET_DV), F32), pltpu.VMEM((RET_PAIR, RET_DK, RET_DV), F32),
                            pltpu.VMEM((seq, vw), F32), pltpu.VMEM((RET_PAIR, RET_CHUNK, RET_CHUNK), F32)],
        ),
        out_shape=[jax.ShapeDtypeStruct((n_batch * seq, RET_V), BF16),
                   jax.ShapeDtypeStruct((n_batch * ctx_len, RET_V), BF16)],
        compiler_params=_cparams(2),
        name="retention",
    )(lg, p, p, p, p, p, p, p, p)


def _route(h2, wr_ref, br_ref):
    hi, lo = _split_bf16(h2)
    l1 = _dot(hi, wr_ref[...])
    logits = l1[:, :LANES] + l1[:, LANES:] + _dot(lo, wr_ref[:, :LANES]) + br_ref[...]
    lane = lax.broadcasted_iota(jnp.int32, logits.shape, 1).astype(F32)
    neg = -jnp.inf
    big = float(LANES)
    is_g = lane < N_GROUPS
    gl = jnp.where(is_g, logits, neg)
    gm = jnp.max(gl, axis=-1, keepdims=True)
    gidx = jnp.min(jnp.where(gl == gm, lane, big), axis=-1, keepdims=True)
    gsum = jnp.sum(jnp.where(is_g, jnp.exp(jnp.where(is_g, logits, gm) - gm), 0.0), axis=-1, keepdims=True)
    g_p = 1.0 / gsum
    lo_l = N_GROUPS + EXPERTS_PER_GROUP * gidx
    in_grp = (lane >= lo_l) & (lane < lo_l + EXPERTS_PER_GROUP)
    el = jnp.where(in_grp, logits, neg)
    e1 = jnp.max(el, axis=-1, keepdims=True)
    i1 = jnp.min(jnp.where(el == e1, lane, big), axis=-1, keepdims=True)
    el2 = jnp.where(lane == i1, neg, el)
    e2 = jnp.max(el2, axis=-1, keepdims=True)
    i2 = jnp.min(jnp.where(el2 == e2, lane, big), axis=-1, keepdims=True)
    t = jnp.exp(e2 - e1)
    return lane, i1, i2, g_p / (1.0 + t), g_p * t / (1.0 + t)


def _place_in_tile(lane, i1, i2, tri_ref):
    onehot = jnp.where((lane == i1) | (lane == i2), 1.0, 0.0)
    before = _dot(tri_ref[...], onehot.astype(BF16))
    counts = jnp.sum(onehot, axis=0, keepdims=True)
    runs = jnp.floor((counts + (SEG_ALIGN - 1)) * (1.0 / SEG_ALIGN)) * SEG_ALIGN
    li = lax.broadcasted_iota(jnp.int32, (LANES, LANES), 0)
    lj = lax.broadcasted_iota(jnp.int32, (LANES, LANES), 1)
    upper = jnp.where(li < lj, 1.0, 0.0).astype(BF16)
    starts = _dot(jnp.broadcast_to(runs, (SUBLANES, LANES)).astype(BF16), upper)[0:1, :]
    p1 = jnp.sum(jnp.where(lane == i1, before + starts, 0.0), axis=-1, keepdims=True)
    p2 = jnp.sum(jnp.where(lane == i2, before + starts, 0.0), axis=-1, keepdims=True)
    return p1, p2, runs


def _unpack_bf16_pairs(u):
    lo = lax.bitcast_convert_type(lax.shift_left(u, 16), F32)
    hi = lax.bitcast_convert_type(u & jnp.int32(-65536), F32)
    return jnp.concatenate([lo, hi], axis=1).astype(BF16)


def _out_kernel(*refs, nl, has_ctx):
    if has_ctx:
        (a_lat, a_ctx, x_lat, x_ctx, m_ref, w_ref, wr_ref, br_ref, tri_ref,
         x1_ref, h2_ref, e_ref, k_ref, g_ref, c_ref) = refs
        is_lat = pl.program_id(0) < nl
    else:
        a_lat, x_lat, m_ref, w_ref, wr_ref, br_ref, tri_ref, x1_ref, h2_ref, e_ref, k_ref, g_ref, c_ref = refs
    picks = []
    for a, b in _row_halves(x1_ref.shape[0]):
        lhs = _load_rows(is_lat, a_lat, a_ctx, a, b) if has_ctx else a_lat[a:b, :]
        x = _load_rows(is_lat, x_lat, x_ctx, a, b) if has_ctx else x_lat[a:b, :]
        x1 = x + m_ref[0, 2:3, :] * _dot(lhs, w_ref[...])
        x1_ref[a:b, :] = x1
        h2 = _modulate(x1, m_ref[0, 3:4, :], m_ref[0, 4:5, :])
        h2_ref[a:b, :] = h2.astype(BF16)
        picks.append(_route(h2, wr_ref, br_ref))
    lane, i1, i2, ga, gb = (jnp.concatenate(v, axis=0) for v in zip(*picks))
    p1, p2, runs = _place_in_tile(lane, i1, i2, tri_ref)
    e_ref[...] = jnp.concatenate([i1, i2], axis=1).astype(jnp.int32) - N_GROUPS
    k_ref[...] = jnp.concatenate([p1, p2], axis=1).astype(jnp.int32)
    g_ref[...] = jnp.where(lane == 2, ga, jnp.where(lane == 3, gb, 0.0))
    c_ref[0] = jnp.broadcast_to(runs, c_ref.shape[1:])


def _mixer_out(a_lat, a_ctx, x_lat, x_ctx, mod, w_out, wr, br, geo, nt):
    t_rows, nl, tpb, n_batch = geo
    tm = ROW_TILE
    kdim = w_out.shape[0]
    has_ctx = a_ctx is not None
    mrow = lambda i: (jnp.where(i < nl, i // tpb, n_batch), 0, 0)
    lat = lambda w: pl.BlockSpec((tm, w), lambda i: (jnp.minimum(i, nl - 1), 0))
    ctx = lambda w: pl.BlockSpec((tm, w), lambda i: (jnp.maximum(i - nl, 0), 0))
    if has_ctx:
        in_specs = [lat(kdim), ctx(kdim), lat(D_MODEL), ctx(D_MODEL)]
        args = [a_lat, a_ctx, x_lat, x_ctx]
    else:
        in_specs = [lat(kdim), lat(D_MODEL)]
        args = [a_lat, x_lat]
    in_specs += [
        pl.BlockSpec((1, N_MOD, D_MODEL), mrow),
        pl.BlockSpec((kdim, D_MODEL), lambda i: (0, 0)),
        pl.BlockSpec((D_MODEL, 2 * LANES), lambda i: (0, 0)),
        pl.BlockSpec((1, LANES), lambda i: (0, 0)),
        pl.BlockSpec((tm, tm), lambda i: (0, 0)),
    ]
    tri = (jnp.arange(tm)[:, None] > jnp.arange(tm)[None, :]).astype(BF16)
    args += [mod, w_out, wr, br, tri]
    rows = nt * tm
    row_out = lambda w: pl.BlockSpec((tm, w), lambda i: (i, 0))
    return pl.pallas_call(
        functools.partial(_out_kernel, nl=nl, has_ctx=has_ctx),
        grid=(nt,),
        in_specs=in_specs,
        out_specs=[row_out(D_MODEL), row_out(D_MODEL), row_out(2), row_out(2), row_out(LANES),
                   pl.BlockSpec((1, SUBLANES, LANES), lambda i: (i, 0, 0))],
        out_shape=[jax.ShapeDtypeStruct((rows, D_MODEL), F32),
                   jax.ShapeDtypeStruct((rows, D_MODEL), BF16),
                   jax.ShapeDtypeStruct((rows, 2), jnp.int32),
                   jax.ShapeDtypeStruct((rows, 2), jnp.int32),
                   jax.ShapeDtypeStruct((rows, LANES), F32),
                   jax.ShapeDtypeStruct((nt, SUBLANES, LANES), F32)],
        compiler_params=_cparams(1),
        name="mixer_out_ctx" if has_ctx else "mixer_out",
    )(*args)


def _router_weights(w_group, b_group, w_expert, b_expert):
    w = jnp.concatenate([w_group, w_expert], axis=1)
    w = jnp.pad(w, ((0, 0), (0, LANES - w.shape[1])))
    hi = w.astype(BF16)
    lo = (w - hi.astype(F32)).astype(BF16)
    b = jnp.pad(jnp.concatenate([b_group, b_expert]), (0, LANES - N_GROUPS - N_EXPERTS))
    return jnp.concatenate([hi, lo], axis=1), b.reshape(1, LANES).astype(F32)


def _slot_tables(runs, n_tok):
    blk = EXPERT_BLOCK
    nt = runs.shape[0]
    run = runs[:, 0, N_GROUPS:N_GROUPS + N_EXPERTS].astype(jnp.int32)
    total = jnp.sum(run, axis=0)
    padded = (total + blk - 1) // blk * blk
    ends_p = jnp.cumsum(padded)
    base = (ends_p - padded)[None, :] + jnp.cumsum(run, axis=0) - run
    ends_t = jnp.cumsum(run, axis=1)
    shift = base - (ends_t - run)
    shift_rows = jnp.pad(shift.astype(F32), ((0, 0), (N_GROUPS, LANES - N_GROUPS - N_EXPERTS)))[:, None, :]
    chunk_row = jnp.arange(SORT_ROWS // SEG_ALIGN, dtype=jnp.int32) * SEG_ALIGN
    starts_t = ends_t - run
    owns = (starts_t[:, None, :] <= chunk_row[None, :, None]) & (chunk_row[None, :, None] < ends_t[:, None, :])
    chunk_shift = jnp.sum(jnp.where(owns, shift[:, None, :], 0), axis=-1)[:, None, :]
    n_chunks = ends_t[:, -1] // SEG_ALIGN
    n_blocks = -(-(2 * n_tok + nt * N_EXPERTS * (SEG_ALIGN - 1)) // blk) + N_EXPERTS
    first_slot = jnp.arange(n_blocks, dtype=jnp.int32) * blk
    block_e = jnp.minimum(jnp.sum(ends_p[None, :] <= first_slot[:, None], axis=1), N_EXPERTS - 1).astype(jnp.int32)
    n_valid = (ends_p[-1] // blk).astype(jnp.int32).reshape(1)
    pad_tab = jnp.concatenate([ends_p - padded + total, (padded - total) // SEG_ALIGN]).astype(jnp.int32)
    return n_chunks, chunk_shift, shift_rows, pad_tab, block_e, n_valid, n_blocks


def _dispatch_kernel(nchunk_ref, pad_ref, cshift_ref, h_ref, e_ref, pos_ref, srow_ref,
                     o_ref, dest_ref, perm_ref, sorted_ref, zero_ref, sem):
    tm = h_ref.shape[0]
    i = pl.program_id(0)
    pos = pos_ref[...].astype(F32)
    lane = lax.broadcasted_iota(jnp.int32, (tm, LANES), 1).astype(F32)

    e_lane = e_ref[...].astype(F32) + N_GROUPS
    dest = [jnp.sum(jnp.where(lane == e_lane[:, k:k + 1], srow_ref[0], 0.0), axis=-1, keepdims=True)
            + pos[:, k:k + 1] for k in range(2)]
    dest_ref[...] = jnp.concatenate(dest, axis=1).astype(jnp.int32)

    pos_row = [jnp.transpose(jnp.where(lane == 0.0, pos[:, k:k + 1], 0.0))[0:1, :] for k in range(2)]
    for s0 in range(0, SORT_ROWS, SORT_BLOCK):
        s = (lax.broadcasted_iota(jnp.int32, (SORT_BLOCK, tm), 0) + s0).astype(F32)
        perm_ref[s0:s0 + SORT_BLOCK, :] = jnp.where((s == pos_row[0]) | (s == pos_row[1]), 1.0, 0.0).astype(BF16)
    half = D_MODEL // 2
    lo = lax.bitcast_convert_type(_dot(perm_ref[...], h_ref[:, :half]), jnp.int32)
    hi = lax.bitcast_convert_type(_dot(perm_ref[...], h_ref[:, half:]), jnp.int32)
    sorted_ref[...] = lax.shift_right_logical(lo, 16) | (hi & jnp.int32(-65536))

    def chunk_copy(c):
        r = pl.multiple_of(c * SEG_ALIGN, SEG_ALIGN)
        d = pl.multiple_of(r + cshift_ref[0, 0, c], SEG_ALIGN)
        return pltpu.make_async_copy(sorted_ref.at[pl.ds(r, SEG_ALIGN), :], o_ref.at[pl.ds(d, SEG_ALIGN), :], sem)

    def start(c, carry):
        chunk_copy(c).start()
        return carry

    def wait(c, carry):
        chunk_copy(c).wait()
        return carry

    lax.fori_loop(0, nchunk_ref[i], start, 0)
    lax.fori_loop(0, nchunk_ref[i], wait, 0)

    @pl.when(i == pl.num_programs(0) - 1)
    def _():
        zero_ref[...] = jnp.zeros_like(zero_ref)

        def for_each_pad_chunk(fn):
            for e in range(N_EXPERTS):
                first = pad_ref[e]

                def body(c, carry):
                    d = pl.multiple_of(first + c * SEG_ALIGN, SEG_ALIGN)
                    fn(pltpu.make_async_copy(zero_ref, o_ref.at[pl.ds(d, SEG_ALIGN), :], sem))
                    return carry

                lax.fori_loop(0, pad_ref[N_EXPERTS + e], body, 0)

        for_each_pad_chunk(lambda cp: cp.start())
        for_each_pad_chunk(lambda cp: cp.wait())


def _dispatch(h2, e_ids, pos, n_chunks, chunk_shift, shift_rows, pad_tab, n_slots):
    n_tok = h2.shape[0]
    tm = ROW_TILE
    nt = n_tok // tm
    width = D_MODEL // 2
    return pl.pallas_call(
        _dispatch_kernel,
        grid_spec=pltpu.PrefetchScalarGridSpec(
            num_scalar_prefetch=2,
            grid=(nt,),
            in_specs=[pl.BlockSpec((1, 1, SORT_ROWS // SEG_ALIGN), lambda i, n, p: (i, 0, 0), memory_space=pltpu.SMEM),
                      pl.BlockSpec((tm, D_MODEL), lambda i, n, p: (i, 0)),
                      pl.BlockSpec((tm, 2), lambda i, n, p: (i, 0)),
                      pl.BlockSpec((tm, 2), lambda i, n, p: (i, 0)),
                      pl.BlockSpec((1, 1, LANES), lambda i, n, p: (i, 0, 0))],
            out_specs=[pl.BlockSpec(memory_space=pl.ANY),
                       pl.BlockSpec((tm, 2), lambda i, n, p: (i, 0))],
            scratch_shapes=[pltpu.VMEM((SORT_ROWS, tm), BF16), pltpu.VMEM((SORT_ROWS, width), jnp.int32),
                            pltpu.VMEM((SEG_ALIGN, width), jnp.int32), pltpu.SemaphoreType.DMA(())],
        ),
        out_shape=[jax.ShapeDtypeStruct((n_slots, width), jnp.int32),
                   jax.ShapeDtypeStruct((n_tok, 2), jnp.int32)],
        compiler_params=_cparams(1),
        name="dispatch",
    )(n_chunks, pad_tab, chunk_shift, h2, e_ids, pos, shift_rows)


def _expert_kernel(be_ref, nv_ref, x_ref, wg_ref, wu_ref, wd_ref, o_ref, wg_b, wu_b, wd_b):
    i = pl.program_id(0)
    valid = i < nv_ref[0]
    new_expert = (i == 0) | (be_ref[i] != be_ref[jnp.maximum(i - 1, 0)])

    @pl.when(valid & new_expert)
    def _():
        wg_b[...] = wg_ref[0, 0].astype(BF16)
        wu_b[...] = wu_ref[0, 0].astype(BF16)
        wd_b[...] = wd_ref[0, 0].astype(BF16)

    @pl.when(valid)
    def _():
        x = _unpack_bf16_pairs(x_ref[...])
        mid = (_silu(_dot(x, wg_b[...])) * _dot(x, wu_b[...])).astype(BF16)
        for a, b in _row_halves(o_ref.shape[1]):
            o_ref[:, a:b] = _dot(mid, wd_b[:, a:b]).astype(BF16)

    @pl.when(jnp.logical_not(valid))
    def _():
        o_ref[...] = jnp.zeros_like(o_ref)


def _expert_ffn(xs_sorted, block_e, n_valid, w_gate, w_up, w_down, layer):
    n_slots = xs_sorted.shape[0]
    blk = EXPERT_BLOCK
    wspec = lambda r, c: pl.BlockSpec((1, 1, r, c), lambda i, be, nv: (layer, be[i], 0, 0))
    return pl.pallas_call(
        _expert_kernel,
        grid_spec=pltpu.PrefetchScalarGridSpec(
            num_scalar_prefetch=2,
            grid=(n_slots // blk,),
            in_specs=[
                pl.BlockSpec((blk, D_MODEL // 2), lambda i, be, nv: (i, 0)),
                wspec(D_MODEL, EXPERT_HIDDEN), wspec(D_MODEL, EXPERT_HIDDEN), wspec(EXPERT_HIDDEN, D_MODEL),
            ],
            out_specs=pl.BlockSpec((blk, D_MODEL), lambda i, be, nv: (i, 0)),
            scratch_shapes=[pltpu.VMEM((D_MODEL, EXPERT_HIDDEN), BF16),
                            pltpu.VMEM((D_MODEL, EXPERT_HIDDEN), BF16),
                            pltpu.VMEM((EXPERT_HIDDEN, D_MODEL), BF16)],
        ),
        out_shape=jax.ShapeDtypeStruct((n_slots, D_MODEL), BF16),
        compiler_params=_cparams(1),
        name="expert_ffn",
    )(block_e, n_valid, xs_sorted, w_gate, w_up, w_down)


def _moe(h2, e_ids, pos, runs, n_tok, w_gate, w_up, w_down, layer):
    n_chunks, chunk_shift, shift_rows, pad_tab, block_e, n_valid, n_blocks = _slot_tables(runs, n_tok)
    xs_sorted, dest = _dispatch(h2, e_ids, pos, n_chunks, chunk_shift, shift_rows, pad_tab, n_blocks * EXPERT_BLOCK)
    ys = _expert_ffn(xs_sorted, block_e, n_valid, w_gate, w_up, w_down, layer)
    return ys[dest[:, 0]], ys[dest[:, 1]]


def _final_kernel(x1_ref, ya_ref, yb_ref, r_ref, m_ref, gain_ref, o_ref):
    r = r_ref[...]
    f = r[:, 2:3] * ya_ref[...].astype(F32) + r[:, 3:4] * yb_ref[...].astype(F32)
    x2 = x1_ref[...] + m_ref[0, 5:6, :] * f
    ms = jnp.mean(x2 * x2, axis=-1, keepdims=True)
    o_ref[...] = x2 * lax.rsqrt(ms + NORM_EPS) * gain_ref[...]


def _final(x1, ya, yb, route, mod, gain, geo, nt):
    t_rows, nl, tpb, n_batch = geo
    tm = ROW_TILE
    rows = lambda w: pl.BlockSpec((tm, w), lambda i: (i, 0))
    return pl.pallas_call(
        _final_kernel,
        grid=(nt,),
        in_specs=[rows(D_MODEL), rows(D_MODEL), rows(D_MODEL), rows(LANES),
                  pl.BlockSpec((1, N_MOD, D_MODEL), lambda i: (jnp.where(i < nl, i // tpb, n_batch), 0, 0)),
                  pl.BlockSpec((1, D_MODEL), lambda i: (0, 0))],
        out_specs=rows(D_MODEL),
        out_shape=jax.ShapeDtypeStruct((nt * tm, D_MODEL), F32),
        compiler_params=_cparams(1),
        name="combine_final",
    )(x1, ya, yb, route, mod, gain.reshape(1, D_MODEL))


def _proj_att_kernel(x1_ref, ya_ref, yb_ref, r_ref, m0_ref, m_ref, w_ref,
                     qc_ref, qa_ref, qb_ref, kc_ref, ka_ref, kb_ref, o_ref, x2_ref, h_ref):
    for a, b in _row_halves(h_ref.shape[0]):
        r = r_ref[a:b, :]
        f = r[:, 2:3] * ya_ref[a:b, :].astype(F32) + r[:, 3:4] * yb_ref[a:b, :].astype(F32)
        x2 = x1_ref[a:b, :] + m0_ref[0, 5:6, :] * f
        x2_ref[a:b, :] = x2
        h_ref[a:b, :] = _modulate(x2, m_ref[0, 0:1, :], m_ref[0, 1:2, :]).astype(BF16)

    ones = jnp.ones((LANES, LANES), BF16)

    def normed_rope(xs, c_ref, a_ref, b_ref):
        rs = lax.rsqrt(_dot((xs * xs).astype(BF16), ones) * (1.0 / ATT_HEAD_DIM) + NORM_EPS)
        r = (xs * c_ref[...] + pltpu.roll(xs, LANES - ATT_HEAD_DIM // 4, 1) * a_ref[...]
             + pltpu.roll(xs, ATT_HEAD_DIM // 4, 1) * b_ref[...])
        return (r * rs).astype(BF16)

    gw = PROJ_GROUP
    for c0 in range(0, ATT_IN, gw):
        acc = _dot(h_ref[...], w_ref[:, c0:c0 + gw])
        for g in range(gw // LANES):
            col = c0 + g * LANES
            xs = acc[:, g * LANES:(g + 1) * LANES]
            if col < ATT_QW:
                o_ref[:, col:col + LANES] = normed_rope(xs, qc_ref, qa_ref, qb_ref)
            elif col < ATT_QW + ATT_KW:
                o_ref[:, col:col + LANES] = normed_rope(xs, kc_ref, ka_ref, kb_ref)
            else:
                o_ref[:, col:col + LANES] = xs.astype(BF16)


def _proj_att(x1, ya, yb, route, mod_prev, mod, w_qkv, q_tabs, k_tabs, geo):
    t_rows, nl, tpb, n_batch = geo
    tm = ROW_TILE
    nt = t_rows // tm
    rope = pl.BlockSpec((tm, ATT_HEAD_DIM), lambda i: (jnp.where(i < nl, i % tpb, tpb), 0))
    rows = lambda w: pl.BlockSpec((tm, w), lambda i: (i, 0))
    mrow = pl.BlockSpec((1, N_MOD, D_MODEL), lambda i: (jnp.where(i < nl, i // tpb, n_batch), 0, 0))
    return pl.pallas_call(
        _proj_att_kernel,
        grid=(nt,),
        in_specs=[
            rows(D_MODEL), rows(D_MODEL), rows(D_MODEL), rows(LANES), mrow, mrow,
            pl.BlockSpec((D_MODEL, ATT_IN), lambda i: (0, 0)),
            rope, rope, rope, rope, rope, rope,
        ],
        out_specs=[rows(ATT_IN), rows(D_MODEL)],
        out_shape=[jax.ShapeDtypeStruct((t_rows, ATT_IN), BF16),
                   jax.ShapeDtypeStruct((t_rows, D_MODEL), F32)],
        scratch_shapes=[pltpu.VMEM((tm, D_MODEL), BF16)],
        compiler_params=_cparams(1),
        name="proj_att",
    )(x1, ya, yb, route, mod_prev, mod, w_qkv, *q_tabs, *k_tabs)


def _att_kernel(q_ref, kl_ref, vl_ref, kc_ref, vc_ref, o_ref, k_all, v_ext, s_ref, p_ref):
    tq = q_ref.shape[0]
    seq, ctx_len = kl_ref.shape[0], kc_ref.shape[0]

    @pl.when(pl.program_id(2) == 0)
    def _():
        k_all[0:seq, :] = kl_ref[...]
        k_all[seq:seq + ctx_len, :] = kc_ref[...]
        v_ext[0:seq, 0:LANES] = vl_ref[...]
        v_ext[seq:seq + ctx_len, 0:LANES] = vc_ref[...]
        v_ext[:, LANES:] = jnp.ones((seq + ctx_len, LANES), BF16)

    rows = ATT_SOFTMAX_ROWS

    def scores(g):
        s_ref[g * tq:(g + 1) * tq, :] = _dot_nt(q_ref[:, g * LANES:(g + 1) * LANES], k_all[...])

    def softmax(g):
        for r0 in range(g * tq, (g + 1) * tq, rows):
            m = jnp.max(s_ref[r0:r0 + rows, :], axis=-1, keepdims=True)
            p_ref[r0:r0 + rows, :] = jnp.exp2(s_ref[r0:r0 + rows, :] - m).astype(BF16)

    def weighted_values(g):
        oe = _dot(p_ref[g * tq:(g + 1) * tq, :], v_ext[...])
        o_ref[:, g * LANES:(g + 1) * LANES] = (oe[:, :LANES] / oe[:, LANES:]).astype(BF16)

    scores(0)
    scores(1)
    for g in range(ATT_GROUP):
        if g + 2 < ATT_GROUP:
            scores(g + 2)
        softmax(g)
        weighted_values(g)


def _attention(p, n_batch, seq, ctx_len):
    tq = ATT_Q_TILE
    nq = seq // tq
    cb = n_batch * seq // ctx_len
    ko, vo = ATT_QW // LANES, (ATT_QW + ATT_KW) // LANES
    gw = ATT_GROUP * ATT_HEAD_DIM
    n_keys = seq + ctx_len
    return pl.pallas_call(
        _att_kernel,
        grid=(n_batch, ATT_KV_HEADS, nq),
        in_specs=[
            pl.BlockSpec((tq, gw), lambda b, h, t: (b * nq + t, h)),
            pl.BlockSpec((seq, LANES), lambda b, h, t: (b, ko + h)),
            pl.BlockSpec((seq, LANES), lambda b, h, t: (b, vo + h)),
            pl.BlockSpec((ctx_len, LANES), lambda b, h, t: (cb + b, ko + h)),
            pl.BlockSpec((ctx_len, LANES), lambda b, h, t: (cb + b, vo + h)),
        ],
        out_specs=pl.BlockSpec((tq, gw), lambda b, h, t: (b * nq + t, h)),
        out_shape=jax.ShapeDtypeStruct((n_batch * seq, ATT_QW), BF16),
        scratch_shapes=[pltpu.VMEM((n_keys, LANES), BF16), pltpu.VMEM((n_keys, 2 * LANES), BF16),
                        pltpu.VMEM((ATT_GROUP * tq, n_keys), F32), pltpu.VMEM((ATT_GROUP * tq, n_keys), BF16)],
        compiler_params=_cparams(3),
        name="attention",
    )(p, p, p, p, p)


def kernel(x, c, ctx, c_ctx, w_mod, b_mod, ret_w_in, ret_w_out, ret_log_decay_fwd, ret_log_decay_bwd, att_w_qkv, att_w_o, att_q_gain, att_k_gain, moe_w_group, moe_b_group, moe_w_expert, moe_b_expert, moe_w_gate, moe_w_up, moe_w_down, final_norm_gain):
    n_batch, seq, d = x.shape
    ctx_len = ctx.shape[1]
    tm = ROW_TILE
    assert d == D_MODEL and w_mod.shape[0] == DEPTH == 2
    assert seq % tm == 0 and (n_batch * ctx_len) % tm == 0 and seq % GRID_W == 0
    assert seq % RET_CHUNK == 0 and ctx_len % RET_CHUNK == 0 and (n_batch * seq) % ctx_len == 0
    n_lat = n_batch * seq
    t_rows = n_lat + n_batch * ctx_len
    nl = n_lat // tm
    geo = (t_rows, nl, seq // tm, n_batch)
    nt = t_rows // tm

    x_lat, x_ctx = x.reshape(n_lat, d), ctx.reshape(n_batch * ctx_len, d)
    pad_rows = -(n_batch + 1) % MOD_ROWS_PAD
    c_rows = jnp.concatenate([c, c_ctx[None, :], jnp.zeros((pad_rows, d), F32)], axis=0)
    mod = _mod_vectors(c_rows, w_mod, b_mod)

    cos, sin = _ret_rope_tables(seq, tm)
    p = _proj_ret(x_lat, x_ctx, mod[0], ret_w_in[0].astype(BF16), cos, sin, geo)
    lg = jnp.stack([ret_log_decay_fwd[0], ret_log_decay_bwd[0]]).astype(F32)
    y_lat, y_ctx = _retention(p, lg, n_batch, seq, ctx_len)
    wr, br = _router_weights(moe_w_group[0], moe_b_group[0], moe_w_expert[0], moe_b_expert[0])
    x1, h2, e_ids, pos, route, runs = _mixer_out(
        y_lat, y_ctx, x_lat, x_ctx, mod[0], ret_w_out[0].astype(BF16), wr, br, geo, nt)
    ya, yb = _moe(h2, e_ids, pos, runs, t_rows, moe_w_gate, moe_w_up, moe_w_down, 0)

    q_tabs = _att_rope_tables(seq, tm, att_q_gain[0], ATT_HEAD_DIM ** -0.5 * LOG2_E)
    k_tabs = _att_rope_tables(seq, tm, att_k_gain[0], 1.0)
    p, xs = _proj_att(x1, ya, yb, route, mod[0], mod[1], att_w_qkv[0].astype(BF16), q_tabs, k_tabs, geo)
    o = _attention(p, n_batch, seq, ctx_len)
    wr, br = _router_weights(moe_w_group[1], moe_b_group[1], moe_w_expert[1], moe_b_expert[1])
    x1, h2, e_ids, pos, route, runs = _mixer_out(
        o, None, xs, None, mod[1], att_w_o[0].astype(BF16), wr, br, geo, nl)
    ya, yb = _moe(h2, e_ids, pos, runs, n_lat, moe_w_gate, moe_w_up, moe_w_down, 1)
    out = _final(x1, ya, yb, route, mod[1], final_norm_gain, geo, nl)
    return out.reshape(n_batch, seq, d)
```

```python
import functools

import jax
import jax.numpy as jnp
from jax import lax
from jax.experimental import pallas as pl
from jax.experimental.pallas import tpu as pltpu

F32 = jnp.float32
BF16 = jnp.bfloat16

D_MODEL = 1024
DEPTH = 2
GRID_W = 64
N_MOD = 6
NORM_EPS = 1e-6
HEAD_NORM_EPS = 1e-5
ROPE_THETA = 10000.0

RET_HEADS = 4
RET_DK = D_MODEL // RET_HEADS
RET_DV = 2 * RET_DK
RET_QK = RET_HEADS * RET_DK
RET_V = RET_HEADS * RET_DV
RET_IN = 2 * RET_QK + 2 * RET_V

ATT_HEAD_DIM = 128
ATT_Q_HEADS = D_MODEL // ATT_HEAD_DIM
ATT_KV_HEADS = 2
ATT_GROUP = ATT_Q_HEADS // ATT_KV_HEADS
ATT_QW = ATT_Q_HEADS * ATT_HEAD_DIM
ATT_KW = ATT_KV_HEADS * ATT_HEAD_DIM
ATT_IN = ATT_QW + 2 * ATT_KW

N_GROUPS = 4
EXPERTS_PER_GROUP = 8
N_EXPERTS = N_GROUPS * EXPERTS_PER_GROUP
EXPERT_HIDDEN = D_MODEL // 2

LANES = 128
SUBLANES = 8
ROW_TILE = 512
PROJ_GROUP = 512
RET_CHUNK = 256
RET_PAIR = 2
ATT_Q_TILE = 512
ATT_SOFTMAX_ROWS = 64
LOG2_E = 1.4426950408889634
EXPERT_BLOCK = 1024
SEG_ALIGN = 8
SORT_ROWS = 2 * ROW_TILE + 256
SORT_BLOCK = 256
WAIT_GROUP = 8
MOD_ROWS_PAD = 8
VMEM_LIMIT = 56 * 1024 * 1024


def _cparams(n_axes):
    return pltpu.CompilerParams(
        dimension_semantics=("arbitrary",) * n_axes, vmem_limit_bytes=VMEM_LIMIT)


def _silu(v):
    return v * (1.0 / (1.0 + jnp.exp(-v)))


def _modulate(x, shift, scale):
    ms = jnp.mean(x * x, axis=-1, keepdims=True)
    return x * lax.rsqrt(ms + NORM_EPS) * (1.0 + scale) + shift


def _split_bf16(v):
    hi = v.astype(BF16)
    lo = (v - hi.astype(F32)).astype(BF16)
    return hi, lo


def _dot(a, b):
    return jnp.dot(a, b, preferred_element_type=F32)


def _dot_nt(a, b):
    return lax.dot_general(a, b, (((1,), (1,)), ((), ())), preferred_element_type=F32)


def _row_halves(n):
    return ((0, n // 2), (n // 2, n))


def _mod_kernel(c_ref, w_ref, b_ref, o_ref):
    a_hi, a_lo = _split_bf16(_silu(c_ref[...]))
    w_hi, w_lo = _split_bf16(w_ref[0])
    acc = _dot(a_hi, w_hi) + _dot(a_lo, w_hi) + _dot(a_hi, w_lo)
    o_ref[0] = acc + b_ref[0]


def _mod_vectors(c_rows, w_mod, b_mod):
    rows = c_rows.shape[0]
    n = w_mod.shape[-1]
    tn = 1024
    out = pl.pallas_call(
        _mod_kernel,
        grid=(DEPTH, n // tn),
        in_specs=[
            pl.BlockSpec((rows, D_MODEL), lambda l, j: (0, 0)),
            pl.BlockSpec((1, D_MODEL, tn), lambda l, j: (l, 0, j)),
            pl.BlockSpec((1, 1, tn), lambda l, j: (l, 0, j)),
        ],
        out_specs=pl.BlockSpec((1, rows, tn), lambda l, j: (l, 0, j)),
        out_shape=jax.ShapeDtypeStruct((DEPTH, rows, n), F32),
        compiler_params=_cparams(2),
        name="mod_vectors",
    )(c_rows, w_mod, b_mod.reshape(DEPTH, 1, n))
    return out.reshape(DEPTH, rows, N_MOD, D_MODEL)


def _rope_angles(n, head_dim):
    rows = n // GRID_W
    r = jnp.repeat(jnp.arange(rows), GRID_W).astype(F32)
    col = jnp.tile(jnp.arange(GRID_W), rows).astype(F32)
    nf = head_dim // 4
    inv = ROPE_THETA ** (-jnp.arange(nf, dtype=F32) / nf)
    ar = r[:, None] * inv
    ac = col[:, None] * inv
    return jnp.concatenate([ar, ar, ac, ac], axis=-1)


def _ret_rope_tables(n, pad):
    ang = _rope_angles(n, RET_DK)
    lane = jnp.arange(RET_DK) % LANES
    sin = jnp.where(lane < LANES // 2, -jnp.sin(ang), jnp.sin(ang))
    cos = jnp.concatenate([jnp.cos(ang), jnp.ones((pad, RET_DK), F32)], axis=0)
    sin = jnp.concatenate([sin, jnp.zeros((pad, RET_DK), F32)], axis=0)
    return cos, sin


def _att_rope_tables(n, pad, gain, scale):
    ang = _rope_angles(n, ATT_HEAD_DIM)
    quarter = ATT_HEAD_DIM // 4
    first = jnp.arange(ATT_HEAD_DIM) % (2 * quarter) < quarter
    sin = jnp.sin(ang)
    c = jnp.cos(ang) * gain * scale
    a = jnp.where(first, -sin, 0.0) * jnp.roll(gain, ATT_HEAD_DIM - quarter) * scale
    b = jnp.where(first, 0.0, sin) * jnp.roll(gain, quarter) * scale
    ident = jnp.broadcast_to(gain * scale, (pad, ATT_HEAD_DIM))
    zeros = jnp.zeros((pad, ATT_HEAD_DIM), F32)
    return (jnp.concatenate([c, ident], axis=0), jnp.concatenate([a, zeros], axis=0),
            jnp.concatenate([b, zeros], axis=0))


def _load_rows(is_lat, lat_ref, ctx_ref, a, b):
    return jnp.where(is_lat, lat_ref[a:b, :], ctx_ref[a:b, :])


def _proj_ret_kernel(xl_ref, xc_ref, m_ref, w_ref, cos_ref, sin_ref, o_ref, h_ref, *, nl):
    is_lat = pl.program_id(0) < nl
    for a, b in _row_halves(h_ref.shape[0]):
        x = _load_rows(is_lat, xl_ref, xc_ref, a, b)
        h_ref[a:b, :] = _modulate(x, m_ref[0, 0:1, :], m_ref[0, 1:2, :]).astype(BF16)

    gw = PROJ_GROUP
    for c0 in range(0, RET_IN, gw):
        acc = _dot(h_ref[...], w_ref[:, c0:c0 + gw])
        if c0 < 2 * RET_QK:
            scale = 1.0 if c0 < RET_QK else RET_DK ** -0.5
            for g in range(gw // LANES):
                xs = acc[:, g * LANES:(g + 1) * LANES]
                t = ((c0 // LANES + g) % 2) * LANES
                r = xs * cos_ref[:, t:t + LANES] + pltpu.roll(xs, LANES // 2, 1) * sin_ref[:, t:t + LANES]
                o_ref[:, c0 + g * LANES:c0 + (g + 1) * LANES] = (r * scale).astype(BF16)
        elif c0 < 2 * RET_QK + RET_V:
            o_ref[:, c0:c0 + gw] = acc.astype(BF16)
        else:
            o_ref[:, c0:c0 + gw] = _silu(acc).astype(BF16)


def _proj_ret(x_lat, x_ctx, mod, w_in, cos, sin, geo):
    t_rows, nl, tpb, n_batch = geo
    tm = ROW_TILE
    nt = t_rows // tm
    return pl.pallas_call(
        functools.partial(_proj_ret_kernel, nl=nl),
        grid=(nt,),
        in_specs=[
            pl.BlockSpec((tm, D_MODEL), lambda i: (jnp.minimum(i, nl - 1), 0)),
            pl.BlockSpec((tm, D_MODEL), lambda i: (jnp.maximum(i - nl, 0), 0)),
            pl.BlockSpec((1, N_MOD, D_MODEL), lambda i: (jnp.where(i < nl, i // tpb, n_batch), 0, 0)),
            pl.BlockSpec((D_MODEL, RET_IN), lambda i: (0, 0)),
            pl.BlockSpec((tm, RET_DK), lambda i: (jnp.where(i < nl, i % tpb, tpb), 0)),
            pl.BlockSpec((tm, RET_DK), lambda i: (jnp.where(i < nl, i % tpb, tpb), 0)),
        ],
        out_specs=pl.BlockSpec((tm, RET_IN), lambda i: (i, 0)),
        out_shape=jax.ShapeDtypeStruct((t_rows, RET_IN), BF16),
        scratch_shapes=[pltpu.VMEM((tm, D_MODEL), BF16)],
        compiler_params=_cparams(1),
        name="proj_ret",
    )(x_lat, x_ctx, mod, w_in, cos, sin)


def _ret_kernel(lg_ref, ql, kl, vl, gl, qc, kc, vc, gc, ol, oc,
                sf_ref, sb_ref, y_ref, mask_ref):
    c = RET_CHUNK
    heads = [pl.program_id(0) * RET_PAIR + j for j in range(RET_PAIR)]
    lgf = [lg_ref[0, hd] for hd in heads]
    lgb = [lg_ref[1, hd] for hd in heads]

    @pl.when(pl.program_id(1) == 0)
    def _():
        ii = lax.broadcasted_iota(jnp.int32, (c, c), 0).astype(F32)
        jj = lax.broadcasted_iota(jnp.int32, (c, c), 1).astype(F32)
        d = ii - jj
        for j in range(RET_PAIR):
            mask_ref[j] = (jnp.where(d >= 0, jnp.exp(lgf[j] * jnp.maximum(d, 0.0)), 0.0)
                           + jnp.where(d <= 0, jnp.exp(lgb[j] * jnp.maximum(-d, 0.0)), 0.0))

    pos = lax.broadcasted_iota(jnp.int32, (c, 1), 0).astype(F32)
    xi_f = [jnp.exp(l * (pos + 1.0)) for l in lgf]
    zeta_f = [jnp.exp(l * (c - 1.0 - pos)) for l in lgf]
    xi_b = [jnp.exp(l * (c - pos)) for l in lgb]
    zeta_b = [jnp.exp(l * pos) for l in lgb]
    dec_f = [jnp.exp(jnp.full((1, RET_DV), l * c, F32)) for l in lgf]
    dec_b = [jnp.exp(jnp.full((1, RET_DV), l * c, F32)) for l in lgb]

    def state_update(s_ref, j, k, v, zeta, dec):
        kz = (k.astype(F32) * zeta).T.astype(BF16)
        s_ref[j] = s_ref[j] * dec + _dot(kz, v)

    def chunk(ref, r0, j, width):
        return ref[pl.ds(r0, c), j * width:(j + 1) * width]

    def run(q_ref, k_ref, v_ref, g_ref, o_ref, n):
        def fwd(t, carry):
            r0 = pl.multiple_of(t * c, c)
            for j in range(RET_PAIR):
                q, k, v = chunk(q_ref, r0, j, RET_DK), chunk(k_ref, r0, j, RET_DK), chunk(v_ref, r0, j, RET_DV)
                a = (_dot_nt(q, k) * mask_ref[j]).astype(BF16)
                cross = _dot(q, sf_ref[j].astype(BF16)) * xi_f[j]
                y_ref[pl.ds(r0, c), j * RET_DV:(j + 1) * RET_DV] = _dot(a, v) + cross
                state_update(sf_ref, j, k, v, zeta_f[j], dec_f[j])
            return carry

        lax.fori_loop(0, n, fwd, 0)

        def bwd(t, carry):
            r0 = pl.multiple_of((n - 1 - t) * c, c)
            for j in range(RET_PAIR):
                q, k, v = chunk(q_ref, r0, j, RET_DK), chunk(k_ref, r0, j, RET_DK), chunk(v_ref, r0, j, RET_DV)
                y = chunk(y_ref, r0, j, RET_DV) + _dot(q, sb_ref[j].astype(BF16)) * xi_b[j]
                mu = jnp.mean(y, axis=-1, keepdims=True)
                yc = y - mu
                var = jnp.mean(yc * yc, axis=-1, keepdims=True)
                yn = yc * lax.rsqrt(var + HEAD_NORM_EPS)
                o_ref[pl.ds(r0, c), j * RET_DV:(j + 1) * RET_DV] = (
                    yn * chunk(g_ref, r0, j, RET_DV).astype(F32)).astype(BF16)
                state_update(sb_ref, j, k, v, zeta_b[j], dec_b[j])
            return carry

        lax.fori_loop(0, n, bwd, 0)

    sf_ref[...] = jnp.zeros_like(sf_ref)
    sb_ref[...] = jnp.zeros_like(sb_ref)
    run(qc, kc, vc, gc, oc, qc.shape[0] // c)
    run(ql, kl, vl, gl, ol, ql.shape[0] // c)


def _retention(p, lg, n_batch, seq, ctx_len):
    cb = n_batch * seq // ctx_len
    kw, vw = RET_PAIR * RET_DK, RET_PAIR * RET_DV
    qo, ko = 0, RET_QK // kw
    vo, go = 2 * RET_QK // vw, (2 * RET_QK + RET_V) // vw
    lat = lambda w, off: pl.BlockSpec((seq, w), lambda h, b, lg_: (b, off + h))
    ctx = lambda w, off: pl.BlockSpec((ctx_len, w), lambda h, b, lg_: (cb + b, off + h))
    return pl.pallas_call(
        _ret_kernel,
        grid_spec=pltpu.PrefetchScalarGridSpec(
            num_scalar_prefetch=1,
            grid=(RET_HEADS // RET_PAIR, n_batch),
            in_specs=[lat(kw, qo), lat(kw, ko), lat(vw, vo), lat(vw, go),
                      ctx(kw, qo), ctx(kw, ko), ctx(vw, vo), ctx(vw, go)],
            out_specs=[pl.BlockSpec((seq, vw), lambda h, b, lg_: (b, h)),
                       pl.BlockSpec((ctx_len, vw), lambda h, b, lg_: (b, h))],
            scratch_shapes=[pltpu.VMEM((RET_PAIR, RET_DK, RET_DV), F32), pltpu.VMEM((RET_PAIR, RET_DK, RET_DV), F32),
                            pltpu.VMEM((seq, vw), F32), pltpu.VMEM((RET_PAIR, RET_CHUNK, RET_CHUNK), F32)],
        ),
        out_shape=[jax.ShapeDtypeStruct((n_batch * seq, RET_V), BF16),
                   jax.ShapeDtypeStruct((n_batch * ctx_len, RET_V), BF16)],
        compiler_params=_cparams(2),
        name="retention",
    )(lg, p, p, p, p, p, p, p, p)


def _route(h2, wr_ref, br_ref):
    hi, lo = _split_bf16(h2)
    l1 = _dot(hi, wr_ref[...])
    logits = l1[:, :LANES] + l1[:, LANES:] + _dot(lo, wr_ref[:, :LANES]) + br_ref[...]
    lane = lax.broadcasted_iota(jnp.int32, logits.shape, 1).astype(F32)
    neg = -jnp.inf
    big = float(LANES)
    is_g = lane < N_GROUPS
    gl = jnp.where(is_g, logits, neg)
    gm = jnp.max(gl, axis=-1, keepdims=True)
    gidx = jnp.min(jnp.where(gl == gm, lane, big), axis=-1, keepdims=True)
    gsum = jnp.sum(jnp.where(is_g, jnp.exp(jnp.where(is_g, logits, gm) - gm), 0.0), axis=-1, keepdims=True)
    g_p = 1.0 / gsum
    lo_l = N_GROUPS + EXPERTS_PER_GROUP * gidx
    in_grp = (lane >= lo_l) & (lane < lo_l + EXPERTS_PER_GROUP)
    el = jnp.where(in_grp, logits, neg)
    e1 = jnp.max(el, axis=-1, keepdims=True)
    i1 = jnp.min(jnp.where(el == e1, lane, big), axis=-1, keepdims=True)
    el2 = jnp.where(lane == i1, neg, el)
    e2 = jnp.max(el2, axis=-1, keepdims=True)
    i2 = jnp.min(jnp.where(el2 == e2, lane, big), axis=-1, keepdims=True)
    t = jnp.exp(e2 - e1)
    return lane, i1, i2, g_p / (1.0 + t), g_p * t / (1.0 + t)


def _place_in_tile(lane, i1, i2, tri_ref):
    onehot = jnp.where((lane == i1) | (lane == i2), 1.0, 0.0)
    before = _dot(tri_ref[...], onehot.astype(BF16))
    counts = jnp.sum(onehot, axis=0, keepdims=True)
    runs = jnp.floor((counts + (SEG_ALIGN - 1)) * (1.0 / SEG_ALIGN)) * SEG_ALIGN
    li = lax.broadcasted_iota(jnp.int32, (LANES, LANES), 0)
    lj = lax.broadcasted_iota(jnp.int32, (LANES, LANES), 1)
    upper = jnp.where(li < lj, 1.0, 0.0).astype(BF16)
    starts = _dot(jnp.broadcast_to(runs, (SUBLANES, LANES)).astype(BF16), upper)[0:1, :]
    p1 = jnp.sum(jnp.where(lane == i1, before + starts, 0.0), axis=-1, keepdims=True)
    p2 = jnp.sum(jnp.where(lane == i2, before + starts, 0.0), axis=-1, keepdims=True)
    return p1, p2, runs


def _unpack_bf16_pairs(u):
    lo = lax.bitcast_convert_type(lax.shift_left(u, 16), F32)
    hi = lax.bitcast_convert_type(u & jnp.int32(-65536), F32)
    return jnp.concatenate([lo, hi], axis=1).astype(BF16)


def _out_kernel(*refs, nl, has_ctx):
    if has_ctx:
        (a_lat, a_ctx, x_lat, x_ctx, m_ref, w_ref, wr_ref, br_ref, tri_ref,
         x1_ref, h2_ref, e_ref, k_ref, g_ref, c_ref) = refs
        is_lat = pl.program_id(0) < nl
    else:
        a_lat, x_lat, m_ref, w_ref, wr_ref, br_ref, tri_ref, x1_ref, h2_ref, e_ref, k_ref, g_ref, c_ref = refs
    picks = []
    for a, b in _row_halves(x1_ref.shape[0]):
        lhs = _load_rows(is_lat, a_lat, a_ctx, a, b) if has_ctx else a_lat[a:b, :]
        x = _load_rows(is_lat, x_lat, x_ctx, a, b) if has_ctx else x_lat[a:b, :]
        x1 = x + m_ref[0, 2:3, :] * _dot(lhs, w_ref[...])
        x1_ref[a:b, :] = x1
        h2 = _modulate(x1, m_ref[0, 3:4, :], m_ref[0, 4:5, :])
        h2_ref[a:b, :] = h2.astype(BF16)
        picks.append(_route(h2, wr_ref, br_ref))
    lane, i1, i2, ga, gb = (jnp.concatenate(v, axis=0) for v in zip(*picks))
    p1, p2, runs = _place_in_tile(lane, i1, i2, tri_ref)
    e_ref[...] = jnp.concatenate([i1, i2], axis=1).astype(jnp.int32) - N_GROUPS
    k_ref[...] = jnp.concatenate([p1, p2], axis=1).astype(jnp.int32)
    g_ref[...] = jnp.where(lane == 2, ga, jnp.where(lane == 3, gb, 0.0))
    c_ref[0] = jnp.broadcast_to(runs, c_ref.shape[1:])


def _mixer_out(a_lat, a_ctx, x_lat, x_ctx, mod, w_out, wr, br, geo, nt):
    t_rows, nl, tpb, n_batch = geo
    tm = ROW_TILE
    kdim = w_out.shape[0]
    has_ctx = a_ctx is not None
    mrow = lambda i: (jnp.where(i < nl, i // tpb, n_batch), 0, 0)
    lat = lambda w: pl.BlockSpec((tm, w), lambda i: (jnp.minimum(i, nl - 1), 0))
    ctx = lambda w: pl.BlockSpec((tm, w), lambda i: (jnp.maximum(i - nl, 0), 0))
    if has_ctx:
        in_specs = [lat(kdim), ctx(kdim), lat(D_MODEL), ctx(D_MODEL)]
        args = [a_lat, a_ctx, x_lat, x_ctx]
    else:
        in_specs = [lat(kdim), lat(D_MODEL)]
        args = [a_lat, x_lat]
    in_specs += [
        pl.BlockSpec((1, N_MOD, D_MODEL), mrow),
        pl.BlockSpec((kdim, D_MODEL), lambda i: (0, 0)),
        pl.BlockSpec((D_MODEL, 2 * LANES), lambda i: (0, 0)),
        pl.BlockSpec((1, LANES), lambda i: (0, 0)),
        pl.BlockSpec((tm, tm), lambda i: (0, 0)),
    ]
    tri = (jnp.arange(tm)[:, None] > jnp.arange(tm)[None, :]).astype(BF16)
    args += [mod, w_out, wr, br, tri]
    rows = nt * tm
    row_out = lambda w: pl.BlockSpec((tm, w), lambda i: (i, 0))
    return pl.pallas_call(
        functools.partial(_out_kernel, nl=nl, has_ctx=has_ctx),
        grid=(nt,),
        in_specs=in_specs,
        out_specs=[row_out(D_MODEL), row_out(D_MODEL), row_out(2), row_out(2), row_out(LANES),
                   pl.BlockSpec((1, SUBLANES, LANES), lambda i: (i, 0, 0))],
        out_shape=[jax.ShapeDtypeStruct((rows, D_MODEL), F32),
                   jax.ShapeDtypeStruct((rows, D_MODEL), BF16),
                   jax.ShapeDtypeStruct((rows, 2), jnp.int32),
                   jax.ShapeDtypeStruct((rows, 2), jnp.int32),
                   jax.ShapeDtypeStruct((rows, LANES), F32),
                   jax.ShapeDtypeStruct((nt, SUBLANES, LANES), F32)],
        compiler_params=_cparams(1),
        name="mixer_out_ctx" if has_ctx else "mixer_out",
    )(*args)


def _router_weights(w_group, b_group, w_expert, b_expert):
    w = jnp.concatenate([w_group, w_expert], axis=1)
    w = jnp.pad(w, ((0, 0), (0, LANES - w.shape[1])))
    hi = w.astype(BF16)
    lo = (w - hi.astype(F32)).astype(BF16)
    b = jnp.pad(jnp.concatenate([b_group, b_expert]), (0, LANES - N_GROUPS - N_EXPERTS))
    return jnp.concatenate([hi, lo], axis=1), b.reshape(1, LANES).astype(F32)


def _slot_tables(runs, n_tok):
    blk = EXPERT_BLOCK
    nt = runs.shape[0]
    run = runs[:, 0, N_GROUPS:N_GROUPS + N_EXPERTS].astype(jnp.int32)
    total = jnp.sum(run, axis=0)
    padded = (total + blk - 1) // blk * blk
    ends_p = jnp.cumsum(padded)
    base = (ends_p - padded)[None, :] + jnp.cumsum(run, axis=0) - run
    ends_t = jnp.cumsum(run, axis=1)
    shift = base - (ends_t - run)
    shift_rows = jnp.pad(shift.astype(F32), ((0, 0), (N_GROUPS, LANES - N_GROUPS - N_EXPERTS)))[:, None, :]
    chunk_row = jnp.arange(SORT_ROWS // SEG_ALIGN, dtype=jnp.int32) * SEG_ALIGN
    starts_t = ends_t - run
    owns = (starts_t[:, None, :] <= chunk_row[None, :, None]) & (chunk_row[None, :, None] < ends_t[:, None, :])
    chunk_shift = jnp.sum(jnp.where(owns, shift[:, None, :], 0), axis=-1)[:, None, :]
    n_chunks = ends_t[:, -1] // SEG_ALIGN
    n_blocks = -(-(2 * n_tok + nt * N_EXPERTS * (SEG_ALIGN - 1)) // blk) + N_EXPERTS
    first_slot = jnp.arange(n_blocks, dtype=jnp.int32) * blk
    block_e = jnp.minimum(jnp.sum(ends_p[None, :] <= first_slot[:, None], axis=1), N_EXPERTS - 1).astype(jnp.int32)
    n_valid = (ends_p[-1] // blk).astype(jnp.int32).reshape(1)
    pad_tab = jnp.concatenate([ends_p - padded + total, (padded - total) // SEG_ALIGN]).astype(jnp.int32)
    return n_chunks, chunk_shift, shift_rows, pad_tab, block_e, n_valid, n_blocks


def _dispatch_kernel(nchunk_ref, pad_ref, cshift_ref, h_ref, e_ref, pos_ref, srow_ref,
                     o_ref, dest_ref, perm_ref, sorted_ref, zero_ref, sem):
    tm = h_ref.shape[0]
    i = pl.program_id(0)
    pos = pos_ref[...].astype(F32)
    lane = lax.broadcasted_iota(jnp.int32, (tm, LANES), 1).astype(F32)

    e_lane = e_ref[...].astype(F32) + N_GROUPS
    dest = [jnp.sum(jnp.where(lane == e_lane[:, k:k + 1], srow_ref[0], 0.0), axis=-1, keepdims=True)
            + pos[:, k:k + 1] for k in range(2)]
    dest_ref[...] = jnp.concatenate(dest, axis=1).astype(jnp.int32)

    pos_row = [jnp.transpose(jnp.where(lane == 0.0, pos[:, k:k + 1], 0.0))[0:1, :] for k in range(2)]
    for s0 in range(0, SORT_ROWS, SORT_BLOCK):
        s = (lax.broadcasted_iota(jnp.int32, (SORT_BLOCK, tm), 0) + s0).astype(F32)
        perm_ref[s0:s0 + SORT_BLOCK, :] = jnp.where((s == pos_row[0]) | (s == pos_row[1]), 1.0, 0.0).astype(BF16)
    half = D_MODEL // 2
    lo = lax.bitcast_convert_type(_dot(perm_ref[...], h_ref[:, :half]), jnp.int32)
    hi = lax.bitcast_convert_type(_dot(perm_ref[...], h_ref[:, half:]), jnp.int32)
    sorted_ref[...] = lax.shift_right_logical(lo, 16) | (hi & jnp.int32(-65536))

    def chunk_copy(c):
        r = pl.multiple_of(c * SEG_ALIGN, SEG_ALIGN)
        d = pl.multiple_of(r + cshift_ref[0, 0, c], SEG_ALIGN)
        return pltpu.make_async_copy(sorted_ref.at[pl.ds(r, SEG_ALIGN), :], o_ref.at[pl.ds(d, SEG_ALIGN), :], sem)

    def start(c, carry):
        chunk_copy(c).start()
        return carry

    n = nchunk_ref[i]
    lax.fori_loop(0, n, start, 0)

    def wait_rows(rows):
        def body(c, carry):
            pltpu.make_async_copy(sorted_ref.at[pl.ds(0, rows), :], o_ref.at[pl.ds(0, rows), :], sem).wait()
            return carry
        return body

    lax.fori_loop(0, n // WAIT_GROUP, wait_rows(WAIT_GROUP * SEG_ALIGN), 0)
    lax.fori_loop(0, n % WAIT_GROUP, wait_rows(SEG_ALIGN), 0)

    @pl.when(i == pl.num_programs(0) - 1)
    def _():
        zero_ref[...] = jnp.zeros_like(zero_ref)

        def for_each_pad_chunk(fn):
            for e in range(N_EXPERTS):
                first = pad_ref[e]

                def body(c, carry):
                    d = pl.multiple_of(first + c * SEG_ALIGN, SEG_ALIGN)
                    fn(pltpu.make_async_copy(zero_ref, o_ref.at[pl.ds(d, SEG_ALIGN), :], sem))
                    return carry

                lax.fori_loop(0, pad_ref[N_EXPERTS + e], body, 0)

        for_each_pad_chunk(lambda cp: cp.start())
        for_each_pad_chunk(lambda cp: cp.wait())


def _dispatch(h2, e_ids, pos, n_chunks, chunk_shift, shift_rows, pad_tab, n_slots):
    n_tok = h2.shape[0]
    tm = ROW_TILE
    nt = n_tok // tm
    width = D_MODEL // 2
    return pl.pallas_call(
        _dispatch_kernel,
        grid_spec=pltpu.PrefetchScalarGridSpec(
            num_scalar_prefetch=2,
            grid=(nt,),
            in_specs=[pl.BlockSpec((1, 1, SORT_ROWS // SEG_ALIGN), lambda i, n, p: (i, 0, 0), memory_space=pltpu.SMEM),
                      pl.BlockSpec((tm, D_MODEL), lambda i, n, p: (i, 0)),
                      pl.BlockSpec((tm, 2), lambda i, n, p: (i, 0)),
                      pl.BlockSpec((tm, 2), lambda i, n, p: (i, 0)),
                      pl.BlockSpec((1, 1, LANES), lambda i, n, p: (i, 0, 0))],
            out_specs=[pl.BlockSpec(memory_space=pl.ANY),
                       pl.BlockSpec((tm, 2), lambda i, n, p: (i, 0))],
            scratch_shapes=[pltpu.VMEM((SORT_ROWS, tm), BF16), pltpu.VMEM((SORT_ROWS, width), jnp.int32),
                            pltpu.VMEM((SEG_ALIGN, width), jnp.int32), pltpu.SemaphoreType.DMA(())],
        ),
        out_shape=[jax.ShapeDtypeStruct((n_slots, width), jnp.int32),
                   jax.ShapeDtypeStruct((n_tok, 2), jnp.int32)],
        compiler_params=_cparams(1),
        name="dispatch",
    )(n_chunks, pad_tab, chunk_shift, h2, e_ids, pos, shift_rows)


def _expert_kernel(be_ref, nv_ref, x_ref, wg_ref, wu_ref, wd_ref, o_ref, wg_b, wu_b, wd_b):
    i = pl.program_id(0)
    valid = i < nv_ref[0]
    new_expert = (i == 0) | (be_ref[i] != be_ref[jnp.maximum(i - 1, 0)])

    @pl.when(valid & new_expert)
    def _():
        wg_b[...] = wg_ref[0, 0].astype(BF16)
        wu_b[...] = wu_ref[0, 0].astype(BF16)
        wd_b[...] = wd_ref[0, 0].astype(BF16)

    @pl.when(valid)
    def _():
        x = _unpack_bf16_pairs(x_ref[...])
        mid = (_silu(_dot(x, wg_b[...])) * _dot(x, wu_b[...])).astype(BF16)
        for a, b in _row_halves(o_ref.shape[1]):
            o_ref[:, a:b] = _dot(mid, wd_b[:, a:b]).astype(BF16)

    @pl.when(jnp.logical_not(valid))
    def _():
        o_ref[...] = jnp.zeros_like(o_ref)


def _expert_ffn(xs_sorted, block_e, n_valid, w_gate, w_up, w_down, layer):
    n_slots = xs_sorted.shape[0]
    blk = EXPERT_BLOCK
    wspec = lambda r, c: pl.BlockSpec((1, 1, r, c), lambda i, be, nv: (layer, be[i], 0, 0))
    return pl.pallas_call(
        _expert_kernel,
        grid_spec=pltpu.PrefetchScalarGridSpec(
            num_scalar_prefetch=2,
            grid=(n_slots // blk,),
            in_specs=[
                pl.BlockSpec((blk, D_MODEL // 2), lambda i, be, nv: (i, 0)),
                wspec(D_MODEL, EXPERT_HIDDEN), wspec(D_MODEL, EXPERT_HIDDEN), wspec(EXPERT_HIDDEN, D_MODEL),
            ],
            out_specs=pl.BlockSpec((blk, D_MODEL), lambda i, be, nv: (i, 0)),
            scratch_shapes=[pltpu.VMEM((D_MODEL, EXPERT_HIDDEN), BF16),
                            pltpu.VMEM((D_MODEL, EXPERT_HIDDEN), BF16),
                            pltpu.VMEM((EXPERT_HIDDEN, D_MODEL), BF16)],
        ),
        out_shape=jax.ShapeDtypeStruct((n_slots, D_MODEL), BF16),
        compiler_params=_cparams(1),
        name="expert_ffn",
    )(block_e, n_valid, xs_sorted, w_gate, w_up, w_down)


def _moe(h2, e_ids, pos, runs, n_tok, w_gate, w_up, w_down, layer):
    n_chunks, chunk_shift, shift_rows, pad_tab, block_e, n_valid, n_blocks = _slot_tables(runs, n_tok)
    xs_sorted, dest = _dispatch(h2, e_ids, pos, n_chunks, chunk_shift, shift_rows, pad_tab, n_blocks * EXPERT_BLOCK)
    ys = _expert_ffn(xs_sorted, block_e, n_valid, w_gate, w_up, w_down, layer)
    return ys[dest[:, 0]], ys[dest[:, 1]]


def _final_kernel(x1_ref, ya_ref, yb_ref, r_ref, m_ref, gain_ref, o_ref):
    r = r_ref[...]
    f = r[:, 2:3] * ya_ref[...].astype(F32) + r[:, 3:4] * yb_ref[...].astype(F32)
    x2 = x1_ref[...] + m_ref[0, 5:6, :] * f
    ms = jnp.mean(x2 * x2, axis=-1, keepdims=True)
    o_ref[...] = x2 * lax.rsqrt(ms + NORM_EPS) * gain_ref[...]


def _final(x1, ya, yb, route, mod, gain, geo, nt):
    t_rows, nl, tpb, n_batch = geo
    tm = ROW_TILE
    rows = lambda w: pl.BlockSpec((tm, w), lambda i: (i, 0))
    return pl.pallas_call(
        _final_kernel,
        grid=(nt,),
        in_specs=[rows(D_MODEL), rows(D_MODEL), rows(D_MODEL), rows(LANES),
                  pl.BlockSpec((1, N_MOD, D_MODEL), lambda i: (jnp.where(i < nl, i // tpb, n_batch), 0, 0)),
                  pl.BlockSpec((1, D_MODEL), lambda i: (0, 0))],
        out_specs=rows(D_MODEL),
        out_shape=jax.ShapeDtypeStruct((nt * tm, D_MODEL), F32),
        compiler_params=_cparams(1),
        name="combine_final",
    )(x1, ya, yb, route, mod, gain.reshape(1, D_MODEL))


def _proj_att_kernel(x1_ref, ya_ref, yb_ref, r_ref, m0_ref, m_ref, w_ref,
                     qc_ref, qa_ref, qb_ref, kc_ref, ka_ref, kb_ref, o_ref, x2_ref, h_ref):
    for a, b in _row_halves(h_ref.shape[0]):
        r = r_ref[a:b, :]
        f = r[:, 2:3] * ya_ref[a:b, :].astype(F32) + r[:, 3:4] * yb_ref[a:b, :].astype(F32)
        x2 = x1_ref[a:b, :] + m0_ref[0, 5:6, :] * f
        x2_ref[a:b, :] = x2
        h_ref[a:b, :] = _modulate(x2, m_ref[0, 0:1, :], m_ref[0, 1:2, :]).astype(BF16)

    ones = jnp.ones((LANES, LANES), BF16)

    def normed_rope(xs, c_ref, a_ref, b_ref):
        rs = lax.rsqrt(_dot((xs * xs).astype(BF16), ones) * (1.0 / ATT_HEAD_DIM) + NORM_EPS)
        r = (xs * c_ref[...] + pltpu.roll(xs, LANES - ATT_HEAD_DIM // 4, 1) * a_ref[...]
             + pltpu.roll(xs, ATT_HEAD_DIM // 4, 1) * b_ref[...])
        return (r * rs).astype(BF16)

    gw = PROJ_GROUP
    for c0 in range(0, ATT_IN, gw):
        acc = _dot(h_ref[...], w_ref[:, c0:c0 + gw])
        for g in range(gw // LANES):
            col = c0 + g * LANES
            xs = acc[:, g * LANES:(g + 1) * LANES]
            if col < ATT_QW:
                o_ref[:, col:col + LANES] = normed_rope(xs, qc_ref, qa_ref, qb_ref)
            elif col < ATT_QW + ATT_KW:
                o_ref[:, col:col + LANES] = normed_rope(xs, kc_ref, ka_ref, kb_ref)
            else:
                o_ref[:, col:col + LANES] = xs.astype(BF16)


def _proj_att(x1, ya, yb, route, mod_prev, mod, w_qkv, q_tabs, k_tabs, geo):
    t_rows, nl, tpb, n_batch = geo
    tm = ROW_TILE
    nt = t_rows // tm
    rope = pl.BlockSpec((tm, ATT_HEAD_DIM), lambda i: (jnp.where(i < nl, i % tpb, tpb), 0))
    rows = lambda w: pl.BlockSpec((tm, w), lambda i: (i, 0))
    mrow = pl.BlockSpec((1, N_MOD, D_MODEL), lambda i: (jnp.where(i < nl, i // tpb, n_batch), 0, 0))
    return pl.pallas_call(
        _proj_att_kernel,
        grid=(nt,),
        in_specs=[
            rows(D_MODEL), rows(D_MODEL), rows(D_MODEL), rows(LANES), mrow, mrow,
            pl.BlockSpec((D_MODEL, ATT_IN), lambda i: (0, 0)),
            rope, rope, rope, rope, rope, rope,
        ],
        out_specs=[rows(ATT_IN), rows(D_MODEL)],
        out_shape=[jax.ShapeDtypeStruct((t_rows, ATT_IN), BF16),
                   jax.ShapeDtypeStruct((t_rows, D_MODEL), F32)],
        scratch_shapes=[pltpu.VMEM((tm, D_MODEL), BF16)],
        compiler_params=_cparams(1),
        name="proj_att",
    )(x1, ya, yb, route, mod_prev, mod, w_qkv, *q_tabs, *k_tabs)


def _att_kernel(q_ref, kl_ref, vl_ref, kc_ref, vc_ref, o_ref, k_all, v_ext, s_ref, p_ref):
    tq = q_ref.shape[0]
    seq, ctx_len = kl_ref.shape[0], kc_ref.shape[0]

    @pl.when(pl.program_id(2) == 0)
    def _():
        k_all[0:seq, :] = kl_ref[...]
        k_all[seq:seq + ctx_len, :] = kc_ref[...]
        v_ext[0:seq, 0:LANES] = vl_ref[...]
        v_ext[seq:seq + ctx_len, 0:LANES] = vc_ref[...]
        v_ext[:, LANES:] = jnp.ones((seq + ctx_len, LANES), BF16)

    rows = ATT_SOFTMAX_ROWS

    def scores(g):
        s_ref[g * tq:(g + 1) * tq, :] = _dot_nt(q_ref[:, g * LANES:(g + 1) * LANES], k_all[...])

    def softmax(g):
        for r0 in range(g * tq, (g + 1) * tq, rows):
            m = jnp.max(s_ref[r0:r0 + rows, :], axis=-1, keepdims=True)
            p_ref[r0:r0 + rows, :] = jnp.exp2(s_ref[r0:r0 + rows, :] - m).astype(BF16)

    def weighted_values(g):
        oe = _dot(p_ref[g * tq:(g + 1) * tq, :], v_ext[...])
        o_ref[:, g * LANES:(g + 1) * LANES] = (oe[:, :LANES] / oe[:, LANES:]).astype(BF16)

    scores(0)
    scores(1)
    for g in range(ATT_GROUP):
        if g + 2 < ATT_GROUP:
            scores(g + 2)
        softmax(g)
        weighted_values(g)


def _attention(p, n_batch, seq, ctx_len):
    tq = ATT_Q_TILE
    nq = seq // tq
    cb = n_batch * seq // ctx_len
    ko, vo = ATT_QW // LANES, (ATT_QW + ATT_KW) // LANES
    gw = ATT_GROUP * ATT_HEAD_DIM
    n_keys = seq + ctx_len
    return pl.pallas_call(
        _att_kernel,
        grid=(n_batch, ATT_KV_HEADS, nq),
        in_specs=[
            pl.BlockSpec((tq, gw), lambda b, h, t: (b * nq + t, h)),
            pl.BlockSpec((seq, LANES), lambda b, h, t: (b, ko + h)),
            pl.BlockSpec((seq, LANES), lambda b, h, t: (b, vo + h)),
            pl.BlockSpec((ctx_len, LANES), lambda b, h, t: (cb + b, ko + h)),
            pl.BlockSpec((ctx_len, LANES), lambda b, h, t: (cb + b, vo + h)),
        ],
        out_specs=pl.BlockSpec((tq, gw), lambda b, h, t: (b * nq + t, h)),
        out_shape=jax.ShapeDtypeStruct((n_batch * seq, ATT_QW), BF16),
        scratch_shapes=[pltpu.VMEM((n_keys, LANES), BF16), pltpu.VMEM((n_keys, 2 * LANES), BF16),
                        pltpu.VMEM((ATT_GROUP * tq, n_keys), F32), pltpu.VMEM((ATT_GROUP * tq, n_keys), BF16)],
        compiler_params=_cparams(3),
        name="attention",
    )(p, p, p, p, p)


def kernel(x, c, ctx, c_ctx, w_mod, b_mod, ret_w_in, ret_w_out, ret_log_decay_fwd, ret_log_decay_bwd, att_w_qkv, att_w_o, att_q_gain, att_k_gain, moe_w_group, moe_b_group, moe_w_expert, moe_b_expert, moe_w_gate, moe_w_up, moe_w_down, final_norm_gain):
    n_batch, seq, d = x.shape
    ctx_len = ctx.shape[1]
    tm = ROW_TILE
    assert d == D_MODEL and w_mod.shape[0] == DEPTH == 2
    assert seq % tm == 0 and (n_batch * ctx_len) % tm == 0 and seq % GRID_W == 0
    assert seq % RET_CHUNK == 0 and ctx_len % RET_CHUNK == 0 and (n_batch * seq) % ctx_len == 0
    n_lat = n_batch * seq
    t_rows = n_lat + n_batch * ctx_len
    nl = n_lat // tm
    geo = (t_rows, nl, seq // tm, n_batch)
    nt = t_rows // tm

    x_lat, x_ctx = x.reshape(n_lat, d), ctx.reshape(n_batch * ctx_len, d)
    pad_rows = -(n_batch + 1) % MOD_ROWS_PAD
    c_rows = jnp.concatenate([c, c_ctx[None, :], jnp.zeros((pad_rows, d), F32)], axis=0)
    mod = _mod_vectors(c_rows, w_mod, b_mod)

    cos, sin = _ret_rope_tables(seq, tm)
    p = _proj_ret(x_lat, x_ctx, mod[0], ret_w_in[0].astype(BF16), cos, sin, geo)
    lg = jnp.stack([ret_log_decay_fwd[0], ret_log_decay_bwd[0]]).astype(F32)
    y_lat, y_ctx = _retention(p, lg, n_batch, seq, ctx_len)
    wr, br = _router_weights(moe_w_group[0], moe_b_group[0], moe_w_expert[0], moe_b_expert[0])
    x1, h2, e_ids, pos, route, runs = _mixer_out(
        y_lat, y_ctx, x_lat, x_ctx, mod[0], ret_w_out[0].astype(BF16), wr, br, geo, nt)
    ya, yb = _moe(h2, e_ids, pos, runs, t_rows, moe_w_gate, moe_w_up, moe_w_down, 0)

    q_tabs = _att_rope_tables(seq, tm, att_q_gain[0], ATT_HEAD_DIM ** -0.5 * LOG2_E)
    k_tabs = _att_rope_tables(seq, tm, att_k_gain[0], 1.0)
    p, xs = _proj_att(x1, ya, yb, route, mod[0], mod[1], att_w_qkv[0].astype(BF16), q_tabs, k_tabs, geo)
    o = _attention(p, n_batch, seq, ctx_len)
    wr, br = _router_weights(moe_w_group[1], moe_b_group[1], moe_w_expert[1], moe_b_expert[1])
    x1, h2, e_ids, pos, route, runs = _mixer_out(
        o, None, xs, None, mod[1], att_w_o[0].astype(BF16), wr, br, geo, nl)
    ya, yb = _moe(h2, e_ids, pos, runs, n_lat, moe_w_gate, moe_w_up, moe_w_down, 1)
    out = _final(x1, ya, yb, route, mod[1], final_norm_gain, geo, nl)
    return out.reshape(n_batch, seq, d)
```

```python
import functools

import jax
import jax.numpy as jnp
from jax import lax
from jax.experimental import pallas as pl
from jax.experimental.pallas import tpu as pltpu

F32 = jnp.float32
BF16 = jnp.bfloat16

D_MODEL = 1024
DEPTH = 2
GRID_W = 64
N_MOD = 6
NORM_EPS = 1e-6
HEAD_NORM_EPS = 1e-5
ROPE_THETA = 10000.0

RET_HEADS = 4
RET_DK = D_MODEL // RET_HEADS
RET_DV = 2 * RET_DK
RET_QK = RET_HEADS * RET_DK
RET_V = RET_HEADS * RET_DV
RET_IN = 2 * RET_QK + 2 * RET_V

ATT_HEAD_DIM = 128
ATT_Q_HEADS = D_MODEL // ATT_HEAD_DIM
ATT_KV_HEADS = 2
ATT_GROUP = ATT_Q_HEADS // ATT_KV_HEADS
ATT_QW = ATT_Q_HEADS * ATT_HEAD_DIM
ATT_KW = ATT_KV_HEADS * ATT_HEAD_DIM
ATT_IN = ATT_QW + 2 * ATT_KW

N_GROUPS = 4
EXPERTS_PER_GROUP = 8
N_EXPERTS = N_GROUPS * EXPERTS_PER_GROUP
EXPERT_HIDDEN = D_MODEL // 2

LANES = 128
SUBLANES = 8
ROW_TILE = 512
PROJ_GROUP = 512
RET_CHUNK = 256
RET_PAIR = 2
RET_UNROLL_FWD = 4
RET_UNROLL_BWD = 2
ATT_Q_TILE = 512
ATT_SOFTMAX_ROWS = 64
LOG2_E = 1.4426950408889634
EXPERT_BLOCK = 1024
SEG_ALIGN = 8
SORT_ROWS = 2 * ROW_TILE + 256
SORT_BLOCK = 256
WAIT_GROUP = 8
MOD_ROWS_PAD = 8
VMEM_LIMIT = 56 * 1024 * 1024


def _cparams(n_axes):
    return pltpu.CompilerParams(
        dimension_semantics=("arbitrary",) * n_axes, vmem_limit_bytes=VMEM_LIMIT)


def _silu(v):
    return v * (1.0 / (1.0 + jnp.exp(-v)))


def _modulate(x, shift, scale):
    ms = jnp.mean(x * x, axis=-1, keepdims=True)
    return x * lax.rsqrt(ms + NORM_EPS) * (1.0 + scale) + shift


def _split_bf16(v):
    hi = v.astype(BF16)
    lo = (v - hi.astype(F32)).astype(BF16)
    return hi, lo


def _dot(a, b):
    return jnp.dot(a, b, preferred_element_type=F32)


def _dot_nt(a, b):
    return lax.dot_general(a, b, (((1,), (1,)), ((), ())), preferred_element_type=F32)


def _row_halves(n):
    return ((0, n // 2), (n // 2, n))


def _mod_kernel(c_ref, w_ref, b_ref, o_ref):
    a_hi, a_lo = _split_bf16(_silu(c_ref[...]))
    w_hi, w_lo = _split_bf16(w_ref[0])
    acc = _dot(a_hi, w_hi) + _dot(a_lo, w_hi) + _dot(a_hi, w_lo)
    o_ref[0] = acc + b_ref[0]


def _mod_vectors(c_rows, w_mod, b_mod):
    rows = c_rows.shape[0]
    n = w_mod.shape[-1]
    tn = 1024
    out = pl.pallas_call(
        _mod_kernel,
        grid=(DEPTH, n // tn),
        in_specs=[
            pl.BlockSpec((rows, D_MODEL), lambda l, j: (0, 0)),
            pl.BlockSpec((1, D_MODEL, tn), lambda l, j: (l, 0, j)),
            pl.BlockSpec((1, 1, tn), lambda l, j: (l, 0, j)),
        ],
        out_specs=pl.BlockSpec((1, rows, tn), lambda l, j: (l, 0, j)),
        out_shape=jax.ShapeDtypeStruct((DEPTH, rows, n), F32),
        compiler_params=_cparams(2),
        name="mod_vectors",
    )(c_rows, w_mod, b_mod.reshape(DEPTH, 1, n))
    return out.reshape(DEPTH, rows, N_MOD, D_MODEL)


def _rope_angles(n, head_dim):
    rows = n // GRID_W
    r = jnp.repeat(jnp.arange(rows), GRID_W).astype(F32)
    col = jnp.tile(jnp.arange(GRID_W), rows).astype(F32)
    nf = head_dim // 4
    inv = ROPE_THETA ** (-jnp.arange(nf, dtype=F32) / nf)
    ar = r[:, None] * inv
    ac = col[:, None] * inv
    return jnp.concatenate([ar, ar, ac, ac], axis=-1)


def _ret_rope_tables(n, pad):
    ang = _rope_angles(n, RET_DK)
    lane = jnp.arange(RET_DK) % LANES
    sin = jnp.where(lane < LANES // 2, -jnp.sin(ang), jnp.sin(ang))
    cos = jnp.concatenate([jnp.cos(ang), jnp.ones((pad, RET_DK), F32)], axis=0)
    sin = jnp.concatenate([sin, jnp.zeros((pad, RET_DK), F32)], axis=0)
    return cos, sin


def _att_rope_tables(n, pad, gain, scale):
    ang = _rope_angles(n, ATT_HEAD_DIM)
    quarter = ATT_HEAD_DIM // 4
    first = jnp.arange(ATT_HEAD_DIM) % (2 * quarter) < quarter
    sin = jnp.sin(ang)
    c = jnp.cos(ang) * gain * scale
    a = jnp.where(first, -sin, 0.0) * jnp.roll(gain, ATT_HEAD_DIM - quarter) * scale
    b = jnp.where(first, 0.0, sin) * jnp.roll(gain, quarter) * scale
    ident = jnp.broadcast_to(gain * scale, (pad, ATT_HEAD_DIM))
    zeros = jnp.zeros((pad, ATT_HEAD_DIM), F32)
    return (jnp.concatenate([c, ident], axis=0), jnp.concatenate([a, zeros], axis=0),
            jnp.concatenate([b, zeros], axis=0))


def _load_rows(is_lat, lat_ref, ctx_ref, a, b):
    return jnp.where(is_lat, lat_ref[a:b, :], ctx_ref[a:b, :])


def _proj_ret_kernel(xl_ref, xc_ref, m_ref, w_ref, cos_ref, sin_ref, o_ref, h_ref, *, nl):
    is_lat = pl.program_id(0) < nl
    for a, b in _row_halves(h_ref.shape[0]):
        x = _load_rows(is_lat, xl_ref, xc_ref, a, b)
        h_ref[a:b, :] = _modulate(x, m_ref[0, 0:1, :], m_ref[0, 1:2, :]).astype(BF16)

    gw = PROJ_GROUP
    for c0 in range(0, RET_IN, gw):
        acc = _dot(h_ref[...], w_ref[:, c0:c0 + gw])
        if c0 < 2 * RET_QK:
            scale = 1.0 if c0 < RET_QK else RET_DK ** -0.5
            for g in range(gw // LANES):
                xs = acc[:, g * LANES:(g + 1) * LANES]
                t = ((c0 // LANES + g) % 2) * LANES
                r = xs * cos_ref[:, t:t + LANES] + pltpu.roll(xs, LANES // 2, 1) * sin_ref[:, t:t + LANES]
                o_ref[:, c0 + g * LANES:c0 + (g + 1) * LANES] = (r * scale).astype(BF16)
        elif c0 < 2 * RET_QK + RET_V:
            o_ref[:, c0:c0 + gw] = acc.astype(BF16)
        else:
            o_ref[:, c0:c0 + gw] = _silu(acc).astype(BF16)


def _proj_ret(x_lat, x_ctx, mod, w_in, cos, sin, geo):
    t_rows, nl, tpb, n_batch = geo
    tm = ROW_TILE
    nt = t_rows // tm
    return pl.pallas_call(
        functools.partial(_proj_ret_kernel, nl=nl),
        grid=(nt,),
        in_specs=[
            pl.BlockSpec((tm, D_MODEL), lambda i: (jnp.minimum(i, nl - 1), 0)),
            pl.BlockSpec((tm, D_MODEL), lambda i: (jnp.maximum(i - nl, 0), 0)),
            pl.BlockSpec((1, N_MOD, D_MODEL), lambda i: (jnp.where(i < nl, i // tpb, n_batch), 0, 0)),
            pl.BlockSpec((D_MODEL, RET_IN), lambda i: (0, 0)),
            pl.BlockSpec((tm, RET_DK), lambda i: (jnp.where(i < nl, i % tpb, tpb), 0)),
            pl.BlockSpec((tm, RET_DK), lambda i: (jnp.where(i < nl, i % tpb, tpb), 0)),
        ],
        out_specs=pl.BlockSpec((tm, RET_IN), lambda i: (i, 0)),
        out_shape=jax.ShapeDtypeStruct((t_rows, RET_IN), BF16),
        scratch_shapes=[pltpu.VMEM((tm, D_MODEL), BF16)],
        compiler_params=_cparams(1),
        name="proj_ret",
    )(x_lat, x_ctx, mod, w_in, cos, sin)


def _ret_kernel(lg_ref, ql, kl, vl, gl, qc, kc, vc, gc, ol, oc,
                sf_ref, sb_ref, y_ref, mask_ref):
    c = RET_CHUNK
    heads = [pl.program_id(0) * RET_PAIR + j for j in range(RET_PAIR)]
    lgf = [lg_ref[0, hd] for hd in heads]
    lgb = [lg_ref[1, hd] for hd in heads]

    @pl.when(pl.program_id(1) == 0)
    def _():
        ii = lax.broadcasted_iota(jnp.int32, (c, c), 0).astype(F32)
        jj = lax.broadcasted_iota(jnp.int32, (c, c), 1).astype(F32)
        d = ii - jj
        for j in range(RET_PAIR):
            mask_ref[j] = (jnp.where(d >= 0, jnp.exp(lgf[j] * jnp.maximum(d, 0.0)), 0.0)
                           + jnp.where(d <= 0, jnp.exp(lgb[j] * jnp.maximum(-d, 0.0)), 0.0))

    pos = lax.broadcasted_iota(jnp.int32, (c, 1), 0).astype(F32)
    xi_f = [jnp.exp(l * (pos + 1.0)) for l in lgf]
    zeta_f = [jnp.exp(l * (c - 1.0 - pos)) for l in lgf]
    xi_b = [jnp.exp(l * (c - pos)) for l in lgb]
    zeta_b = [jnp.exp(l * pos) for l in lgb]
    dec_f = [jnp.exp(jnp.full((1, RET_DV), l * c, F32)) for l in lgf]
    dec_b = [jnp.exp(jnp.full((1, RET_DV), l * c, F32)) for l in lgb]

    def state_update(s_ref, j, k, v, zeta, dec):
        kz = (k.astype(F32) * zeta).T.astype(BF16)
        s_ref[j] = s_ref[j] * dec + _dot(kz, v)

    def chunk(ref, r0, j, width):
        return ref[pl.ds(r0, c), j * width:(j + 1) * width]

    def run(q_ref, k_ref, v_ref, g_ref, o_ref, n):
        def fwd(t, carry):
            r0 = pl.multiple_of(t * c, c)
            for j in range(RET_PAIR):
                q, k, v = chunk(q_ref, r0, j, RET_DK), chunk(k_ref, r0, j, RET_DK), chunk(v_ref, r0, j, RET_DV)
                a = (_dot_nt(q, k) * mask_ref[j]).astype(BF16)
                cross = _dot(q, sf_ref[j].astype(BF16)) * xi_f[j]
                y_ref[pl.ds(r0, c), j * RET_DV:(j + 1) * RET_DV] = _dot(a, v) + cross
                state_update(sf_ref, j, k, v, zeta_f[j], dec_f[j])
            return carry

        lax.fori_loop(0, n, fwd, 0, unroll=min(n, RET_UNROLL_FWD))

        def bwd(t, carry):
            r0 = pl.multiple_of((n - 1 - t) * c, c)
            for j in range(RET_PAIR):
                q, k, v = chunk(q_ref, r0, j, RET_DK), chunk(k_ref, r0, j, RET_DK), chunk(v_ref, r0, j, RET_DV)
                y = chunk(y_ref, r0, j, RET_DV) + _dot(q, sb_ref[j].astype(BF16)) * xi_b[j]
                mu = jnp.mean(y, axis=-1, keepdims=True)
                yc = y - mu
                var = jnp.mean(yc * yc, axis=-1, keepdims=True)
                yn = yc * lax.rsqrt(var + HEAD_NORM_EPS)
                o_ref[pl.ds(r0, c), j * RET_DV:(j + 1) * RET_DV] = (
                    yn * chunk(g_ref, r0, j, RET_DV).astype(F32)).astype(BF16)
                state_update(sb_ref, j, k, v, zeta_b[j], dec_b[j])
            return carry

        lax.fori_loop(0, n, bwd, 0, unroll=min(n, RET_UNROLL_BWD))

    sf_ref[...] = jnp.zeros_like(sf_ref)
    sb_ref[...] = jnp.zeros_like(sb_ref)
    run(qc, kc, vc, gc, oc, qc.shape[0] // c)
    run(ql, kl, vl, gl, ol, ql.shape[0] // c)


def _retention(p, lg, n_batch, seq, ctx_len):
    cb = n_batch * seq // ctx_len
    kw, vw = RET_PAIR * RET_DK, RET_PAIR * RET_DV
    qo, ko = 0, RET_QK // kw
    vo, go = 2 * RET_QK // vw, (2 * RET_QK + RET_V) // vw
    lat = lambda w, off: pl.BlockSpec((seq, w), lambda h, b, lg_: (b, off + h))
    ctx = lambda w, off: pl.BlockSpec((ctx_len, w), lambda h, b, lg_: (cb + b, off + h))
    return pl.pallas_call(
        _ret_kernel,
        grid_spec=pltpu.PrefetchScalarGridSpec(
            num_scalar_prefetch=1,
            grid=(RET_HEADS // RET_PAIR, n_batch),
            in_specs=[lat(kw, qo), lat(kw, ko), lat(vw, vo), lat(vw, go),
                      ctx(kw, qo), ctx(kw, ko), ctx(vw, vo), ctx(vw, go)],
            out_specs=[pl.BlockSpec((seq, vw), lambda h, b, lg_: (b, h)),
                       pl.BlockSpec((ctx_len, vw), lambda h, b, lg_: (b, h))],
            scratch_shapes=[pltpu.VMEM((RET_PAIR, RET_DK, RET_DV), F32), pltpu.VMEM((RET_PAIR, RET_DK, RET_DV), F32),
                            pltpu.VMEM((seq, vw), F32), pltpu.VMEM((RET_PAIR, RET_CHUNK, RET_CHUNK), F32)],
        ),
        out_shape=[jax.ShapeDtypeStruct((n_batch * seq, RET_V), BF16),
                   jax.ShapeDtypeStruct((n_batch * ctx_len, RET_V), BF16)],
        compiler_params=_cparams(2),
        name="retention",
    )(lg, p, p, p, p, p, p, p, p)


def _route(h2, wr_ref, br_ref):
    hi, lo = _split_bf16(h2)
    l1 = _dot(hi, wr_ref[...])
    logits = l1[:, :LANES] + l1[:, LANES:] + _dot(lo, wr_ref[:, :LANES]) + br_ref[...]
    lane = lax.broadcasted_iota(jnp.int32, logits.shape, 1).astype(F32)
    neg = -jnp.inf
    big = float(LANES)
    is_g = lane < N_GROUPS
    gl = jnp.where(is_g, logits, neg)
    gm = jnp.max(gl, axis=-1, keepdims=True)
    gidx = jnp.min(jnp.where(gl == gm, lane, big), axis=-1, keepdims=True)
    gsum = jnp.sum(jnp.where(is_g, jnp.exp(jnp.where(is_g, logits, gm) - gm), 0.0), axis=-1, keepdims=True)
    g_p = 1.0 / gsum
    lo_l = N_GROUPS + EXPERTS_PER_GROUP * gidx
    in_grp = (lane >= lo_l) & (lane < lo_l + EXPERTS_PER_GROUP)
    el = jnp.where(in_grp, logits, neg)
    e1 = jnp.max(el, axis=-1, keepdims=True)
    i1 = jnp.min(jnp.where(el == e1, lane, big), axis=-1, keepdims=True)
    el2 = jnp.where(lane == i1, neg, el)
    e2 = jnp.max(el2, axis=-1, keepdims=True)
    i2 = jnp.min(jnp.where(el2 == e2, lane, big), axis=-1, keepdims=True)
    t = jnp.exp(e2 - e1)
    return lane, i1, i2, g_p / (1.0 + t), g_p * t / (1.0 + t)


def _place_in_tile(lane, i1, i2, tri_ref):
    onehot = jnp.where((lane == i1) | (lane == i2), 1.0, 0.0)
    before = _dot(tri_ref[...], onehot.astype(BF16))
    counts = jnp.sum(onehot, axis=0, keepdims=True)
    runs = jnp.floor((counts + (SEG_ALIGN - 1)) * (1.0 / SEG_ALIGN)) * SEG_ALIGN
    li = lax.broadcasted_iota(jnp.int32, (LANES, LANES), 0)
    lj = lax.broadcasted_iota(jnp.int32, (LANES, LANES), 1)
    upper = jnp.where(li < lj, 1.0, 0.0).astype(BF16)
    starts = _dot(jnp.broadcast_to(runs, (SUBLANES, LANES)).astype(BF16), upper)[0:1, :]
    p1 = jnp.sum(jnp.where(lane == i1, before + starts, 0.0), axis=-1, keepdims=True)
    p2 = jnp.sum(jnp.where(lane == i2, before + starts, 0.0), axis=-1, keepdims=True)
    return p1, p2, runs


def _unpack_bf16_pairs(u):
    lo = lax.bitcast_convert_type(lax.shift_left(u, 16), F32)
    hi = lax.bitcast_convert_type(u & jnp.int32(-65536), F32)
    return jnp.concatenate([lo, hi], axis=1).astype(BF16)


def _out_kernel(*refs, nl, has_ctx):
    if has_ctx:
        (a_lat, a_ctx, x_lat, x_ctx, m_ref, w_ref, wr_ref, br_ref, tri_ref,
         x1_ref, h2_ref, e_ref, k_ref, g_ref, c_ref) = refs
        is_lat = pl.program_id(0) < nl
    else:
        a_lat, x_lat, m_ref, w_ref, wr_ref, br_ref, tri_ref, x1_ref, h2_ref, e_ref, k_ref, g_ref, c_ref = refs
    picks = []
    for a, b in _row_halves(x1_ref.shape[0]):
        lhs = _load_rows(is_lat, a_lat, a_ctx, a, b) if has_ctx else a_lat[a:b, :]
        x = _load_rows(is_lat, x_lat, x_ctx, a, b) if has_ctx else x_lat[a:b, :]
        x1 = x + m_ref[0, 2:3, :] * _dot(lhs, w_ref[...])
        x1_ref[a:b, :] = x1
        h2 = _modulate(x1, m_ref[0, 3:4, :], m_ref[0, 4:5, :])
        h2_ref[a:b, :] = h2.astype(BF16)
        picks.append(_route(h2, wr_ref, br_ref))
    lane, i1, i2, ga, gb = (jnp.concatenate(v, axis=0) for v in zip(*picks))
    p1, p2, runs = _place_in_tile(lane, i1, i2, tri_ref)
    e_ref[...] = jnp.concatenate([i1, i2], axis=1).astype(jnp.int32) - N_GROUPS
    k_ref[...] = jnp.concatenate([p1, p2], axis=1).astype(jnp.int32)
    g_ref[...] = jnp.where(lane == 2, ga, jnp.where(lane == 3, gb, 0.0))
    c_ref[0] = jnp.broadcast_to(runs, c_ref.shape[1:])


def _mixer_out(a_lat, a_ctx, x_lat, x_ctx, mod, w_out, wr, br, geo, nt):
    t_rows, nl, tpb, n_batch = geo
    tm = ROW_TILE
    kdim = w_out.shape[0]
    has_ctx = a_ctx is not None
    mrow = lambda i: (jnp.where(i < nl, i // tpb, n_batch), 0, 0)
    lat = lambda w: pl.BlockSpec((tm, w), lambda i: (jnp.minimum(i, nl - 1), 0))
    ctx = lambda w: pl.BlockSpec((tm, w), lambda i: (jnp.maximum(i - nl, 0), 0))
    if has_ctx:
        in_specs = [lat(kdim), ctx(kdim), lat(D_MODEL), ctx(D_MODEL)]
        args = [a_lat, a_ctx, x_lat, x_ctx]
    else:
        in_specs = [lat(kdim), lat(D_MODEL)]
        args = [a_lat, x_lat]
    in_specs += [
        pl.BlockSpec((1, N_MOD, D_MODEL), mrow),
        pl.BlockSpec((kdim, D_MODEL), lambda i: (0, 0)),
        pl.BlockSpec((D_MODEL, 2 * LANES), lambda i: (0, 0)),
        pl.BlockSpec((1, LANES), lambda i: (0, 0)),
        pl.BlockSpec((tm, tm), lambda i: (0, 0)),
    ]
    tri = (jnp.arange(tm)[:, None] > jnp.arange(tm)[None, :]).astype(BF16)
    args += [mod, w_out, wr, br, tri]
    rows = nt * tm
    row_out = lambda w: pl.BlockSpec((tm, w), lambda i: (i, 0))
    return pl.pallas_call(
        functools.partial(_out_kernel, nl=nl, has_ctx=has_ctx),
        grid=(nt,),
        in_specs=in_specs,
        out_specs=[row_out(D_MODEL), row_out(D_MODEL), row_out(2), row_out(2), row_out(LANES),
                   pl.BlockSpec((1, SUBLANES, LANES), lambda i: (i, 0, 0))],
        out_shape=[jax.ShapeDtypeStruct((rows, D_MODEL), F32),
                   jax.ShapeDtypeStruct((rows, D_MODEL), BF16),
                   jax.ShapeDtypeStruct((rows, 2), jnp.int32),
                   jax.ShapeDtypeStruct((rows, 2), jnp.int32),
                   jax.ShapeDtypeStruct((rows, LANES), F32),
                   jax.ShapeDtypeStruct((nt, SUBLANES, LANES), F32)],
        compiler_params=_cparams(1),
        name="mixer_out_ctx" if has_ctx else "mixer_out",
    )(*args)


def _router_weights(w_group, b_group, w_expert, b_expert):
    w = jnp.concatenate([w_group, w_expert], axis=1)
    w = jnp.pad(w, ((0, 0), (0, LANES - w.shape[1])))
    hi = w.astype(BF16)
    lo = (w - hi.astype(F32)).astype(BF16)
    b = jnp.pad(jnp.concatenate([b_group, b_expert]), (0, LANES - N_GROUPS - N_EXPERTS))
    return jnp.concatenate([hi, lo], axis=1), b.reshape(1, LANES).astype(F32)


def _slot_tables(runs, n_tok):
    blk = EXPERT_BLOCK
    nt = runs.shape[0]
    run = runs[:, 0, N_GROUPS:N_GROUPS + N_EXPERTS].astype(jnp.int32)
    total = jnp.sum(run, axis=0)
    padded = (total + blk - 1) // blk * blk
    ends_p = jnp.cumsum(padded)
    base = (ends_p - padded)[None, :] + jnp.cumsum(run, axis=0) - run
    ends_t = jnp.cumsum(run, axis=1)
    shift = base - (ends_t - run)
    shift_rows = jnp.pad(shift.astype(F32), ((0, 0), (N_GROUPS, LANES - N_GROUPS - N_EXPERTS)))[:, None, :]
    chunk_row = jnp.arange(SORT_ROWS // SEG_ALIGN, dtype=jnp.int32) * SEG_ALIGN
    starts_t = ends_t - run
    owns = (starts_t[:, None, :] <= chunk_row[None, :, None]) & (chunk_row[None, :, None] < ends_t[:, None, :])
    chunk_shift = jnp.sum(jnp.where(owns, shift[:, None, :], 0), axis=-1)[:, None, :]
    n_chunks = ends_t[:, -1] // SEG_ALIGN
    n_blocks = -(-(2 * n_tok + nt * N_EXPERTS * (SEG_ALIGN - 1)) // blk) + N_EXPERTS
    first_slot = jnp.arange(n_blocks, dtype=jnp.int32) * blk
    block_e = jnp.minimum(jnp.sum(ends_p[None, :] <= first_slot[:, None], axis=1), N_EXPERTS - 1).astype(jnp.int32)
    n_valid = (ends_p[-1] // blk).astype(jnp.int32).reshape(1)
    pad_tab = jnp.concatenate([ends_p - padded + total, (padded - total) // SEG_ALIGN]).astype(jnp.int32)
    return n_chunks, chunk_shift, shift_rows, pad_tab, block_e, n_valid, n_blocks


def _dispatch_kernel(nchunk_ref, pad_ref, cshift_ref, h_ref, e_ref, pos_ref, srow_ref,
                     o_ref, dest_ref, perm_ref, sorted_ref, zero_ref, sem):
    tm = h_ref.shape[0]
    i = pl.program_id(0)
    pos = pos_ref[...].astype(F32)
    lane = lax.broadcasted_iota(jnp.int32, (tm, LANES), 1).astype(F32)

    e_lane = e_ref[...].astype(F32) + N_GROUPS
    dest = [jnp.sum(jnp.where(lane == e_lane[:, k:k + 1], srow_ref[0], 0.0), axis=-1, keepdims=True)
            + pos[:, k:k + 1] for k in range(2)]
    dest_ref[...] = jnp.concatenate(dest, axis=1).astype(jnp.int32)

    pos_row = [jnp.transpose(jnp.where(lane == 0.0, pos[:, k:k + 1], 0.0))[0:1, :] for k in range(2)]
    for s0 in range(0, SORT_ROWS, SORT_BLOCK):
        s = (lax.broadcasted_iota(jnp.int32, (SORT_BLOCK, tm), 0) + s0).astype(F32)
        perm_ref[s0:s0 + SORT_BLOCK, :] = jnp.where((s == pos_row[0]) | (s == pos_row[1]), 1.0, 0.0).astype(BF16)
    half = D_MODEL // 2
    lo = lax.bitcast_convert_type(_dot(perm_ref[...], h_ref[:, :half]), jnp.int32)
    hi = lax.bitcast_convert_type(_dot(perm_ref[...], h_ref[:, half:]), jnp.int32)
    sorted_ref[...] = lax.shift_right_logical(lo, 16) | (hi & jnp.int32(-65536))

    def chunk_copy(c):
        r = pl.multiple_of(c * SEG_ALIGN, SEG_ALIGN)
        d = pl.multiple_of(r + cshift_ref[0, 0, c], SEG_ALIGN)
        return pltpu.make_async_copy(sorted_ref.at[pl.ds(r, SEG_ALIGN), :], o_ref.at[pl.ds(d, SEG_ALIGN), :], sem)

    def start(c, carry):
        chunk_copy(c).start()
        return carry

    n = nchunk_ref[i]
    lax.fori_loop(0, n, start, 0)

    def wait_rows(rows):
        def body(c, carry):
            pltpu.make_async_copy(sorted_ref.at[pl.ds(0, rows), :], o_ref.at[pl.ds(0, rows), :], sem).wait()
            return carry
        return body

    lax.fori_loop(0, n // WAIT_GROUP, wait_rows(WAIT_GROUP * SEG_ALIGN), 0)
    lax.fori_loop(0, n % WAIT_GROUP, wait_rows(SEG_ALIGN), 0)

    @pl.when(i == pl.num_programs(0) - 1)
    def _():
        zero_ref[...] = jnp.zeros_like(zero_ref)

        def for_each_pad_chunk(fn):
            for e in range(N_EXPERTS):
                first = pad_ref[e]

                def body(c, carry):
                    d = pl.multiple_of(first + c * SEG_ALIGN, SEG_ALIGN)
                    fn(pltpu.make_async_copy(zero_ref, o_ref.at[pl.ds(d, SEG_ALIGN), :], sem))
                    return carry

                lax.fori_loop(0, pad_ref[N_EXPERTS + e], body, 0)

        for_each_pad_chunk(lambda cp: cp.start())
        for_each_pad_chunk(lambda cp: cp.wait())


def _dispatch(h2, e_ids, pos, n_chunks, chunk_shift, shift_rows, pad_tab, n_slots):
    n_tok = h2.shape[0]
    tm = ROW_TILE
    nt = n_tok // tm
    width = D_MODEL // 2
    return pl.pallas_call(
        _dispatch_kernel,
        grid_spec=pltpu.PrefetchScalarGridSpec(
            num_scalar_prefetch=2,
            grid=(nt,),
            in_specs=[pl.BlockSpec((1, 1, SORT_ROWS // SEG_ALIGN), lambda i, n, p: (i, 0, 0), memory_space=pltpu.SMEM),
                      pl.BlockSpec((tm, D_MODEL), lambda i, n, p: (i, 0)),
                      pl.BlockSpec((tm, 2), lambda i, n, p: (i, 0)),
                      pl.BlockSpec((tm, 2), lambda i, n, p: (i, 0)),
                      pl.BlockSpec((1, 1, LANES), lambda i, n, p: (i, 0, 0))],
            out_specs=[pl.BlockSpec(memory_space=pl.ANY),
                       pl.BlockSpec((tm, 2), lambda i, n, p: (i, 0))],
            scratch_shapes=[pltpu.VMEM((SORT_ROWS, tm), BF16), pltpu.VMEM((SORT_ROWS, width), jnp.int32),
                            pltpu.VMEM((SEG_ALIGN, width), jnp.int32), pltpu.SemaphoreType.DMA(())],
        ),
        out_shape=[jax.ShapeDtypeStruct((n_slots, width), jnp.int32),
                   jax.ShapeDtypeStruct((n_tok, 2), jnp.int32)],
        compiler_params=_cparams(1),
        name="dispatch",
    )(n_chunks, pad_tab, chunk_shift, h2, e_ids, pos, shift_rows)


def _expert_kernel(be_ref, nv_ref, x_ref, wg_ref, wu_ref, wd_ref, o_ref, wg_b, wu_b, wd_b):
    i = pl.program_id(0)
    valid = i < nv_ref[0]
    new_expert = (i == 0) | (be_ref[i] != be_ref[jnp.maximum(i - 1, 0)])

    @pl.when(valid & new_expert)
    def _():
        wg_b[...] = wg_ref[0, 0].astype(BF16)
        wu_b[...] = wu_ref[0, 0].astype(BF16)
        wd_b[...] = wd_ref[0, 0].astype(BF16)

    @pl.when(valid)
    def _():
        x = _unpack_bf16_pairs(x_ref[...])
        mid = (_silu(_dot(x, wg_b[...])) * _dot(x, wu_b[...])).astype(BF16)
        for a, b in _row_halves(o_ref.shape[1]):
            o_ref[:, a:b] = _dot(mid, wd_b[:, a:b]).astype(BF16)

    @pl.when(jnp.logical_not(valid))
    def _():
        o_ref[...] = jnp.zeros_like(o_ref)


def _expert_ffn(xs_sorted, block_e, n_valid, w_gate, w_up, w_down, layer):
    n_slots = xs_sorted.shape[0]
    blk = EXPERT_BLOCK
    wspec = lambda r, c: pl.BlockSpec((1, 1, r, c), lambda i, be, nv: (layer, be[i], 0, 0))
    return pl.pallas_call(
        _expert_kernel,
        grid_spec=pltpu.PrefetchScalarGridSpec(
            num_scalar_prefetch=2,
            grid=(n_slots // blk,),
            in_specs=[
                pl.BlockSpec((blk, D_MODEL // 2), lambda i, be, nv: (i, 0)),
                wspec(D_MODEL, EXPERT_HIDDEN), wspec(D_MODEL, EXPERT_HIDDEN), wspec(EXPERT_HIDDEN, D_MODEL),
            ],
            out_specs=pl.BlockSpec((blk, D_MODEL), lambda i, be, nv: (i, 0)),
            scratch_shapes=[pltpu.VMEM((D_MODEL, EXPERT_HIDDEN), BF16),
                            pltpu.VMEM((D_MODEL, EXPERT_HIDDEN), BF16),
                            pltpu.VMEM((EXPERT_HIDDEN, D_MODEL), BF16)],
        ),
        out_shape=jax.ShapeDtypeStruct((n_slots, D_MODEL), BF16),
        compiler_params=_cparams(1),
        name="expert_ffn",
    )(block_e, n_valid, xs_sorted, w_gate, w_up, w_down)


def _moe(h2, e_ids, pos, runs, n_tok, w_gate, w_up, w_down, layer):
    n_chunks, chunk_shift, shift_rows, pad_tab, block_e, n_valid, n_blocks = _slot_tables(runs, n_tok)
    xs_sorted, dest = _dispatch(h2, e_ids, pos, n_chunks, chunk_shift, shift_rows, pad_tab, n_blocks * EXPERT_BLOCK)
    ys = _expert_ffn(xs_sorted, block_e, n_valid, w_gate, w_up, w_down, layer)
    return ys[dest[:, 0]], ys[dest[:, 1]]


def _final_kernel(x1_ref, ya_ref, yb_ref, r_ref, m_ref, gain_ref, o_ref):
    r = r_ref[...]
    f = r[:, 2:3] * ya_ref[...].astype(F32) + r[:, 3:4] * yb_ref[...].astype(F32)
    x2 = x1_ref[...] + m_ref[0, 5:6, :] * f
    ms = jnp.mean(x2 * x2, axis=-1, keepdims=True)
    o_ref[...] = x2 * lax.rsqrt(ms + NORM_EPS) * gain_ref[...]


def _final(x1, ya, yb, route, mod, gain, geo, nt):
    t_rows, nl, tpb, n_batch = geo
    tm = ROW_TILE
    rows = lambda w: pl.BlockSpec((tm, w), lambda i: (i, 0))
    return pl.pallas_call(
        _final_kernel,
        grid=(nt,),
        in_specs=[rows(D_MODEL), rows(D_MODEL), rows(D_MODEL), rows(LANES),
                  pl.BlockSpec((1, N_MOD, D_MODEL), lambda i: (jnp.where(i < nl, i // tpb, n_batch), 0, 0)),
                  pl.BlockSpec((1, D_MODEL), lambda i: (0, 0))],
        out_specs=rows(D_MODEL),
        out_shape=jax.ShapeDtypeStruct((nt * tm, D_MODEL), F32),
        compiler_params=_cparams(1),
        name="combine_final",
    )(x1, ya, yb, route, mod, gain.reshape(1, D_MODEL))


def _proj_att_kernel(x1_ref, ya_ref, yb_ref, r_ref, m0_ref, m_ref, w_ref,
                     qc_ref, qa_ref, qb_ref, kc_ref, ka_ref, kb_ref, o_ref, x2_ref, h_ref):
    for a, b in _row_halves(h_ref.shape[0]):
        r = r_ref[a:b, :]
        f = r[:, 2:3] * ya_ref[a:b, :].astype(F32) + r[:, 3:4] * yb_ref[a:b, :].astype(F32)
        x2 = x1_ref[a:b, :] + m0_ref[0, 5:6, :] * f
        x2_ref[a:b, :] = x2
        h_ref[a:b, :] = _modulate(x2, m_ref[0, 0:1, :], m_ref[0, 1:2, :]).astype(BF16)

    ones = jnp.ones((LANES, LANES), BF16)

    def normed_rope(xs, c_ref, a_ref, b_ref):
        rs = lax.rsqrt(_dot((xs * xs).astype(BF16), ones) * (1.0 / ATT_HEAD_DIM) + NORM_EPS)
        r = (xs * c_ref[...] + pltpu.roll(xs, LANES - ATT_HEAD_DIM // 4, 1) * a_ref[...]
             + pltpu.roll(xs, ATT_HEAD_DIM // 4, 1) * b_ref[...])
        return (r * rs).astype(BF16)

    gw = PROJ_GROUP
    for c0 in range(0, ATT_IN, gw):
        acc = _dot(h_ref[...], w_ref[:, c0:c0 + gw])
        for g in range(gw // LANES):
            col = c0 + g * LANES
            xs = acc[:, g * LANES:(g + 1) * LANES]
            if col < ATT_QW:
                o_ref[:, col:col + LANES] = normed_rope(xs, qc_ref, qa_ref, qb_ref)
            elif col < ATT_QW + ATT_KW:
                o_ref[:, col:col + LANES] = normed_rope(xs, kc_ref, ka_ref, kb_ref)
            else:
                o_ref[:, col:col + LANES] = xs.astype(BF16)


def _proj_att(x1, ya, yb, route, mod_prev, mod, w_qkv, q_tabs, k_tabs, geo):
    t_rows, nl, tpb, n_batch = geo
    tm = ROW_TILE
    nt = t_rows // tm
    rope = pl.BlockSpec((tm, ATT_HEAD_DIM), lambda i: (jnp.where(i < nl, i % tpb, tpb), 0))
    rows = lambda w: pl.BlockSpec((tm, w), lambda i: (i, 0))
    mrow = pl.BlockSpec((1, N_MOD, D_MODEL), lambda i: (jnp.where(i < nl, i // tpb, n_batch), 0, 0))
    return pl.pallas_call(
        _proj_att_kernel,
        grid=(nt,),
        in_specs=[
            rows(D_MODEL), rows(D_MODEL), rows(D_MODEL), rows(LANES), mrow, mrow,
            pl.BlockSpec((D_MODEL, ATT_IN), lambda i: (0, 0)),
            rope, rope, rope, rope, rope, rope,
        ],
        out_specs=[rows(ATT_IN), rows(D_MODEL)],
        out_shape=[jax.ShapeDtypeStruct((t_rows, ATT_IN), BF16),
                   jax.ShapeDtypeStruct((t_rows, D_MODEL), F32)],
        scratch_shapes=[pltpu.VMEM((tm, D_MODEL), BF16)],
        compiler_params=_cparams(1),
        name="proj_att",
    )(x1, ya, yb, route, mod_prev, mod, w_qkv, *q_tabs, *k_tabs)


def _att_kernel(q_ref, kl_ref, vl_ref, kc_ref, vc_ref, o_ref, k_all, v_ext, s_ref, p_ref):
    tq = q_ref.shape[0]
    seq, ctx_len = kl_ref.shape[0], kc_ref.shape[0]

    @pl.when(pl.program_id(2) == 0)
    def _():
        k_all[0:seq, :] = kl_ref[...]
        k_all[seq:seq + ctx_len, :] = kc_ref[...]
        v_ext[0:seq, 0:LANES] = vl_ref[...]
        v_ext[seq:seq + ctx_len, 0:LANES] = vc_ref[...]
        v_ext[:, LANES:] = jnp.ones((seq + ctx_len, LANES), BF16)

    rows = ATT_SOFTMAX_ROWS

    def scores(g):
        s_ref[g * tq:(g + 1) * tq, :] = _dot_nt(q_ref[:, g * LANES:(g + 1) * LANES], k_all[...])

    def softmax(g):
        for r0 in range(g * tq, (g + 1) * tq, rows):
            m = jnp.max(s_ref[r0:r0 + rows, :], axis=-1, keepdims=True)
            p_ref[r0:r0 + rows, :] = jnp.exp2(s_ref[r0:r0 + rows, :] - m).astype(BF16)

    def weighted_values(g):
        oe = _dot(p_ref[g * tq:(g + 1) * tq, :], v_ext[...])
        o_ref[:, g * LANES:(g + 1) * LANES] = (oe[:, :LANES] / oe[:, LANES:]).astype(BF16)

    scores(0)
    scores(1)
    for g in range(ATT_GROUP):
        if g + 2 < ATT_GROUP:
            scores(g + 2)
        softmax(g)
        weighted_values(g)


def _attention(p, n_batch, seq, ctx_len):
    tq = ATT_Q_TILE
    nq = seq // tq
    cb = n_batch * seq // ctx_len
    ko, vo = ATT_QW // LANES, (ATT_QW + ATT_KW) // LANES
    gw = ATT_GROUP * ATT_HEAD_DIM
    n_keys = seq + ctx_len
    return pl.pallas_call(
        _att_kernel,
        grid=(n_batch, ATT_KV_HEADS, nq),
        in_specs=[
            pl.BlockSpec((tq, gw), lambda b, h, t: (b * nq + t, h)),
            pl.BlockSpec((seq, LANES), lambda b, h, t: (b, ko + h)),
            pl.BlockSpec((seq, LANES), lambda b, h, t: (b, vo + h)),
            pl.BlockSpec((ctx_len, LANES), lambda b, h, t: (cb + b, ko + h)),
            pl.BlockSpec((ctx_len, LANES), lambda b, h, t: (cb + b, vo + h)),
        ],
        out_specs=pl.BlockSpec((tq, gw), lambda b, h, t: (b * nq + t, h)),
        out_shape=jax.ShapeDtypeStruct((n_batch * seq, ATT_QW), BF16),
        scratch_shapes=[pltpu.VMEM((n_keys, LANES), BF16), pltpu.VMEM((n_keys, 2 * LANES), BF16),
                        pltpu.VMEM((ATT_GROUP * tq, n_keys), F32), pltpu.VMEM((ATT_GROUP * tq, n_keys), BF16)],
        compiler_params=_cparams(3),
        name="attention",
    )(p, p, p, p, p)


def kernel(x, c, ctx, c_ctx, w_mod, b_mod, ret_w_in, ret_w_out, ret_log_decay_fwd, ret_log_decay_bwd, att_w_qkv, att_w_o, att_q_gain, att_k_gain, moe_w_group, moe_b_group, moe_w_expert, moe_b_expert, moe_w_gate, moe_w_up, moe_w_down, final_norm_gain):
    n_batch, seq, d = x.shape
    ctx_len = ctx.shape[1]
    tm = ROW_TILE
    assert d == D_MODEL and w_mod.shape[0] == DEPTH == 2
    assert seq % tm == 0 and (n_batch * ctx_len) % tm == 0 and seq % GRID_W == 0
    assert seq % RET_CHUNK == 0 and ctx_len % RET_CHUNK == 0 and (n_batch * seq) % ctx_len == 0
    n_lat = n_batch * seq
    t_rows = n_lat + n_batch * ctx_len
    nl = n_lat // tm
    geo = (t_rows, nl, seq // tm, n_batch)
    nt = t_rows // tm

    x_lat, x_ctx = x.reshape(n_lat, d), ctx.reshape(n_batch * ctx_len, d)
    pad_rows = -(n_batch + 1) % MOD_ROWS_PAD
    c_rows = jnp.concatenate([c, c_ctx[None, :], jnp.zeros((pad_rows, d), F32)], axis=0)
    mod = _mod_vectors(c_rows, w_mod, b_mod)

    cos, sin = _ret_rope_tables(seq, tm)
    p = _proj_ret(x_lat, x_ctx, mod[0], ret_w_in[0].astype(BF16), cos, sin, geo)
    lg = jnp.stack([ret_log_decay_fwd[0], ret_log_decay_bwd[0]]).astype(F32)
    y_lat, y_ctx = _retention(p, lg, n_batch, seq, ctx_len)
    wr, br = _router_weights(moe_w_group[0], moe_b_group[0], moe_w_expert[0], moe_b_expert[0])
    x1, h2, e_ids, pos, route, runs = _mixer_out(
        y_lat, y_ctx, x_lat, x_ctx, mod[0], ret_w_out[0].astype(BF16), wr, br, geo, nt)
    ya, yb = _moe(h2, e_ids, pos, runs, t_rows, moe_w_gate, moe_w_up, moe_w_down, 0)

    q_tabs = _att_rope_tables(seq, tm, att_q_gain[0], ATT_HEAD_DIM ** -0.5 * LOG2_E)
    k_tabs = _att_rope_tables(seq, tm, att_k_gain[0], 1.0)
    p, xs = _proj_att(x1, ya, yb, route, mod[0], mod[1], att_w_qkv[0].astype(BF16), q_tabs, k_tabs, geo)
    o = _attention(p, n_batch, seq, ctx_len)
    wr, br = _router_weights(moe_w_group[1], moe_b_group[1], moe_w_expert[1], moe_b_expert[1])
    x1, h2, e_ids, pos, route, runs = _mixer_out(
        o, None, xs, None, mod[1], att_w_o[0].astype(BF16), wr, br, geo, nl)
    ya, yb = _moe(h2, e_ids, pos, runs, n_lat, moe_w_gate, moe_w_up, moe_w_down, 1)
    out = _final(x1, ya, yb, route, mod[1], final_norm_gain, geo, nl)
    return out.reshape(n_batch, seq, d)
```

```python
import functools

import jax
import jax.numpy as jnp
from jax import lax
from jax.experimental import pallas as pl
from jax.experimental.pallas import tpu as pltpu

F32 = jnp.float32
BF16 = jnp.bfloat16

D_MODEL = 1024
DEPTH = 2
GRID_W = 64
N_MOD = 6
NORM_EPS = 1e-6
HEAD_NORM_EPS = 1e-5
ROPE_THETA = 10000.0

RET_HEADS = 4
RET_DK = D_MODEL // RET_HEADS
RET_DV = 2 * RET_DK
RET_QK = RET_HEADS * RET_DK
RET_V = RET_HEADS * RET_DV
RET_IN = 2 * RET_QK + 2 * RET_V

ATT_HEAD_DIM = 128
ATT_Q_HEADS = D_MODEL // ATT_HEAD_DIM
ATT_KV_HEADS = 2
ATT_GROUP = ATT_Q_HEADS // ATT_KV_HEADS
ATT_QW = ATT_Q_HEADS * ATT_HEAD_DIM
ATT_KW = ATT_KV_HEADS * ATT_HEAD_DIM
ATT_IN = ATT_QW + 2 * ATT_KW

N_GROUPS = 4
EXPERTS_PER_GROUP = 8
N_EXPERTS = N_GROUPS * EXPERTS_PER_GROUP
EXPERT_HIDDEN = D_MODEL // 2

LANES = 128
SUBLANES = 8
ROW_TILE = 512
PROJ_GROUP = 512
RET_CHUNK = 256
RET_PAIR = 2
RET_UNROLL_FWD = 4
RET_UNROLL_BWD = 2
ATT_Q_TILE = 512
ATT_SOFTMAX_ROWS = 64
LOG2_E = 1.4426950408889634
EXPERT_BLOCK = 1024
SEG_ALIGN = 8
SORT_ROWS = 2 * ROW_TILE + 256
SORT_BLOCK = 256
WAIT_GROUP = 8
MOD_ROWS_PAD = 8
VMEM_LIMIT = 56 * 1024 * 1024


def _cparams(n_axes):
    return pltpu.CompilerParams(
        dimension_semantics=("arbitrary",) * n_axes, vmem_limit_bytes=VMEM_LIMIT)


def _silu(v):
    return v * (1.0 / (1.0 + jnp.exp(-v)))


def _modulate(x, shift, scale):
    ms = jnp.mean(x * x, axis=-1, keepdims=True)
    return x * lax.rsqrt(ms + NORM_EPS) * (1.0 + scale) + shift


def _split_bf16(v):
    hi = v.astype(BF16)
    lo = (v - hi.astype(F32)).astype(BF16)
    return hi, lo


def _dot(a, b):
    return jnp.dot(a, b, preferred_element_type=F32)


def _dot_nt(a, b):
    return lax.dot_general(a, b, (((1,), (1,)), ((), ())), preferred_element_type=F32)


def _row_halves(n):
    return ((0, n // 2), (n // 2, n))


def _mod_kernel(c_ref, w_ref, b_ref, o_ref):
    a_hi, a_lo = _split_bf16(_silu(c_ref[...]))
    w_hi, w_lo = _split_bf16(w_ref[0])
    acc = _dot(a_hi, w_hi) + _dot(a_lo, w_hi) + _dot(a_hi, w_lo)
    o_ref[0] = acc + b_ref[0]


def _mod_vectors(c_rows, w_mod, b_mod):
    rows = c_rows.shape[0]
    n = w_mod.shape[-1]
    tn = 1024
    out = pl.pallas_call(
        _mod_kernel,
        grid=(DEPTH, n // tn),
        in_specs=[
            pl.BlockSpec((rows, D_MODEL), lambda l, j: (0, 0)),
            pl.BlockSpec((1, D_MODEL, tn), lambda l, j: (l, 0, j)),
            pl.BlockSpec((1, 1, tn), lambda l, j: (l, 0, j)),
        ],
        out_specs=pl.BlockSpec((1, rows, tn), lambda l, j: (l, 0, j)),
        out_shape=jax.ShapeDtypeStruct((DEPTH, rows, n), F32),
        compiler_params=_cparams(2),
        name="mod_vectors",
    )(c_rows, w_mod, b_mod.reshape(DEPTH, 1, n))
    return out.reshape(DEPTH, rows, N_MOD, D_MODEL)


def _rope_angles(n, head_dim):
    rows = n // GRID_W
    r = jnp.repeat(jnp.arange(rows), GRID_W).astype(F32)
    col = jnp.tile(jnp.arange(GRID_W), rows).astype(F32)
    nf = head_dim // 4
    inv = ROPE_THETA ** (-jnp.arange(nf, dtype=F32) / nf)
    ar = r[:, None] * inv
    ac = col[:, None] * inv
    return jnp.concatenate([ar, ar, ac, ac], axis=-1)


def _ret_rope_tables(n, pad):
    ang = _rope_angles(n, RET_DK)
    lane = jnp.arange(RET_DK) % LANES
    sin = jnp.where(lane < LANES // 2, -jnp.sin(ang), jnp.sin(ang))
    cos = jnp.concatenate([jnp.cos(ang), jnp.ones((pad, RET_DK), F32)], axis=0)
    sin = jnp.concatenate([sin, jnp.zeros((pad, RET_DK), F32)], axis=0)
    return cos, sin


def _att_rope_tables(n, pad, gain, scale):
    ang = _rope_angles(n, ATT_HEAD_DIM)
    quarter = ATT_HEAD_DIM // 4
    first = jnp.arange(ATT_HEAD_DIM) % (2 * quarter) < quarter
    sin = jnp.sin(ang)
    c = jnp.cos(ang) * gain * scale
    a = jnp.where(first, -sin, 0.0) * jnp.roll(gain, ATT_HEAD_DIM - quarter) * scale
    b = jnp.where(first, 0.0, sin) * jnp.roll(gain, quarter) * scale
    ident = jnp.broadcast_to(gain * scale, (pad, ATT_HEAD_DIM))
    zeros = jnp.zeros((pad, ATT_HEAD_DIM), F32)
    return (jnp.concatenate([c, ident], axis=0), jnp.concatenate([a, zeros], axis=0),
            jnp.concatenate([b, zeros], axis=0))


def _load_rows(is_lat, lat_ref, ctx_ref, a, b):
    return jnp.where(is_lat, lat_ref[a:b, :], ctx_ref[a:b, :])


def _proj_ret_kernel(xl_ref, xc_ref, m_ref, w_ref, cos_ref, sin_ref, o_ref, h_ref, *, nl):
    is_lat = pl.program_id(0) < nl
    for a, b in _row_halves(h_ref.shape[0]):
        x = _load_rows(is_lat, xl_ref, xc_ref, a, b)
        h_ref[a:b, :] = _modulate(x, m_ref[0, 0:1, :], m_ref[0, 1:2, :]).astype(BF16)

    gw = PROJ_GROUP
    for c0 in range(0, RET_IN, gw):
        acc = _dot(h_ref[...], w_ref[:, c0:c0 + gw])
        if c0 < 2 * RET_QK:
            scale = 1.0 if c0 < RET_QK else RET_DK ** -0.5
            for g in range(gw // LANES):
                xs = acc[:, g * LANES:(g + 1) * LANES]
                t = ((c0 // LANES + g) % 2) * LANES
                r = xs * cos_ref[:, t:t + LANES] + pltpu.roll(xs, LANES // 2, 1) * sin_ref[:, t:t + LANES]
                o_ref[:, c0 + g * LANES:c0 + (g + 1) * LANES] = (r * scale).astype(BF16)
        elif c0 < 2 * RET_QK + RET_V:
            o_ref[:, c0:c0 + gw] = acc.astype(BF16)
        else:
            o_ref[:, c0:c0 + gw] = _silu(acc).astype(BF16)


def _proj_ret(x_lat, x_ctx, mod, w_in, cos, sin, geo):
    t_rows, nl, tpb, n_batch = geo
    tm = ROW_TILE
    nt = t_rows // tm
    return pl.pallas_call(
        functools.partial(_proj_ret_kernel, nl=nl),
        grid=(nt,),
        in_specs=[
            pl.BlockSpec((tm, D_MODEL), lambda i: (jnp.minimum(i, nl - 1), 0)),
            pl.BlockSpec((tm, D_MODEL), lambda i: (jnp.maximum(i - nl, 0), 0)),
            pl.BlockSpec((1, N_MOD, D_MODEL), lambda i: (jnp.where(i < nl, i // tpb, n_batch), 0, 0)),
            pl.BlockSpec((D_MODEL, RET_IN), lambda i: (0, 0)),
            pl.BlockSpec((tm, RET_DK), lambda i: (jnp.where(i < nl, i % tpb, tpb), 0)),
            pl.BlockSpec((tm, RET_DK), lambda i: (jnp.where(i < nl, i % tpb, tpb), 0)),
        ],
        out_specs=pl.BlockSpec((tm, RET_IN), lambda i: (i, 0)),
        out_shape=jax.ShapeDtypeStruct((t_rows, RET_IN), BF16),
        scratch_shapes=[pltpu.VMEM((tm, D_MODEL), BF16)],
        compiler_params=_cparams(1),
        name="proj_ret",
    )(x_lat, x_ctx, mod, w_in, cos, sin)


def _ret_kernel(lg_ref, ql, kl, vl, gl, qc, kc, vc, gc, ol, oc,
                sf_ref, sb_ref, y_ref, mask_ref):
    c = RET_CHUNK
    heads = [pl.program_id(0) * RET_PAIR + j for j in range(RET_PAIR)]
    lgf = [lg_ref[0, hd] for hd in heads]
    lgb = [lg_ref[1, hd] for hd in heads]

    @pl.when(pl.program_id(1) == 0)
    def _():
        ii = lax.broadcasted_iota(jnp.int32, (c, c), 0).astype(F32)
        jj = lax.broadcasted_iota(jnp.int32, (c, c), 1).astype(F32)
        d = ii - jj
        for j in range(RET_PAIR):
            mask_ref[j] = (jnp.where(d >= 0, jnp.exp(lgf[j] * jnp.maximum(d, 0.0)), 0.0)
                           + jnp.where(d <= 0, jnp.exp(lgb[j] * jnp.maximum(-d, 0.0)), 0.0))

    pos = lax.broadcasted_iota(jnp.int32, (c, 1), 0).astype(F32)
    xi_f = [jnp.exp(l * (pos + 1.0)) for l in lgf]
    zeta_f = [jnp.exp(l * (c - 1.0 - pos)) for l in lgf]
    xi_b = [jnp.exp(l * (c - pos)) for l in lgb]
    zeta_b = [jnp.exp(l * pos) for l in lgb]
    dec_f = [jnp.exp(jnp.full((1, RET_DV), l * c, F32)) for l in lgf]
    dec_b = [jnp.exp(jnp.full((1, RET_DV), l * c, F32)) for l in lgb]

    def state_update(s_ref, j, k, v, zeta, dec):
        kz = (k.astype(F32) * zeta).T.astype(BF16)
        s_ref[j] = s_ref[j] * dec + _dot(kz, v)

    def chunk(ref, r0, j, width):
        return ref[pl.ds(r0, c), j * width:(j + 1) * width]

    def run(q_ref, k_ref, v_ref, g_ref, o_ref, n):
        def fwd(t, carry):
            r0 = pl.multiple_of(t * c, c)
            for j in range(RET_PAIR):
                q, k, v = chunk(q_ref, r0, j, RET_DK), chunk(k_ref, r0, j, RET_DK), chunk(v_ref, r0, j, RET_DV)
                a = (_dot_nt(q, k) * mask_ref[j]).astype(BF16)
                cross = _dot(q, sf_ref[j].astype(BF16)) * xi_f[j]
                y_ref[pl.ds(r0, c), j * RET_DV:(j + 1) * RET_DV] = _dot(a, v) + cross
                state_update(sf_ref, j, k, v, zeta_f[j], dec_f[j])
            return carry

        lax.fori_loop(0, n, fwd, 0, unroll=min(n, RET_UNROLL_FWD))

        def bwd(t, carry):
            r0 = pl.multiple_of((n - 1 - t) * c, c)
            for j in range(RET_PAIR):
                q, k, v = chunk(q_ref, r0, j, RET_DK), chunk(k_ref, r0, j, RET_DK), chunk(v_ref, r0, j, RET_DV)
                y = chunk(y_ref, r0, j, RET_DV) + _dot(q, sb_ref[j].astype(BF16)) * xi_b[j]
                mu = jnp.mean(y, axis=-1, keepdims=True)
                yc = y - mu
                var = jnp.mean(yc * yc, axis=-1, keepdims=True)
                yn = yc * lax.rsqrt(var + HEAD_NORM_EPS)
                o_ref[pl.ds(r0, c), j * RET_DV:(j + 1) * RET_DV] = (
                    yn * chunk(g_ref, r0, j, RET_DV).astype(F32)).astype(BF16)
                state_update(sb_ref, j, k, v, zeta_b[j], dec_b[j])
            return carry

        lax.fori_loop(0, n, bwd, 0, unroll=min(n, RET_UNROLL_BWD))

    sf_ref[...] = jnp.zeros_like(sf_ref)
    sb_ref[...] = jnp.zeros_like(sb_ref)
    run(qc, kc, vc, gc, oc, qc.shape[0] // c)
    run(ql, kl, vl, gl, ol, ql.shape[0] // c)


def _retention(p, lg, n_batch, seq, ctx_len):
    cb = n_batch * seq // ctx_len
    kw, vw = RET_PAIR * RET_DK, RET_PAIR * RET_DV
    qo, ko = 0, RET_QK // kw
    vo, go = 2 * RET_QK // vw, (2 * RET_QK + RET_V) // vw
    lat = lambda w, off: pl.BlockSpec((seq, w), lambda h, b, lg_: (b, off + h))
    ctx = lambda w, off: pl.BlockSpec((ctx_len, w), lambda h, b, lg_: (cb + b, off + h))
    return pl.pallas_call(
        _ret_kernel,
        grid_spec=pltpu.PrefetchScalarGridSpec(
            num_scalar_prefetch=1,
            grid=(RET_HEADS // RET_PAIR, n_batch),
            in_specs=[lat(kw, qo), lat(kw, ko), lat(vw, vo), lat(vw, go),
                      ctx(kw, qo), ctx(kw, ko), ctx(vw, vo), ctx(vw, go)],
            out_specs=[pl.BlockSpec((seq, vw), lambda h, b, lg_: (b, h)),
                       pl.BlockSpec((ctx_len, vw), lambda h, b, lg_: (b, h))],
            scratch_shapes=[pltpu.VMEM((RET_PAIR, RET_DK, RET_DV), F32), pltpu.VMEM((RET_PAIR, RET_DK, RET_DV), F32),
                            pltpu.VMEM((seq, vw), F32), pltpu.VMEM((RET_PAIR, RET_CHUNK, RET_CHUNK), F32)],
        ),
        out_shape=[jax.ShapeDtypeStruct((n_batch * seq, RET_V), BF16),
                   jax.ShapeDtypeStruct((n_batch * ctx_len, RET_V), BF16)],
        compiler_params=_cparams(2),
        name="retention",
    )(lg, p, p, p, p, p, p, p, p)


def _route(h2, wr_ref, br_ref):
    hi, lo = _split_bf16(h2)
    l1 = _dot(hi, wr_ref[...])
    logits = l1[:, :LANES] + l1[:, LANES:] + _dot(lo, wr_ref[:, :LANES]) + br_ref[...]
    lane = lax.broadcasted_iota(jnp.int32, logits.shape, 1).astype(F32)
    neg = -jnp.inf
    big = float(LANES)
    is_g = lane < N_GROUPS
    gl = jnp.where(is_g, logits, neg)
    gm = jnp.max(gl, axis=-1, keepdims=True)
    gidx = jnp.min(jnp.where(gl == gm, lane, big), axis=-1, keepdims=True)
    gsum = jnp.sum(jnp.where(is_g, jnp.exp(jnp.where(is_g, logits, gm) - gm), 0.0), axis=-1, keepdims=True)
    g_p = 1.0 / gsum
    lo_l = N_GROUPS + EXPERTS_PER_GROUP * gidx
    in_grp = (lane >= lo_l) & (lane < lo_l + EXPERTS_PER_GROUP)
    el = jnp.where(in_grp, logits, neg)
    e1 = jnp.max(el, axis=-1, keepdims=True)
    i1 = jnp.min(jnp.where(el == e1, lane, big), axis=-1, keepdims=True)
    el2 = jnp.where(lane == i1, neg, el)
    e2 = jnp.max(el2, axis=-1, keepdims=True)
    i2 = jnp.min(jnp.where(el2 == e2, lane, big), axis=-1, keepdims=True)
    t = jnp.exp(e2 - e1)
    return lane, i1, i2, g_p / (1.0 + t), g_p * t / (1.0 + t)


def _place_in_tile(lane, i1, i2, tri_ref):
    onehot = jnp.where((lane == i1) | (lane == i2), 1.0, 0.0)
    before = _dot(tri_ref[...], onehot.astype(BF16))
    counts = jnp.sum(onehot, axis=0, keepdims=True)
    runs = jnp.floor((counts + (SEG_ALIGN - 1)) * (1.0 / SEG_ALIGN)) * SEG_ALIGN
    li = lax.broadcasted_iota(jnp.int32, (LANES, LANES), 0)
    lj = lax.broadcasted_iota(jnp.int32, (LANES, LANES), 1)
    upper = jnp.where(li < lj, 1.0, 0.0).astype(BF16)
    starts = _dot(jnp.broadcast_to(runs, (SUBLANES, LANES)).astype(BF16), upper)[0:1, :]
    p1 = jnp.sum(jnp.where(lane == i1, before + starts, 0.0), axis=-1, keepdims=True)
    p2 = jnp.sum(jnp.where(lane == i2, before + starts, 0.0), axis=-1, keepdims=True)
    return p1, p2, runs


def _unpack_bf16_pairs(u):
    lo = lax.bitcast_convert_type(lax.shift_left(u, 16), F32)
    hi = lax.bitcast_convert_type(u & jnp.int32(-65536), F32)
    return jnp.concatenate([lo, hi], axis=1).astype(BF16)


def _out_kernel(*refs, nl, has_ctx):
    if has_ctx:
        (a_lat, a_ctx, x_lat, x_ctx, m_ref, w_ref, wr_ref, br_ref, tri_ref,
         x1_ref, h2_ref, e_ref, k_ref, g_ref, c_ref) = refs
        is_lat = pl.program_id(0) < nl
    else:
        a_lat, x_lat, m_ref, w_ref, wr_ref, br_ref, tri_ref, x1_ref, h2_ref, e_ref, k_ref, g_ref, c_ref = refs
    picks = []
    for a, b in _row_halves(x1_ref.shape[0]):
        lhs = _load_rows(is_lat, a_lat, a_ctx, a, b) if has_ctx else a_lat[a:b, :]
        x = _load_rows(is_lat, x_lat, x_ctx, a, b) if has_ctx else x_lat[a:b, :]
        x1 = x + m_ref[0, 2:3, :] * _dot(lhs, w_ref[...])
        x1_ref[a:b, :] = x1
        h2 = _modulate(x1, m_ref[0, 3:4, :], m_ref[0, 4:5, :])
        h2_ref[a:b, :] = h2.astype(BF16)
        picks.append(_route(h2, wr_ref, br_ref))
    lane, i1, i2, ga, gb = (jnp.concatenate(v, axis=0) for v in zip(*picks))
    p1, p2, runs = _place_in_tile(lane, i1, i2, tri_ref)
    e_ref[...] = jnp.concatenate([i1, i2], axis=1).astype(jnp.int32) - N_GROUPS
    k_ref[...] = jnp.concatenate([p1, p2], axis=1).astype(jnp.int32)
    g_ref[...] = jnp.where(lane == 2, ga, jnp.where(lane == 3, gb, 0.0))
    c_ref[0] = jnp.broadcast_to(runs, c_ref.shape[1:])


def _mixer_out(a_lat, a_ctx, x_lat, x_ctx, mod, w_out, wr, br, geo, nt):
    t_rows, nl, tpb, n_batch = geo
    tm = ROW_TILE
    kdim = w_out.shape[0]
    has_ctx = a_ctx is not None
    mrow = lambda i: (jnp.where(i < nl, i // tpb, n_batch), 0, 0)
    lat = lambda w: pl.BlockSpec((tm, w), lambda i: (jnp.minimum(i, nl - 1), 0))
    ctx = lambda w: pl.BlockSpec((tm, w), lambda i: (jnp.maximum(i - nl, 0), 0))
    if has_ctx:
        in_specs = [lat(kdim), ctx(kdim), lat(D_MODEL), ctx(D_MODEL)]
        args = [a_lat, a_ctx, x_lat, x_ctx]
    else:
        in_specs = [lat(kdim), lat(D_MODEL)]
        args = [a_lat, x_lat]
    in_specs += [
        pl.BlockSpec((1, N_MOD, D_MODEL), mrow),
        pl.BlockSpec((kdim, D_MODEL), lambda i: (0, 0)),
        pl.BlockSpec((D_MODEL, 2 * LANES), lambda i: (0, 0)),
        pl.BlockSpec((1, LANES), lambda i: (0, 0)),
        pl.BlockSpec((tm, tm), lambda i: (0, 0)),
    ]
    tri = (jnp.arange(tm)[:, None] > jnp.arange(tm)[None, :]).astype(BF16)
    args += [mod, w_out, wr, br, tri]
    rows = nt * tm
    row_out = lambda w: pl.BlockSpec((tm, w), lambda i: (i, 0))
    return pl.pallas_call(
        functools.partial(_out_kernel, nl=nl, has_ctx=has_ctx),
        grid=(nt,),
        in_specs=in_specs,
        out_specs=[row_out(D_MODEL), row_out(D_MODEL), row_out(2), row_out(2), row_out(LANES),
                   pl.BlockSpec((1, SUBLANES, LANES), lambda i: (i, 0, 0))],
        out_shape=[jax.ShapeDtypeStruct((rows, D_MODEL), F32),
                   jax.ShapeDtypeStruct((rows, D_MODEL), BF16),
                   jax.ShapeDtypeStruct((rows, 2), jnp.int32),
                   jax.ShapeDtypeStruct((rows, 2), jnp.int32),
                   jax.ShapeDtypeStruct((rows, LANES), F32),
                   jax.ShapeDtypeStruct((nt, SUBLANES, LANES), F32)],
        compiler_params=_cparams(1),
        name="mixer_out_ctx" if has_ctx else "mixer_out",
    )(*args)


def _router_weights(w_group, b_group, w_expert, b_expert):
    w = jnp.concatenate([w_group, w_expert], axis=1)
    w = jnp.pad(w, ((0, 0), (0, LANES - w.shape[1])))
    hi = w.astype(BF16)
    lo = (w - hi.astype(F32)).astype(BF16)
    b = jnp.pad(jnp.concatenate([b_group, b_expert]), (0, LANES - N_GROUPS - N_EXPERTS))
    return jnp.concatenate([hi, lo], axis=1), b.reshape(1, LANES).astype(F32)


def _slot_tables(runs, n_tok):
    blk = EXPERT_BLOCK
    nt = runs.shape[0]
    run = runs[:, 0, N_GROUPS:N_GROUPS + N_EXPERTS].astype(jnp.int32)
    total = jnp.sum(run, axis=0)
    padded = (total + blk - 1) // blk * blk
    ends_p = jnp.cumsum(padded)
    base = (ends_p - padded)[None, :] + jnp.cumsum(run, axis=0) - run
    ends_t = jnp.cumsum(run, axis=1)
    shift = base - (ends_t - run)
    shift_rows = jnp.pad(shift.astype(F32), ((0, 0), (N_GROUPS, LANES - N_GROUPS - N_EXPERTS)))[:, None, :]
    chunk_row = jnp.arange(SORT_ROWS // SEG_ALIGN, dtype=jnp.int32) * SEG_ALIGN
    starts_t = ends_t - run
    owns = (starts_t[:, None, :] <= chunk_row[None, :, None]) & (chunk_row[None, :, None] < ends_t[:, None, :])
    chunk_shift = jnp.sum(jnp.where(owns, shift[:, None, :], 0), axis=-1)[:, None, :]
    n_chunks = ends_t[:, -1] // SEG_ALIGN
    n_blocks = -(-(2 * n_tok + nt * N_EXPERTS * (SEG_ALIGN - 1)) // blk) + N_EXPERTS
    first_slot = jnp.arange(n_blocks, dtype=jnp.int32) * blk
    block_e = jnp.minimum(jnp.sum(ends_p[None, :] <= first_slot[:, None], axis=1), N_EXPERTS - 1).astype(jnp.int32)
    n_valid = (ends_p[-1] // blk).astype(jnp.int32).reshape(1)
    pad_tab = jnp.concatenate([ends_p - padded + total, (padded - total) // SEG_ALIGN]).astype(jnp.int32)
    return n_chunks, chunk_shift, shift_rows, pad_tab, block_e, n_valid, n_blocks


def _dispatch_kernel(nchunk_ref, pad_ref, cshift_ref, h_ref, e_ref, pos_ref, srow_ref,
                     o_ref, dest_ref, perm_ref, sorted_ref, zero_ref, sem):
    tm = h_ref.shape[0]
    i = pl.program_id(0)
    pos = pos_ref[...].astype(F32)
    lane = lax.broadcasted_iota(jnp.int32, (tm, LANES), 1).astype(F32)

    e_lane = e_ref[...].astype(F32) + N_GROUPS
    dest = [jnp.sum(jnp.where(lane == e_lane[:, k:k + 1], srow_ref[0], 0.0), axis=-1, keepdims=True)
            + pos[:, k:k + 1] for k in range(2)]
    dest_ref[...] = jnp.concatenate(dest, axis=1).astype(jnp.int32)

    pos_row = [jnp.transpose(jnp.where(lane == 0.0, pos[:, k:k + 1], 0.0))[0:1, :] for k in range(2)]
    for s0 in range(0, SORT_ROWS, SORT_BLOCK):
        s = (lax.broadcasted_iota(jnp.int32, (SORT_BLOCK, tm), 0) + s0).astype(F32)
        perm_ref[s0:s0 + SORT_BLOCK, :] = jnp.where((s == pos_row[0]) | (s == pos_row[1]), 1.0, 0.0).astype(BF16)
    half = D_MODEL // 2
    lo = lax.bitcast_convert_type(_dot(perm_ref[...], h_ref[:, :half]), jnp.int32)
    hi = lax.bitcast_convert_type(_dot(perm_ref[...], h_ref[:, half:]), jnp.int32)
    sorted_ref[...] = lax.shift_right_logical(lo, 16) | (hi & jnp.int32(-65536))

    def chunk_copy(c):
        r = pl.multiple_of(c * SEG_ALIGN, SEG_ALIGN)
        d = pl.multiple_of(r + cshift_ref[0, 0, c], SEG_ALIGN)
        return pltpu.make_async_copy(sorted_ref.at[pl.ds(r, SEG_ALIGN), :], o_ref.at[pl.ds(d, SEG_ALIGN), :], sem)

    def start(c, carry):
        chunk_copy(c).start()
        return carry

    n = nchunk_ref[i]

    def start_pair(c, carry):
        chunk_copy(2 * c).start()
        chunk_copy(2 * c + 1).start()
        return carry

    lax.fori_loop(0, n // 2, start_pair, 0)
    lax.fori_loop(2 * (n // 2), n, start, 0)

    def wait_rows(rows):
        def body(c, carry):
            pltpu.make_async_copy(sorted_ref.at[pl.ds(0, rows), :], o_ref.at[pl.ds(0, rows), :], sem).wait()
            return carry
        return body

    lax.fori_loop(0, n // WAIT_GROUP, wait_rows(WAIT_GROUP * SEG_ALIGN), 0)
    lax.fori_loop(0, n % WAIT_GROUP, wait_rows(SEG_ALIGN), 0)

    @pl.when(i == pl.num_programs(0) - 1)
    def _():
        zero_ref[...] = jnp.zeros_like(zero_ref)

        def for_each_pad_chunk(fn):
            for e in range(N_EXPERTS):
                first = pad_ref[e]

                def body(c, carry):
                    d = pl.multiple_of(first + c * SEG_ALIGN, SEG_ALIGN)
                    fn(pltpu.make_async_copy(zero_ref, o_ref.at[pl.ds(d, SEG_ALIGN), :], sem))
                    return carry

                lax.fori_loop(0, pad_ref[N_EXPERTS + e], body, 0)

        for_each_pad_chunk(lambda cp: cp.start())
        for_each_pad_chunk(lambda cp: cp.wait())


def _dispatch(h2, e_ids, pos, n_chunks, chunk_shift, shift_rows, pad_tab, n_slots):
    n_tok = h2.shape[0]
    tm = ROW_TILE
    nt = n_tok // tm
    width = D_MODEL // 2
    return pl.pallas_call(
        _dispatch_kernel,
        grid_spec=pltpu.PrefetchScalarGridSpec(
            num_scalar_prefetch=2,
            grid=(nt,),
            in_specs=[pl.BlockSpec((1, 1, SORT_ROWS // SEG_ALIGN), lambda i, n, p: (i, 0, 0), memory_space=pltpu.SMEM),
                      pl.BlockSpec((tm, D_MODEL), lambda i, n, p: (i, 0)),
                      pl.BlockSpec((tm, 2), lambda i, n, p: (i, 0)),
                      pl.BlockSpec((tm, 2), lambda i, n, p: (i, 0)),
                      pl.BlockSpec((1, 1, LANES), lambda i, n, p: (i, 0, 0))],
            out_specs=[pl.BlockSpec(memory_space=pl.ANY),
                       pl.BlockSpec((tm, 2), lambda i, n, p: (i, 0))],
            scratch_shapes=[pltpu.VMEM((SORT_ROWS, tm), BF16), pltpu.VMEM((SORT_ROWS, width), jnp.int32),
                            pltpu.VMEM((SEG_ALIGN, width), jnp.int32), pltpu.SemaphoreType.DMA(())],
        ),
        out_shape=[jax.ShapeDtypeStruct((n_slots, width), jnp.int32),
                   jax.ShapeDtypeStruct((n_tok, 2), jnp.int32)],
        compiler_params=_cparams(1),
        name="dispatch",
    )(n_chunks, pad_tab, chunk_shift, h2, e_ids, pos, shift_rows)


def _expert_kernel(be_ref, nv_ref, x_ref, wg_ref, wu_ref, wd_ref, o_ref, wg_b, wu_b, wd_b):
    i = pl.program_id(0)
    valid = i < nv_ref[0]
    new_expert = (i == 0) | (be_ref[i] != be_ref[jnp.maximum(i - 1, 0)])

    @pl.when(valid & new_expert)
    def _():
        wg_b[...] = wg_ref[0, 0].astype(BF16)
        wu_b[...] = wu_ref[0, 0].astype(BF16)
        wd_b[...] = wd_ref[0, 0].astype(BF16)

    @pl.when(valid)
    def _():
        x = _unpack_bf16_pairs(x_ref[...])
        mid = (_silu(_dot(x, wg_b[...])) * _dot(x, wu_b[...])).astype(BF16)
        for a, b in _row_halves(o_ref.shape[1]):
            o_ref[:, a:b] = _dot(mid, wd_b[:, a:b]).astype(BF16)

    @pl.when(jnp.logical_not(valid))
    def _():
        o_ref[...] = jnp.zeros_like(o_ref)


def _expert_ffn(xs_sorted, block_e, n_valid, w_gate, w_up, w_down, layer):
    n_slots = xs_sorted.shape[0]
    blk = EXPERT_BLOCK
    wspec = lambda r, c: pl.BlockSpec((1, 1, r, c), lambda i, be, nv: (layer, be[i], 0, 0))
    return pl.pallas_call(
        _expert_kernel,
        grid_spec=pltpu.PrefetchScalarGridSpec(
            num_scalar_prefetch=2,
            grid=(n_slots // blk,),
            in_specs=[
                pl.BlockSpec((blk, D_MODEL // 2), lambda i, be, nv: (i, 0)),
                wspec(D_MODEL, EXPERT_HIDDEN), wspec(D_MODEL, EXPERT_HIDDEN), wspec(EXPERT_HIDDEN, D_MODEL),
            ],
            out_specs=pl.BlockSpec((blk, D_MODEL), lambda i, be, nv: (i, 0)),
            scratch_shapes=[pltpu.VMEM((D_MODEL, EXPERT_HIDDEN), BF16),
                            pltpu.VMEM((D_MODEL, EXPERT_HIDDEN), BF16),
                            pltpu.VMEM((EXPERT_HIDDEN, D_MODEL), BF16)],
        ),
        out_shape=jax.ShapeDtypeStruct((n_slots, D_MODEL), BF16),
        compiler_params=_cparams(1),
        name="expert_ffn",
    )(block_e, n_valid, xs_sorted, w_gate, w_up, w_down)


def _moe(h2, e_ids, pos, runs, n_tok, w_gate, w_up, w_down, layer):
    n_chunks, chunk_shift, shift_rows, pad_tab, block_e, n_valid, n_blocks = _slot_tables(runs, n_tok)
    xs_sorted, dest = _dispatch(h2, e_ids, pos, n_chunks, chunk_shift, shift_rows, pad_tab, n_blocks * EXPERT_BLOCK)
    ys = _expert_ffn(xs_sorted, block_e, n_valid, w_gate, w_up, w_down, layer)
    take = lambda idx: ys.at[idx].get(mode='promise_in_bounds', unique_indices=True)
    return take(dest[:, 0]), take(dest[:, 1])


def _final_kernel(x1_ref, ya_ref, yb_ref, r_ref, m_ref, gain_ref, o_ref):
    r = r_ref[...]
    f = r[:, 2:3] * ya_ref[...].astype(F32) + r[:, 3:4] * yb_ref[...].astype(F32)
    x2 = x1_ref[...] + m_ref[0, 5:6, :] * f
    ms = jnp.mean(x2 * x2, axis=-1, keepdims=True)
    o_ref[...] = x2 * lax.rsqrt(ms + NORM_EPS) * gain_ref[...]


def _final(x1, ya, yb, route, mod, gain, geo, nt):
    t_rows, nl, tpb, n_batch = geo
    tm = ROW_TILE
    rows = lambda w: pl.BlockSpec((tm, w), lambda i: (i, 0))
    return pl.pallas_call(
        _final_kernel,
        grid=(nt,),
        in_specs=[rows(D_MODEL), rows(D_MODEL), rows(D_MODEL), rows(LANES),
                  pl.BlockSpec((1, N_MOD, D_MODEL), lambda i: (jnp.where(i < nl, i // tpb, n_batch), 0, 0)),
                  pl.BlockSpec((1, D_MODEL), lambda i: (0, 0))],
        out_specs=rows(D_MODEL),
        out_shape=jax.ShapeDtypeStruct((nt * tm, D_MODEL), F32),
        compiler_params=_cparams(1),
        name="combine_final",
    )(x1, ya, yb, route, mod, gain.reshape(1, D_MODEL))


def _proj_att_kernel(x1_ref, ya_ref, yb_ref, r_ref, m0_ref, m_ref, w_ref,
                     qc_ref, qa_ref, qb_ref, kc_ref, ka_ref, kb_ref, o_ref, x2_ref, h_ref):
    for a, b in _row_halves(h_ref.shape[0]):
        r = r_ref[a:b, :]
        f = r[:, 2:3] * ya_ref[a:b, :].astype(F32) + r[:, 3:4] * yb_ref[a:b, :].astype(F32)
        x2 = x1_ref[a:b, :] + m0_ref[0, 5:6, :] * f
        x2_ref[a:b, :] = x2
        h_ref[a:b, :] = _modulate(x2, m_ref[0, 0:1, :], m_ref[0, 1:2, :]).astype(BF16)

    ones = jnp.ones((LANES, LANES), BF16)

    def normed_rope(xs, c_ref, a_ref, b_ref):
        rs = lax.rsqrt(_dot((xs * xs).astype(BF16), ones) * (1.0 / ATT_HEAD_DIM) + NORM_EPS)
        r = (xs * c_ref[...] + pltpu.roll(xs, LANES - ATT_HEAD_DIM // 4, 1) * a_ref[...]
             + pltpu.roll(xs, ATT_HEAD_DIM // 4, 1) * b_ref[...])
        return (r * rs).astype(BF16)

    gw = PROJ_GROUP
    for c0 in range(0, ATT_IN, gw):
        acc = _dot(h_ref[...], w_ref[:, c0:c0 + gw])
        for g in range(gw // LANES):
            col = c0 + g * LANES
            xs = acc[:, g * LANES:(g + 1) * LANES]
            if col < ATT_QW:
                o_ref[:, col:col + LANES] = normed_rope(xs, qc_ref, qa_ref, qb_ref)
            elif col < ATT_QW + ATT_KW:
                o_ref[:, col:col + LANES] = normed_rope(xs, kc_ref, ka_ref, kb_ref)
            else:
                o_ref[:, col:col + LANES] = xs.astype(BF16)


def _proj_att(x1, ya, yb, route, mod_prev, mod, w_qkv, q_tabs, k_tabs, geo):
    t_rows, nl, tpb, n_batch = geo
    tm = ROW_TILE
    nt = t_rows // tm
    rope = pl.BlockSpec((tm, ATT_HEAD_DIM), lambda i: (jnp.where(i < nl, i % tpb, tpb), 0))
    rows = lambda w: pl.BlockSpec((tm, w), lambda i: (i, 0))
    mrow = pl.BlockSpec((1, N_MOD, D_MODEL), lambda i: (jnp.where(i < nl, i // tpb, n_batch), 0, 0))
    return pl.pallas_call(
        _proj_att_kernel,
        grid=(nt,),
        in_specs=[
            rows(D_MODEL), rows(D_MODEL), rows(D_MODEL), rows(LANES), mrow, mrow,
            pl.BlockSpec((D_MODEL, ATT_IN), lambda i: (0, 0)),
            rope, rope, rope, rope, rope, rope,
        ],
        out_specs=[rows(ATT_IN), rows(D_MODEL)],
        out_shape=[jax.ShapeDtypeStruct((t_rows, ATT_IN), BF16),
                   jax.ShapeDtypeStruct((t_rows, D_MODEL), F32)],
        scratch_shapes=[pltpu.VMEM((tm, D_MODEL), BF16)],
        compiler_params=_cparams(1),
        name="proj_att",
    )(x1, ya, yb, route, mod_prev, mod, w_qkv, *q_tabs, *k_tabs)


def _att_kernel(q_ref, kl_ref, vl_ref, kc_ref, vc_ref, o_ref, k_all, v_ext, s_ref, p_ref):
    tq = q_ref.shape[0]
    seq, ctx_len = kl_ref.shape[0], kc_ref.shape[0]

    @pl.when(pl.program_id(2) == 0)
    def _():
        k_all[0:seq, :] = kl_ref[...]
        k_all[seq:seq + ctx_len, :] = kc_ref[...]
        v_ext[0:seq, 0:LANES] = vl_ref[...]
        v_ext[seq:seq + ctx_len, 0:LANES] = vc_ref[...]
        v_ext[:, LANES:] = jnp.ones((seq + ctx_len, LANES), BF16)

    rows = ATT_SOFTMAX_ROWS

    def scores(g):
        s_ref[g * tq:(g + 1) * tq, :] = _dot_nt(q_ref[:, g * LANES:(g + 1) * LANES], k_all[...])

    def softmax(g):
        for r0 in range(g * tq, (g + 1) * tq, rows):
            m = jnp.max(s_ref[r0:r0 + rows, :], axis=-1, keepdims=True)
            p_ref[r0:r0 + rows, :] = jnp.exp2(s_ref[r0:r0 + rows, :] - m).astype(BF16)

    def weighted_values(g):
        oe = _dot(p_ref[g * tq:(g + 1) * tq, :], v_ext[...])
        o_ref[:, g * LANES:(g + 1) * LANES] = (oe[:, :LANES] / oe[:, LANES:]).astype(BF16)

    scores(0)
    scores(1)
    for g in range(ATT_GROUP):
        if g + 2 < ATT_GROUP:
            scores(g + 2)
        softmax(g)
        weighted_values(g)


def _attention(p, n_batch, seq, ctx_len):
    tq = ATT_Q_TILE
    nq = seq // tq
    cb = n_batch * seq // ctx_len
    ko, vo = ATT_QW // LANES, (ATT_QW + ATT_KW) // LANES
    gw = ATT_GROUP * ATT_HEAD_DIM
    n_keys = seq + ctx_len
    return pl.pallas_call(
        _att_kernel,
        grid=(n_batch, ATT_KV_HEADS, nq),
        in_specs=[
            pl.BlockSpec((tq, gw), lambda b, h, t: (b * nq + t, h)),
            pl.BlockSpec((seq, LANES), lambda b, h, t: (b, ko + h)),
            pl.BlockSpec((seq, LANES), lambda b, h, t: (b, vo + h)),
            pl.BlockSpec((ctx_len, LANES), lambda b, h, t: (cb + b, ko + h)),
            pl.BlockSpec((ctx_len, LANES), lambda b, h, t: (cb + b, vo + h)),
        ],
        out_specs=pl.BlockSpec((tq, gw), lambda b, h, t: (b * nq + t, h)),
        out_shape=jax.ShapeDtypeStruct((n_batch * seq, ATT_QW), BF16),
        scratch_shapes=[pltpu.VMEM((n_keys, LANES), BF16), pltpu.VMEM((n_keys, 2 * LANES), BF16),
                        pltpu.VMEM((ATT_GROUP * tq, n_keys), F32), pltpu.VMEM((ATT_GROUP * tq, n_keys), BF16)],
        compiler_params=_cparams(3),
        name="attention",
    )(p, p, p, p, p)


def kernel(x, c, ctx, c_ctx, w_mod, b_mod, ret_w_in, ret_w_out, ret_log_decay_fwd, ret_log_decay_bwd, att_w_qkv, att_w_o, att_q_gain, att_k_gain, moe_w_group, moe_b_group, moe_w_expert, moe_b_expert, moe_w_gate, moe_w_up, moe_w_down, final_norm_gain):
    n_batch, seq, d = x.shape
    ctx_len = ctx.shape[1]
    tm = ROW_TILE
    assert d == D_MODEL and w_mod.shape[0] == DEPTH == 2
    assert seq % tm == 0 and (n_batch * ctx_len) % tm == 0 and seq % GRID_W == 0
    assert seq % RET_CHUNK == 0 and ctx_len % RET_CHUNK == 0 and (n_batch * seq) % ctx_len == 0
    n_lat = n_batch * seq
    t_rows = n_lat + n_batch * ctx_len
    nl = n_lat // tm
    geo = (t_rows, nl, seq // tm, n_batch)
    nt = t_rows // tm

    x_lat, x_ctx = x.reshape(n_lat, d), ctx.reshape(n_batch * ctx_len, d)
    pad_rows = -(n_batch + 1) % MOD_ROWS_PAD
    c_rows = jnp.concatenate([c, c_ctx[None, :], jnp.zeros((pad_rows, d), F32)], axis=0)
    mod = _mod_vectors(c_rows, w_mod, b_mod)

    cos, sin = _ret_rope_tables(seq, tm)
    p = _proj_ret(x_lat, x_ctx, mod[0], ret_w_in[0].astype(BF16), cos, sin, geo)
    lg = jnp.stack([ret_log_decay_fwd[0], ret_log_decay_bwd[0]]).astype(F32)
    y_lat, y_ctx = _retention(p, lg, n_batch, seq, ctx_len)
    wr, br = _router_weights(moe_w_group[0], moe_b_group[0], moe_w_expert[0], moe_b_expert[0])
    x1, h2, e_ids, pos, route, runs = _mixer_out(
        y_lat, y_ctx, x_lat, x_ctx, mod[0], ret_w_out[0].astype(BF16), wr, br, geo, nt)
    ya, yb = _moe(h2, e_ids, pos, runs, t_rows, moe_w_gate, moe_w_up, moe_w_down, 0)

    q_tabs = _att_rope_tables(seq, tm, att_q_gain[0], ATT_HEAD_DIM ** -0.5 * LOG2_E)
    k_tabs = _att_rope_tables(seq, tm, att_k_gain[0], 1.0)
    p, xs = _proj_att(x1, ya, yb, route, mod[0], mod[1], att_w_qkv[0].astype(BF16), q_tabs, k_tabs, geo)
    o = _attention(p, n_batch, seq, ctx_len)
    wr, br = _router_weights(moe_w_group[1], moe_b_group[1], moe_w_expert[1], moe_b_expert[1])
    x1, h2, e_ids, pos, route, runs = _mixer_out(
        o, None, xs, None, mod[1], att_w_o[0].astype(BF16), wr, br, geo, nl)
    ya, yb = _moe(h2, e_ids, pos, runs, n_lat, moe_w_gate, moe_w_up, moe_w_down, 1)
    out = _final(x1, ya, yb, route, mod[1], final_norm_gain, geo, nl)
    return out.reshape(n_batch, seq, d)
```

```python
import functools

import jax
import jax.numpy as jnp
from jax import lax
from jax.experimental import pallas as pl
from jax.experimental.pallas import tpu as pltpu

F32 = jnp.float32
BF16 = jnp.bfloat16

D_MODEL = 1024
DEPTH = 2
GRID_W = 64
N_MOD = 6
NORM_EPS = 1e-6
HEAD_NORM_EPS = 1e-5
ROPE_THETA = 10000.0

RET_HEADS = 4
RET_DK = D_MODEL // RET_HEADS
RET_DV = 2 * RET_DK
RET_QK = RET_HEADS * RET_DK
RET_V = RET_HEADS * RET_DV
RET_IN = 2 * RET_QK + 2 * RET_V

ATT_HEAD_DIM = 128
ATT_Q_HEADS = D_MODEL // ATT_HEAD_DIM
ATT_KV_HEADS = 2
ATT_GROUP = ATT_Q_HEADS // ATT_KV_HEADS
ATT_QW = ATT_Q_HEADS * ATT_HEAD_DIM
ATT_KW = ATT_KV_HEADS * ATT_HEAD_DIM
ATT_IN = ATT_QW + 2 * ATT_KW

N_GROUPS = 4
EXPERTS_PER_GROUP = 8
N_EXPERTS = N_GROUPS * EXPERTS_PER_GROUP
EXPERT_HIDDEN = D_MODEL // 2

LANES = 128
SUBLANES = 8
ROW_TILE = 512
PROJ_GROUP = 512
RET_CHUNK = 256
RET_PAIR = 2
RET_UNROLL_FWD = 4
RET_UNROLL_BWD = 2
ATT_Q_TILE = 512
ATT_SOFTMAX_ROWS = 64
LOG2_E = 1.4426950408889634
EXPERT_BLOCK = 1024
SEG_ALIGN = 8
SORT_ROWS = 2 * ROW_TILE + 256
SORT_BLOCK = 256
WAIT_GROUP = 8
MOD_ROWS_PAD = 8
VMEM_LIMIT = 56 * 1024 * 1024


def _cparams(n_axes):
    return pltpu.CompilerParams(
        dimension_semantics=("arbitrary",) * n_axes, vmem_limit_bytes=VMEM_LIMIT)


def _silu(v):
    return v * (1.0 / (1.0 + jnp.exp(-v)))


def _modulate(x, shift, scale):
    ms = jnp.mean(x * x, axis=-1, keepdims=True)
    return x * lax.rsqrt(ms + NORM_EPS) * (1.0 + scale) + shift


def _split_bf16(v):
    hi = v.astype(BF16)
    lo = (v - hi.astype(F32)).astype(BF16)
    return hi, lo


def _dot(a, b):
    return jnp.dot(a, b, preferred_element_type=F32)


def _dot_nt(a, b):
    return lax.dot_general(a, b, (((1,), (1,)), ((), ())), preferred_element_type=F32)


def _row_halves(n):
    return ((0, n // 2), (n // 2, n))


def _mod_kernel(c_ref, w_ref, b_ref, o_ref):
    a_hi, a_lo = _split_bf16(_silu(c_ref[...]))
    w_hi, w_lo = _split_bf16(w_ref[0])
    acc = _dot(a_hi, w_hi) + _dot(a_lo, w_hi) + _dot(a_hi, w_lo)
    o_ref[0] = acc + b_ref[0]


def _mod_vectors(c_rows, w_mod, b_mod):
    rows = c_rows.shape[0]
    n = w_mod.shape[-1]
    tn = 1024
    out = pl.pallas_call(
        _mod_kernel,
        grid=(DEPTH, n // tn),
        in_specs=[
            pl.BlockSpec((rows, D_MODEL), lambda l, j: (0, 0)),
            pl.BlockSpec((1, D_MODEL, tn), lambda l, j: (l, 0, j)),
            pl.BlockSpec((1, 1, tn), lambda l, j: (l, 0, j)),
        ],
        out_specs=pl.BlockSpec((1, rows, tn), lambda l, j: (l, 0, j)),
        out_shape=jax.ShapeDtypeStruct((DEPTH, rows, n), F32),
        compiler_params=_cparams(2),
        name="mod_vectors",
    )(c_rows, w_mod, b_mod.reshape(DEPTH, 1, n))
    return out.reshape(DEPTH, rows, N_MOD, D_MODEL)


def _rope_angles(n, head_dim):
    rows = n // GRID_W
    r = jnp.repeat(jnp.arange(rows), GRID_W).astype(F32)
    col = jnp.tile(jnp.arange(GRID_W), rows).astype(F32)
    nf = head_dim // 4
    inv = ROPE_THETA ** (-jnp.arange(nf, dtype=F32) / nf)
    ar = r[:, None] * inv
    ac = col[:, None] * inv
    return jnp.concatenate([ar, ar, ac, ac], axis=-1)


def _ret_rope_tables(n, pad):
    ang = _rope_angles(n, RET_DK)
    lane = jnp.arange(RET_DK) % LANES
    sin = jnp.where(lane < LANES // 2, -jnp.sin(ang), jnp.sin(ang))
    cos = jnp.concatenate([jnp.cos(ang), jnp.ones((pad, RET_DK), F32)], axis=0)
    sin = jnp.concatenate([sin, jnp.zeros((pad, RET_DK), F32)], axis=0)
    return cos, sin


def _att_rope_tables(n, pad, gain, scale):
    ang = _rope_angles(n, ATT_HEAD_DIM)
    quarter = ATT_HEAD_DIM // 4
    first = jnp.arange(ATT_HEAD_DIM) % (2 * quarter) < quarter
    sin = jnp.sin(ang)
    c = jnp.cos(ang) * gain * scale
    a = jnp.where(first, -sin, 0.0) * jnp.roll(gain, ATT_HEAD_DIM - quarter) * scale
    b = jnp.where(first, 0.0, sin) * jnp.roll(gain, quarter) * scale
    ident = jnp.broadcast_to(gain * scale, (pad, ATT_HEAD_DIM))
    zeros = jnp.zeros((pad, ATT_HEAD_DIM), F32)
    return (jnp.concatenate([c, ident], axis=0), jnp.concatenate([a, zeros], axis=0),
            jnp.concatenate([b, zeros], axis=0))


def _load_rows(is_lat, lat_ref, ctx_ref, a, b):
    return jnp.where(is_lat, lat_ref[a:b, :], ctx_ref[a:b, :])


def _proj_ret_kernel(xl_ref, xc_ref, m_ref, w_ref, cos_ref, sin_ref, o_ref, h_ref, *, nl):
    is_lat = pl.program_id(0) < nl
    for a, b in _row_halves(h_ref.shape[0]):
        x = _load_rows(is_lat, xl_ref, xc_ref, a, b)
        h_ref[a:b, :] = _modulate(x, m_ref[0, 0:1, :], m_ref[0, 1:2, :]).astype(BF16)

    gw = PROJ_GROUP
    for c0 in range(0, RET_IN, gw):
        acc = _dot(h_ref[...], w_ref[:, c0:c0 + gw])
        if c0 < 2 * RET_QK:
            scale = 1.0 if c0 < RET_QK else RET_DK ** -0.5
            for g in range(gw // LANES):
                xs = acc[:, g * LANES:(g + 1) * LANES]
                t = ((c0 // LANES + g) % 2) * LANES
                r = xs * cos_ref[:, t:t + LANES] + pltpu.roll(xs, LANES // 2, 1) * sin_ref[:, t:t + LANES]
                o_ref[:, c0 + g * LANES:c0 + (g + 1) * LANES] = (r * scale).astype(BF16)
        elif c0 < 2 * RET_QK + RET_V:
            o_ref[:, c0:c0 + gw] = acc.astype(BF16)
        else:
            o_ref[:, c0:c0 + gw] = _silu(acc).astype(BF16)


def _proj_ret(x_lat, x_ctx, mod, w_in, cos, sin, geo):
    t_rows, nl, tpb, n_batch = geo
    tm = ROW_TILE
    nt = t_rows // tm
    return pl.pallas_call(
        functools.partial(_proj_ret_kernel, nl=nl),
        grid=(nt,),
        in_specs=[
            pl.BlockSpec((tm, D_MODEL), lambda i: (jnp.minimum(i, nl - 1), 0)),
            pl.BlockSpec((tm, D_MODEL), lambda i: (jnp.maximum(i - nl, 0), 0)),
            pl.BlockSpec((1, N_MOD, D_MODEL), lambda i: (jnp.where(i < nl, i // tpb, n_batch), 0, 0)),
            pl.BlockSpec((D_MODEL, RET_IN), lambda i: (0, 0)),
            pl.BlockSpec((tm, RET_DK), lambda i: (jnp.where(i < nl, i % tpb, tpb), 0)),
            pl.BlockSpec((tm, RET_DK), lambda i: (jnp.where(i < nl, i % tpb, tpb), 0)),
        ],
        out_specs=pl.BlockSpec((tm, RET_IN), lambda i: (i, 0)),
        out_shape=jax.ShapeDtypeStruct((t_rows, RET_IN), BF16),
        scratch_shapes=[pltpu.VMEM((tm, D_MODEL), BF16)],
        compiler_params=_cparams(1),
        name="proj_ret",
    )(x_lat, x_ctx, mod, w_in, cos, sin)


def _ret_kernel(lg_ref, ql, kl, vl, gl, qc, kc, vc, gc, ol, oc,
                sf_ref, sb_ref, y_ref, mask_ref):
    c = RET_CHUNK
    heads = [pl.program_id(0) * RET_PAIR + j for j in range(RET_PAIR)]
    lgf = [lg_ref[0, hd] for hd in heads]
    lgb = [lg_ref[1, hd] for hd in heads]

    @pl.when(pl.program_id(1) == 0)
    def _():
        ii = lax.broadcasted_iota(jnp.int32, (c, c), 0).astype(F32)
        jj = lax.broadcasted_iota(jnp.int32, (c, c), 1).astype(F32)
        d = ii - jj
        for j in range(RET_PAIR):
            mask_ref[j] = (jnp.where(d >= 0, jnp.exp(lgf[j] * jnp.maximum(d, 0.0)), 0.0)
                           + jnp.where(d <= 0, jnp.exp(lgb[j] * jnp.maximum(-d, 0.0)), 0.0))

    pos = lax.broadcasted_iota(jnp.int32, (c, 1), 0).astype(F32)
    xi_f = [jnp.exp(l * (pos + 1.0)) for l in lgf]
    zeta_f = [jnp.exp(l * (c - 1.0 - pos)) for l in lgf]
    xi_b = [jnp.exp(l * (c - pos)) for l in lgb]
    zeta_b = [jnp.exp(l * pos) for l in lgb]
    dec_f = [jnp.exp(jnp.full((1, RET_DV), l * c, F32)) for l in lgf]
    dec_b = [jnp.exp(jnp.full((1, RET_DV), l * c, F32)) for l in lgb]

    def state_update(s_ref, j, k, v, zeta, dec):
        kz = (k.astype(F32) * zeta).T.astype(BF16)
        s_ref[j] = s_ref[j] * dec + _dot(kz, v)

    def chunk(ref, r0, j, width):
        return ref[pl.ds(r0, c), j * width:(j + 1) * width]

    def run(q_ref, k_ref, v_ref, g_ref, o_ref, n):
        def fwd(t, carry):
            r0 = pl.multiple_of(t * c, c)
            for j in range(RET_PAIR):
                q, k, v = chunk(q_ref, r0, j, RET_DK), chunk(k_ref, r0, j, RET_DK), chunk(v_ref, r0, j, RET_DV)
                a = (_dot_nt(q, k) * mask_ref[j]).astype(BF16)
                cross = _dot(q, sf_ref[j].astype(BF16)) * xi_f[j]
                y_ref[pl.ds(r0, c), j * RET_DV:(j + 1) * RET_DV] = _dot(a, v) + cross
                state_update(sf_ref, j, k, v, zeta_f[j], dec_f[j])
            return carry

        lax.fori_loop(0, n, fwd, 0, unroll=min(n, RET_UNROLL_FWD))

        def bwd(t, carry):
            r0 = pl.multiple_of((n - 1 - t) * c, c)
            for j in range(RET_PAIR):
                q, k, v = chunk(q_ref, r0, j, RET_DK), chunk(k_ref, r0, j, RET_DK), chunk(v_ref, r0, j, RET_DV)
                y = chunk(y_ref, r0, j, RET_DV) + _dot(q, sb_ref[j].astype(BF16)) * xi_b[j]
                mu = jnp.mean(y, axis=-1, keepdims=True)
                yc = y - mu
                var = jnp.mean(yc * yc, axis=-1, keepdims=True)
                yn = yc * lax.rsqrt(var + HEAD_NORM_EPS)
                o_ref[pl.ds(r0, c), j * RET_DV:(j + 1) * RET_DV] = (
                    yn * chunk(g_ref, r0, j, RET_DV).astype(F32)).astype(BF16)
                state_update(sb_ref, j, k, v, zeta_b[j], dec_b[j])
            return carry

        lax.fori_loop(0, n, bwd, 0, unroll=min(n, RET_UNROLL_BWD))

    sf_ref[...] = jnp.zeros_like(sf_ref)
    sb_ref[...] = jnp.zeros_like(sb_ref)
    run(qc, kc, vc, gc, oc, qc.shape[0] // c)
    run(ql, kl, vl, gl, ol, ql.shape[0] // c)


def _retention(p, lg, n_batch, seq, ctx_len):
    cb = n_batch * seq // ctx_len
    kw, vw = RET_PAIR * RET_DK, RET_PAIR * RET_DV
    qo, ko = 0, RET_QK // kw
    vo, go = 2 * RET_QK // vw, (2 * RET_QK + RET_V) // vw
    lat = lambda w, off: pl.BlockSpec((seq, w), lambda h, b, lg_: (b, off + h))
    ctx = lambda w, off: pl.BlockSpec((ctx_len, w), lambda h, b, lg_: (cb + b, off + h))
    return pl.pallas_call(
        _ret_kernel,
        grid_spec=pltpu.PrefetchScalarGridSpec(
            num_scalar_prefetch=1,
            grid=(RET_HEADS // RET_PAIR, n_batch),
            in_specs=[lat(kw, qo), lat(kw, ko), lat(vw, vo), lat(vw, go),
                      ctx(kw, qo), ctx(kw, ko), ctx(vw, vo), ctx(vw, go)],
            out_specs=[pl.BlockSpec((seq, vw), lambda h, b, lg_: (b, h)),
                       pl.BlockSpec((ctx_len, vw), lambda h, b, lg_: (b, h))],
            scratch_shapes=[pltpu.VMEM((RET_PAIR, RET_DK, RET_DV), F32), pltpu.VMEM((RET_PAIR, RET_DK, RET_DV), F32),
                            pltpu.VMEM((seq, vw), F32), pltpu.VMEM((RET_PAIR, RET_CHUNK, RET_CHUNK), F32)],
        ),
        out_shape=[jax.ShapeDtypeStruct((n_batch * seq, RET_V), BF16),
                   jax.ShapeDtypeStruct((n_batch * ctx_len, RET_V), BF16)],
        compiler_params=_cparams(2),
        name="retention",
    )(lg, p, p, p, p, p, p, p, p)


def _route(h2, wr_ref, br_ref):
    hi, lo = _split_bf16(h2)
    l1 = _dot(hi, wr_ref[...])
    logits = l1[:, :LANES] + l1[:, LANES:] + _dot(lo, wr_ref[:, :LANES]) + br_ref[...]
    lane = lax.broadcasted_iota(jnp.int32, logits.shape, 1).astype(F32)
    neg = -jnp.inf
    big = float(LANES)
    is_g = lane < N_GROUPS
    gl = jnp.where(is_g, logits, neg)
    gm = jnp.max(gl, axis=-1, keepdims=True)
    gidx = jnp.min(jnp.where(gl == gm, lane, big), axis=-1, keepdims=True)
    gsum = jnp.sum(jnp.where(is_g, jnp.exp(jnp.where(is_g, logits, gm) - gm), 0.0), axis=-1, keepdims=True)
    g_p = 1.0 / gsum
    lo_l = N_GROUPS + EXPERTS_PER_GROUP * gidx
    in_grp = (lane >= lo_l) & (lane < lo_l + EXPERTS_PER_GROUP)
    el = jnp.where(in_grp, logits, neg)
    e1 = jnp.max(el, axis=-1, keepdims=True)
    i1 = jnp.min(jnp.where(el == e1, lane, big), axis=-1, keepdims=True)
    el2 = jnp.where(lane == i1, neg, el)
    e2 = jnp.max(el2, axis=-1, keepdims=True)
    i2 = jnp.min(jnp.where(el2 == e2, lane, big), axis=-1, keepdims=True)
    t = jnp.exp(e2 - e1)
    return lane, i1, i2, g_p / (1.0 + t), g_p * t / (1.0 + t)


def _place_in_tile(lane, i1, i2, tri_ref):
    onehot = jnp.where((lane == i1) | (lane == i2), 1.0, 0.0)
    before = _dot(tri_ref[...], onehot.astype(BF16))
    counts = jnp.sum(onehot, axis=0, keepdims=True)
    runs = jnp.floor((counts + (SEG_ALIGN - 1)) * (1.0 / SEG_ALIGN)) * SEG_ALIGN
    li = lax.broadcasted_iota(jnp.int32, (LANES, LANES), 0)
    lj = lax.broadcasted_iota(jnp.int32, (LANES, LANES), 1)
    upper = jnp.where(li < lj, 1.0, 0.0).astype(BF16)
    starts = _dot(jnp.broadcast_to(runs, (SUBLANES, LANES)).astype(BF16), upper)[0:1, :]
    p1 = jnp.sum(jnp.where(lane == i1, before + starts, 0.0), axis=-1, keepdims=True)
    p2 = jnp.sum(jnp.where(lane == i2, before + starts, 0.0), axis=-1, keepdims=True)
    return p1, p2, runs


def _unpack_bf16_pairs(u):
    lo = lax.bitcast_convert_type(lax.shift_left(u, 16), F32)
    hi = lax.bitcast_convert_type(u & jnp.int32(-65536), F32)
    return jnp.concatenate([lo, hi], axis=1).astype(BF16)


def _out_kernel(*refs, nl, has_ctx):
    if has_ctx:
        (a_lat, a_ctx, x_lat, x_ctx, m_ref, w_ref, wr_ref, br_ref, tri_ref,
         x1_ref, h2_ref, e_ref, k_ref, g_ref, c_ref) = refs
        is_lat = pl.program_id(0) < nl
    else:
        a_lat, x_lat, m_ref, w_ref, wr_ref, br_ref, tri_ref, x1_ref, h2_ref, e_ref, k_ref, g_ref, c_ref = refs
    picks = []
    for a, b in _row_halves(x1_ref.shape[0]):
        lhs = _load_rows(is_lat, a_lat, a_ctx, a, b) if has_ctx else a_lat[a:b, :]
        x = _load_rows(is_lat, x_lat, x_ctx, a, b) if has_ctx else x_lat[a:b, :]
        x1 = x + m_ref[0, 2:3, :] * _dot(lhs, w_ref[...])
        x1_ref[a:b, :] = x1
        h2 = _modulate(x1, m_ref[0, 3:4, :], m_ref[0, 4:5, :])
        h2_ref[a:b, :] = h2.astype(BF16)
        picks.append(_route(h2, wr_ref, br_ref))
    lane, i1, i2, ga, gb = (jnp.concatenate(v, axis=0) for v in zip(*picks))
    p1, p2, runs = _place_in_tile(lane, i1, i2, tri_ref)
    e_ref[...] = jnp.concatenate([i1, i2], axis=1).astype(jnp.int32) - N_GROUPS
    k_ref[...] = jnp.concatenate([p1, p2], axis=1).astype(jnp.int32)
    g_ref[...] = jnp.where(lane == 2, ga, jnp.where(lane == 3, gb, 0.0))
    c_ref[0] = jnp.broadcast_to(runs, c_ref.shape[1:])


def _mixer_out(a_lat, a_ctx, x_lat, x_ctx, mod, w_out, wr, br, geo, nt):
    t_rows, nl, tpb, n_batch = geo
    tm = ROW_TILE
    kdim = w_out.shape[0]
    has_ctx = a_ctx is not None
    mrow = lambda i: (jnp.where(i < nl, i // tpb, n_batch), 0, 0)
    lat = lambda w: pl.BlockSpec((tm, w), lambda i: (jnp.minimum(i, nl - 1), 0))
    ctx = lambda w: pl.BlockSpec((tm, w), lambda i: (jnp.maximum(i - nl, 0), 0))
    if has_ctx:
        in_specs = [lat(kdim), ctx(kdim), lat(D_MODEL), ctx(D_MODEL)]
        args = [a_lat, a_ctx, x_lat, x_ctx]
    else:
        in_specs = [lat(kdim), lat(D_MODEL)]
        args = [a_lat, x_lat]
    in_specs += [
        pl.BlockSpec((1, N_MOD, D_MODEL), mrow),
        pl.BlockSpec((kdim, D_MODEL), lambda i: (0, 0)),
        pl.BlockSpec((D_MODEL, 2 * LANES), lambda i: (0, 0)),
        pl.BlockSpec((1, LANES), lambda i: (0, 0)),
        pl.BlockSpec((tm, tm), lambda i: (0, 0)),
    ]
    tri = (jnp.arange(tm)[:, None] > jnp.arange(tm)[None, :]).astype(BF16)
    args += [mod, w_out, wr, br, tri]
    rows = nt * tm
    row_out = lambda w: pl.BlockSpec((tm, w), lambda i: (i, 0))
    return pl.pallas_call(
        functools.partial(_out_kernel, nl=nl, has_ctx=has_ctx),
        grid=(nt,),
        in_specs=in_specs,
        out_specs=[row_out(D_MODEL), row_out(D_MODEL), row_out(2), row_out(2), row_out(LANES),
                   pl.BlockSpec((1, SUBLANES, LANES), lambda i: (i, 0, 0))],
        out_shape=[jax.ShapeDtypeStruct((rows, D_MODEL), F32),
                   jax.ShapeDtypeStruct((rows, D_MODEL), BF16),
                   jax.ShapeDtypeStruct((rows, 2), jnp.int32),
                   jax.ShapeDtypeStruct((rows, 2), jnp.int32),
                   jax.ShapeDtypeStruct((rows, LANES), F32),
                   jax.ShapeDtypeStruct((nt, SUBLANES, LANES), F32)],
        compiler_params=_cparams(1),
        name="mixer_out_ctx" if has_ctx else "mixer_out",
    )(*args)


def _router_weights(w_group, b_group, w_expert, b_expert):
    w = jnp.concatenate([w_group, w_expert], axis=1)
    w = jnp.pad(w, ((0, 0), (0, LANES - w.shape[1])))
    hi = w.astype(BF16)
    lo = (w - hi.astype(F32)).astype(BF16)
    b = jnp.pad(jnp.concatenate([b_group, b_expert]), (0, LANES - N_GROUPS - N_EXPERTS))
    return jnp.concatenate([hi, lo], axis=1), b.reshape(1, LANES).astype(F32)


def _slot_tables(runs, n_tok):
    blk = EXPERT_BLOCK
    nt = runs.shape[0]
    run = runs[:, 0, N_GROUPS:N_GROUPS + N_EXPERTS].astype(jnp.int32)
    total = jnp.sum(run, axis=0)
    padded = (total + blk - 1) // blk * blk
    ends_p = jnp.cumsum(padded)
    base = (ends_p - padded)[None, :] + jnp.cumsum(run, axis=0) - run
    ends_t = jnp.cumsum(run, axis=1)
    shift = base - (ends_t - run)
    shift_rows = jnp.pad(shift.astype(F32), ((0, 0), (N_GROUPS, LANES - N_GROUPS - N_EXPERTS)))[:, None, :]
    chunk_row = jnp.arange(SORT_ROWS // SEG_ALIGN, dtype=jnp.int32) * SEG_ALIGN
    starts_t = ends_t - run
    owns = (starts_t[:, None, :] <= chunk_row[None, :, None]) & (chunk_row[None, :, None] < ends_t[:, None, :])
    chunk_shift = jnp.sum(jnp.where(owns, shift[:, None, :], 0), axis=-1)[:, None, :]
    n_chunks = ends_t[:, -1] // SEG_ALIGN
    n_blocks = -(-(2 * n_tok + nt * N_EXPERTS * (SEG_ALIGN - 1)) // blk) + N_EXPERTS
    first_slot = jnp.arange(n_blocks, dtype=jnp.int32) * blk
    block_e = jnp.minimum(jnp.sum(ends_p[None, :] <= first_slot[:, None], axis=1), N_EXPERTS - 1).astype(jnp.int32)
    n_valid = (ends_p[-1] // blk).astype(jnp.int32).reshape(1)
    pad_tab = jnp.concatenate([ends_p - padded + total, (padded - total) // SEG_ALIGN]).astype(jnp.int32)
    return n_chunks, chunk_shift, shift_rows, pad_tab, block_e, n_valid, n_blocks


def _dispatch_kernel(nchunk_ref, pad_ref, cshift_ref, h_ref, e_ref, pos_ref, srow_ref,
                     o_ref, dest_ref, perm_ref, sorted_ref, zero_ref, sem):
    tm = h_ref.shape[0]
    i = pl.program_id(0)
    pos = pos_ref[...].astype(F32)
    lane = lax.broadcasted_iota(jnp.int32, (tm, LANES), 1).astype(F32)

    e_lane = e_ref[...].astype(F32) + N_GROUPS
    dest = [jnp.sum(jnp.where(lane == e_lane[:, k:k + 1], srow_ref[0], 0.0), axis=-1, keepdims=True)
            + pos[:, k:k + 1] for k in range(2)]
    dest_ref[...] = jnp.concatenate(dest, axis=1).astype(jnp.int32)

    pos_row = [jnp.transpose(jnp.where(lane == 0.0, pos[:, k:k + 1], 0.0))[0:1, :] for k in range(2)]
    for s0 in range(0, SORT_ROWS, SORT_BLOCK):
        s = (lax.broadcasted_iota(jnp.int32, (SORT_BLOCK, tm), 0) + s0).astype(F32)
        perm_ref[s0:s0 + SORT_BLOCK, :] = jnp.where((s == pos_row[0]) | (s == pos_row[1]), 1.0, 0.0).astype(BF16)
    half = D_MODEL // 2
    lo = lax.bitcast_convert_type(_dot(perm_ref[...], h_ref[:, :half]), jnp.int32)
    hi = lax.bitcast_convert_type(_dot(perm_ref[...], h_ref[:, half:]), jnp.int32)
    sorted_ref[...] = lax.shift_right_logical(lo, 16) | (hi & jnp.int32(-65536))

    def chunk_copy(c):
        r = pl.multiple_of(c * SEG_ALIGN, SEG_ALIGN)
        d = pl.multiple_of(r + cshift_ref[0, 0, c], SEG_ALIGN)
        return pltpu.make_async_copy(sorted_ref.at[pl.ds(r, SEG_ALIGN), :], o_ref.at[pl.ds(d, SEG_ALIGN), :], sem)

    def start(c, carry):
        chunk_copy(c).start()
        return carry

    n = nchunk_ref[i]

    def start_pair(c, carry):
        chunk_copy(2 * c).start(priority=0)
        chunk_copy(2 * c + 1).start(priority=1)
        return carry

    lax.fori_loop(0, n // 2, start_pair, 0)
    lax.fori_loop(2 * (n // 2), n, start, 0)

    def wait_rows(rows):
        def body(c, carry):
            pltpu.make_async_copy(sorted_ref.at[pl.ds(0, rows), :], o_ref.at[pl.ds(0, rows), :], sem).wait()
            return carry
        return body

    lax.fori_loop(0, n // WAIT_GROUP, wait_rows(WAIT_GROUP * SEG_ALIGN), 0)
    lax.fori_loop(0, n % WAIT_GROUP, wait_rows(SEG_ALIGN), 0)

    @pl.when(i == pl.num_programs(0) - 1)
    def _():
        zero_ref[...] = jnp.zeros_like(zero_ref)

        def for_each_pad_chunk(fn):
            for e in range(N_EXPERTS):
                first = pad_ref[e]

                def body(c, carry):
                    d = pl.multiple_of(first + c * SEG_ALIGN, SEG_ALIGN)
                    fn(pltpu.make_async_copy(zero_ref, o_ref.at[pl.ds(d, SEG_ALIGN), :], sem))
                    return carry

                lax.fori_loop(0, pad_ref[N_EXPERTS + e], body, 0)

        for_each_pad_chunk(lambda cp: cp.start())
        for_each_pad_chunk(lambda cp: cp.wait())


def _dispatch(h2, e_ids, pos, n_chunks, chunk_shift, shift_rows, pad_tab, n_slots):
    n_tok = h2.shape[0]
    tm = ROW_TILE
    nt = n_tok // tm
    width = D_MODEL // 2
    return pl.pallas_call(
        _dispatch_kernel,
        grid_spec=pltpu.PrefetchScalarGridSpec(
            num_scalar_prefetch=2,
            grid=(nt,),
            in_specs=[pl.BlockSpec((1, 1, SORT_ROWS // SEG_ALIGN), lambda i, n, p: (i, 0, 0), memory_space=pltpu.SMEM),
                      pl.BlockSpec((tm, D_MODEL), lambda i, n, p: (i, 0)),
                      pl.BlockSpec((tm, 2), lambda i, n, p: (i, 0)),
                      pl.BlockSpec((tm, 2), lambda i, n, p: (i, 0)),
                      pl.BlockSpec((1, 1, LANES), lambda i, n, p: (i, 0, 0))],
            out_specs=[pl.BlockSpec(memory_space=pl.ANY),
                       pl.BlockSpec((tm, 2), lambda i, n, p: (i, 0))],
            scratch_shapes=[pltpu.VMEM((SORT_ROWS, tm), BF16), pltpu.VMEM((SORT_ROWS, width), jnp.int32),
                            pltpu.VMEM((SEG_ALIGN, width), jnp.int32), pltpu.SemaphoreType.DMA(())],
        ),
        out_shape=[jax.ShapeDtypeStruct((n_slots, width), jnp.int32),
                   jax.ShapeDtypeStruct((n_tok, 2), jnp.int32)],
        compiler_params=_cparams(1),
        name="dispatch",
    )(n_chunks, pad_tab, chunk_shift, h2, e_ids, pos, shift_rows)


def _expert_kernel(be_ref, nv_ref, x_ref, wg_ref, wu_ref, wd_ref, o_ref, wg_b, wu_b, wd_b):
    i = pl.program_id(0)
    valid = i < nv_ref[0]
    new_expert = (i == 0) | (be_ref[i] != be_ref[jnp.maximum(i - 1, 0)])

    @pl.when(valid & new_expert)
    def _():
        wg_b[...] = wg_ref[0, 0].astype(BF16)
        wu_b[...] = wu_ref[0, 0].astype(BF16)
        wd_b[...] = wd_ref[0, 0].astype(BF16)

    @pl.when(valid)
    def _():
        x = _unpack_bf16_pairs(x_ref[...])
        mid = (_silu(_dot(x, wg_b[...])) * _dot(x, wu_b[...])).astype(BF16)
        for a, b in _row_halves(o_ref.shape[1]):
            o_ref[:, a:b] = _dot(mid, wd_b[:, a:b]).astype(BF16)

    @pl.when(jnp.logical_not(valid))
    def _():
        o_ref[...] = jnp.zeros_like(o_ref)


def _expert_ffn(xs_sorted, block_e, n_valid, w_gate, w_up, w_down, layer):
    n_slots = xs_sorted.shape[0]
    blk = EXPERT_BLOCK
    wspec = lambda r, c: pl.BlockSpec((1, 1, r, c), lambda i, be, nv: (layer, be[i], 0, 0))
    return pl.pallas_call(
        _expert_kernel,
        grid_spec=pltpu.PrefetchScalarGridSpec(
            num_scalar_prefetch=2,
            grid=(n_slots // blk,),
            in_specs=[
                pl.BlockSpec((blk, D_MODEL // 2), lambda i, be, nv: (i, 0)),
                wspec(D_MODEL, EXPERT_HIDDEN), wspec(D_MODEL, EXPERT_HIDDEN), wspec(EXPERT_HIDDEN, D_MODEL),
            ],
            out_specs=pl.BlockSpec((blk, D_MODEL), lambda i, be, nv: (i, 0)),
            scratch_shapes=[pltpu.VMEM((D_MODEL, EXPERT_HIDDEN), BF16),
                            pltpu.VMEM((D_MODEL, EXPERT_HIDDEN), BF16),
                            pltpu.VMEM((EXPERT_HIDDEN, D_MODEL), BF16)],
        ),
        out_shape=jax.ShapeDtypeStruct((n_slots, D_MODEL), BF16),
        compiler_params=_cparams(1),
        name="expert_ffn",
    )(block_e, n_valid, xs_sorted, w_gate, w_up, w_down)


def _moe(h2, e_ids, pos, runs, n_tok, w_gate, w_up, w_down, layer):
    n_chunks, chunk_shift, shift_rows, pad_tab, block_e, n_valid, n_blocks = _slot_tables(runs, n_tok)
    xs_sorted, dest = _dispatch(h2, e_ids, pos, n_chunks, chunk_shift, shift_rows, pad_tab, n_blocks * EXPERT_BLOCK)
    ys = _expert_ffn(xs_sorted, block_e, n_valid, w_gate, w_up, w_down, layer)
    return ys[dest[:, 0]], ys[dest[:, 1]]


def _final_kernel(x1_ref, ya_ref, yb_ref, r_ref, m_ref, gain_ref, o_ref):
    r = r_ref[...]
    f = r[:, 2:3] * ya_ref[...].astype(F32) + r[:, 3:4] * yb_ref[...].astype(F32)
    x2 = x1_ref[...] + m_ref[0, 5:6, :] * f
    ms = jnp.mean(x2 * x2, axis=-1, keepdims=True)
    o_ref[...] = x2 * lax.rsqrt(ms + NORM_EPS) * gain_ref[...]


def _final(x1, ya, yb, route, mod, gain, geo, nt):
    t_rows, nl, tpb, n_batch = geo
    tm = ROW_TILE
    rows = lambda w: pl.BlockSpec((tm, w), lambda i: (i, 0))
    return pl.pallas_call(
        _final_kernel,
        grid=(nt,),
        in_specs=[rows(D_MODEL), rows(D_MODEL), rows(D_MODEL), rows(LANES),
                  pl.BlockSpec((1, N_MOD, D_MODEL), lambda i: (jnp.where(i < nl, i // tpb, n_batch), 0, 0)),
                  pl.BlockSpec((1, D_MODEL), lambda i: (0, 0))],
        out_specs=rows(D_MODEL),
        out_shape=jax.ShapeDtypeStruct((nt * tm, D_MODEL), F32),
        compiler_params=_cparams(1),
        name="combine_final",
    )(x1, ya, yb, route, mod, gain.reshape(1, D_MODEL))


def _proj_att_kernel(x1_ref, ya_ref, yb_ref, r_ref, m0_ref, m_ref, w_ref,
                     qc_ref, qa_ref, qb_ref, kc_ref, ka_ref, kb_ref, o_ref, x2_ref, h_ref):
    for a, b in _row_halves(h_ref.shape[0]):
        r = r_ref[a:b, :]
        f = r[:, 2:3] * ya_ref[a:b, :].astype(F32) + r[:, 3:4] * yb_ref[a:b, :].astype(F32)
        x2 = x1_ref[a:b, :] + m0_ref[0, 5:6, :] * f
        x2_ref[a:b, :] = x2
        h_ref[a:b, :] = _modulate(x2, m_ref[0, 0:1, :], m_ref[0, 1:2, :]).astype(BF16)

    ones = jnp.ones((LANES, LANES), BF16)

    def normed_rope(xs, c_ref, a_ref, b_ref):
        rs = lax.rsqrt(_dot((xs * xs).astype(BF16), ones) * (1.0 / ATT_HEAD_DIM) + NORM_EPS)
        r = (xs * c_ref[...] + pltpu.roll(xs, LANES - ATT_HEAD_DIM // 4, 1) * a_ref[...]
             + pltpu.roll(xs, ATT_HEAD_DIM // 4, 1) * b_ref[...])
        return (r * rs).astype(BF16)

    gw = PROJ_GROUP
    for c0 in range(0, ATT_IN, gw):
        acc = _dot(h_ref[...], w_ref[:, c0:c0 + gw])
        for g in range(gw // LANES):
            col = c0 + g * LANES
            xs = acc[:, g * LANES:(g + 1) * LANES]
            if col < ATT_QW:
                o_ref[:, col:col + LANES] = normed_rope(xs, qc_ref, qa_ref, qb_ref)
            elif col < ATT_QW + ATT_KW:
                o_ref[:, col:col + LANES] = normed_rope(xs, kc_ref, ka_ref, kb_ref)
            else:
                o_ref[:, col:col + LANES] = xs.astype(BF16)


def _proj_att(x1, ya, yb, route, mod_prev, mod, w_qkv, q_tabs, k_tabs, geo):
    t_rows, nl, tpb, n_batch = geo
    tm = ROW_TILE
    nt = t_rows // tm
    rope = pl.BlockSpec((tm, ATT_HEAD_DIM), lambda i: (jnp.where(i < nl, i % tpb, tpb), 0))
    rows = lambda w: pl.BlockSpec((tm, w), lambda i: (i, 0))
    mrow = pl.BlockSpec((1, N_MOD, D_MODEL), lambda i: (jnp.where(i < nl, i // tpb, n_batch), 0, 0))
    return pl.pallas_call(
        _proj_att_kernel,
        grid=(nt,),
        in_specs=[
            rows(D_MODEL), rows(D_MODEL), rows(D_MODEL), rows(LANES), mrow, mrow,
            pl.BlockSpec((D_MODEL, ATT_IN), lambda i: (0, 0)),
            rope, rope, rope, rope, rope, rope,
        ],
        out_specs=[rows(ATT_IN), rows(D_MODEL)],
        out_shape=[jax.ShapeDtypeStruct((t_rows, ATT_IN), BF16),
                   jax.ShapeDtypeStruct((t_rows, D_MODEL), F32)],
        scratch_shapes=[pltpu.VMEM((tm, D_MODEL), BF16)],
        compiler_params=_cparams(1),
        name="proj_att",
    )(x1, ya, yb, route, mod_prev, mod, w_qkv, *q_tabs, *k_tabs)


def _att_kernel(q_ref, kl_ref, vl_ref, kc_ref, vc_ref, o_ref, k_all, v_ext, s_ref, p_ref):
    tq = q_ref.shape[0]
    seq, ctx_len = kl_ref.shape[0], kc_ref.shape[0]

    @pl.when(pl.program_id(2) == 0)
    def _():
        k_all[0:seq, :] = kl_ref[...]
        k_all[seq:seq + ctx_len, :] = kc_ref[...]
        v_ext[0:seq, 0:LANES] = vl_ref[...]
        v_ext[seq:seq + ctx_len, 0:LANES] = vc_ref[...]
        v_ext[:, LANES:] = jnp.ones((seq + ctx_len, LANES), BF16)

    rows = ATT_SOFTMAX_ROWS

    def scores(g):
        s_ref[g * tq:(g + 1) * tq, :] = _dot_nt(q_ref[:, g * LANES:(g + 1) * LANES], k_all[...])

    def softmax(g):
        for r0 in range(g * tq, (g + 1) * tq, rows):
            m = jnp.max(s_ref[r0:r0 + rows, :], axis=-1, keepdims=True)
            p_ref[r0:r0 + rows, :] = jnp.exp2(s_ref[r0:r0 + rows, :] - m).astype(BF16)

    def weighted_values(g):
        oe = _dot(p_ref[g * tq:(g + 1) * tq, :], v_ext[...])
        o_ref[:, g * LANES:(g + 1) * LANES] = (oe[:, :LANES] / oe[:, LANES:]).astype(BF16)

    scores(0)
    scores(1)
    for g in range(ATT_GROUP):
        if g + 2 < ATT_GROUP:
            scores(g + 2)
        softmax(g)
        weighted_values(g)


def _attention(p, n_batch, seq, ctx_len):
    tq = ATT_Q_TILE
    nq = seq // tq
    cb = n_batch * seq // ctx_len
    ko, vo = ATT_QW // LANES, (ATT_QW + ATT_KW) // LANES
    gw = ATT_GROUP * ATT_HEAD_DIM
    n_keys = seq + ctx_len
    return pl.pallas_call(
        _att_kernel,
        grid=(n_batch, ATT_KV_HEADS, nq),
        in_specs=[
            pl.BlockSpec((tq, gw), lambda b, h, t: (b * nq + t, h)),
            pl.BlockSpec((seq, LANES), lambda b, h, t: (b, ko + h)),
            pl.BlockSpec((seq, LANES), lambda b, h, t: (b, vo + h)),
            pl.BlockSpec((ctx_len, LANES), lambda b, h, t: (cb + b, ko + h)),
            pl.BlockSpec((ctx_len, LANES), lambda b, h, t: (cb + b, vo + h)),
        ],
        out_specs=pl.BlockSpec((tq, gw), lambda b, h, t: (b * nq + t, h)),
        out_shape=jax.ShapeDtypeStruct((n_batch * seq, ATT_QW), BF16),
        scratch_shapes=[pltpu.VMEM((n_keys, LANES), BF16), pltpu.VMEM((n_keys, 2 * LANES), BF16),
                        pltpu.VMEM((ATT_GROUP * tq, n_keys), F32), pltpu.VMEM((ATT_GROUP * tq, n_keys), BF16)],
        compiler_params=_cparams(3),
        name="attention",
    )(p, p, p, p, p)


def kernel(x, c, ctx, c_ctx, w_mod, b_mod, ret_w_in, ret_w_out, ret_log_decay_fwd, ret_log_decay_bwd, att_w_qkv, att_w_o, att_q_gain, att_k_gain, moe_w_group, moe_b_group, moe_w_expert, moe_b_expert, moe_w_gate, moe_w_up, moe_w_down, final_norm_gain):
    n_batch, seq, d = x.shape
    ctx_len = ctx.shape[1]
    tm = ROW_TILE
    assert d == D_MODEL and w_mod.shape[0] == DEPTH == 2
    assert seq % tm == 0 and (n_batch * ctx_len) % tm == 0 and seq % GRID_W == 0
    assert seq % RET_CHUNK == 0 and ctx_len % RET_CHUNK == 0 and (n_batch * seq) % ctx_len == 0
    n_lat = n_batch * seq
    t_rows = n_lat + n_batch * ctx_len
    nl = n_lat // tm
    geo = (t_rows, nl, seq // tm, n_batch)
    nt = t_rows // tm

    x_lat, x_ctx = x.reshape(n_lat, d), ctx.reshape(n_batch * ctx_len, d)
    pad_rows = -(n_batch + 1) % MOD_ROWS_PAD
    c_rows = jnp.concatenate([c, c_ctx[None, :], jnp.zeros((pad_rows, d), F32)], axis=0)
    mod = _mod_vectors(c_rows, w_mod, b_mod)

    cos, sin = _ret_rope_tables(seq, tm)
    p = _proj_ret(x_lat, x_ctx, mod[0], ret_w_in[0].astype(BF16), cos, sin, geo)
    lg = jnp.stack([ret_log_decay_fwd[0], ret_log_decay_bwd[0]]).astype(F32)
    y_lat, y_ctx = _retention(p, lg, n_batch, seq, ctx_len)
    wr, br = _router_weights(moe_w_group[0], moe_b_group[0], moe_w_expert[0], moe_b_expert[0])
    x1, h2, e_ids, pos, route, runs = _mixer_out(
        y_lat, y_ctx, x_lat, x_ctx, mod[0], ret_w_out[0].astype(BF16), wr, br, geo, nt)
    ya, yb = _moe(h2, e_ids, pos, runs, t_rows, moe_w_gate, moe_w_up, moe_w_down, 0)

    q_tabs = _att_rope_tables(seq, tm, att_q_gain[0], ATT_HEAD_DIM ** -0.5 * LOG2_E)
    k_tabs = _att_rope_tables(seq, tm, att_k_gain[0], 1.0)
    p, xs = _proj_att(x1, ya, yb, route, mod[0], mod[1], att_w_qkv[0].astype(BF16), q_tabs, k_tabs, geo)
    o = _attention(p, n_batch, seq, ctx_len)
    wr, br = _router_weights(moe_w_group[1], moe_b_group[1], moe_w_expert[1], moe_b_expert[1])
    x1, h2, e_ids, pos, route, runs = _mixer_out(
        o, None, xs, None, mod[1], att_w_o[0].astype(BF16), wr, br, geo, nl)
    ya, yb = _moe(h2, e_ids, pos, runs, n_lat, moe_w_gate, moe_w_up, moe_w_down, 1)
    out = _final(x1, ya, yb, route, mod[1], final_norm_gain, geo, nl)
    return out.reshape(n_batch, seq, d)
```
